```python
import math
import jax, jax.numpy as jnp
from jax import lax
import numpy as np

D_MODEL = 2048
BATCH = 4
SEQ = 4096
DEPTH = 2

SB_HEAD_DIM = 128
SB_HEADS = (D_MODEL // 2) // SB_HEAD_DIM
SB_WIDTH = SB_HEADS * SB_HEAD_DIM
SB_BLOCK = 128
GDN_HEAD_DIM = 128
GDN_HEADS = (D_MODEL // 2) // GDN_HEAD_DIM
GDN_WIDTH = GDN_HEADS * GDN_HEAD_DIM
GDN_CONV = 4
GDN_CHUNK = 64
MIX_WIDTH = SB_WIDTH + GDN_WIDTH
OFF_SB_Q = 0
OFF_SB_K = OFF_SB_Q + SB_WIDTH
OFF_SB_V = OFF_SB_K + SB_WIDTH
OFF_GDN_QKV = OFF_SB_V + SB_WIDTH
OFF_GDN_Z = OFF_GDN_QKV + 3 * GDN_WIDTH
OFF_GDN_A = OFF_GDN_Z + GDN_WIDTH
OFF_GDN_B = OFF_GDN_A + GDN_HEADS
IN_COLS = OFF_GDN_B + GDN_HEADS
N_GROUPS = 4
EXPERTS_PER_GROUP = 8
N_EXPERTS = N_GROUPS * EXPERTS_PER_GROUP
TOP_K = 2
D_EXPERT = D_MODEL // 4
MOE_BLOCK = 128
RMS_EPS = 1e-6

kernel_name = "hybrid_stickbreak_gdn_hmoe"


def rms_norm(x, w):
    xf = x.astype(jnp.float32)
    y = xf * lax.rsqrt(jnp.mean(xf * xf, axis=-1, keepdims=True) + RMS_EPS)
    return (y * w.astype(jnp.float32)).astype(x.dtype)


def l2_normalize(x):
    xf = x.astype(jnp.float32)
    return xf * lax.rsqrt(jnp.sum(xf * xf, axis=-1, keepdims=True) + RMS_EPS)


def stick_breaking_attention(q, k, v):
    b_, h_, s_, d = q.shape
    nq = s_ // SB_BLOCK
    qf, kf, vf = (t.astype(jnp.float32) for t in (q, k, v))
    q_blocks = qf.reshape(b_, h_, nq, SB_BLOCK, d).transpose(2, 0, 1, 3, 4)
    key_pos = jnp.arange(s_)
    scale = 1.0 / math.sqrt(d)

    def one_block(args):
        qb, i = args
        z = jnp.einsum('bhqd,bhkd->bhqk', qb, kf) * scale
        q_pos = i * SB_BLOCK + jnp.arange(SB_BLOCK)
        causal = key_pos[None, :] < q_pos[:, None]
        log_1mb = jnp.where(causal, jax.nn.log_sigmoid(-z), 0.0)
        after = lax.cumsum(log_1mb, axis=3, reverse=True) - log_1mb
        weights = jnp.where(causal, jnp.exp(jax.nn.log_sigmoid(z) + after), 0.0)
        return jnp.einsum('bhqk,bhkd->bhqd', weights, vf)

    out = lax.map(one_block, (q_blocks, jnp.arange(nq)))
    return out.transpose(1, 2, 0, 3, 4).reshape(b_, h_, s_, d).astype(q.dtype)


def causal_depthwise_conv(x, w):
    kw = w.shape[0]
    return lax.conv_general_dilated(
        x, w[:, None, :], window_strides=(1,), padding=((kw - 1, 0),),
        dimension_numbers=('NWC', 'WIO', 'NWC'), feature_group_count=x.shape[-1])


def chunk_gated_delta_rule(q, k, v, g, beta):
    b_, h_, s_, dk = q.shape
    dv = v.shape[-1]
    c = GDN_CHUNK
    nc = s_ // c
    q = q * (1.0 / math.sqrt(dk))
    q, k, v = (t.reshape(b_, h_, nc, c, t.shape[-1]) for t in (q, k, v))
    g = g.reshape(b_, h_, nc, c)
    beta = beta.reshape(b_, h_, nc, c)
    gc = jnp.cumsum(g, axis=-1)
    incl = jnp.tril(jnp.ones((c, c), dtype=bool))
    strict = jnp.tril(jnp.ones((c, c), dtype=bool), -1)
    diff = gc[..., :, None] - gc[..., None, :]
    decay = jnp.where(incl, jnp.exp(jnp.where(incl, diff, 0.0)), 0.0)
    k_beta = k * beta[..., None]
    v_beta = v * beta[..., None]
    lmat = jnp.where(strict, jnp.einsum('bhncd,bhnkd->bhnck', k_beta, k) * decay, 0.0)
    eye = jnp.eye(c, dtype=q.dtype)
    t_inv = lax.linalg.triangular_solve(lmat + eye, jnp.broadcast_to(eye, lmat.shape),
                                        left_side=True, lower=True)
    u = t_inv @ v_beta
    w = t_inv @ (k_beta * jnp.exp(gc)[..., None])
    attn_intra = jnp.einsum('bhncd,bhnkd->bhnck', q, k) * decay

    def step(state, xs):
        q_i, k_i, u_i, w_i, gc_i, attn_i = xs
        v_new = u_i - w_i @ state
        o = (q_i * jnp.exp(gc_i)[..., None]) @ state + attn_i @ v_new
        g_last = gc_i[..., -1]
        k_dec = k_i * jnp.exp(g_last[..., None] - gc_i)[..., None]
        state = state * jnp.exp(g_last)[..., None, None] + jnp.einsum('bhck,bhcv->bhkv', k_dec, v_new)
        return state, o

    xs = tuple(jnp.moveaxis(t, 2, 0) for t in (q, k, u, w, gc, attn_intra))
    state0 = jnp.zeros((b_, h_, dk, dv), q.dtype)
    _, o = lax.scan(step, state0, xs)
    return jnp.moveaxis(o, 0, 2).reshape(b_, h_, s_, dv)


def hybrid_mixer(h, w_in, q_norm_w, k_norm_w, conv_w, a_log, dt_bias, out_norm_w, w_o):
    b_, s_, _ = h.shape
    proj = h @ w_in

    def heads(t, n):
        return t.reshape(b_, s_, n, -1).transpose(0, 2, 1, 3)

    sb_q = rms_norm(heads(proj[..., OFF_SB_Q:OFF_SB_K], SB_HEADS), q_norm_w)
    sb_k = rms_norm(heads(proj[..., OFF_SB_K:OFF_SB_V], SB_HEADS), k_norm_w)
    sb_v = heads(proj[..., OFF_SB_V:OFF_GDN_QKV], SB_HEADS)
    o_sb = stick_breaking_attention(sb_q, sb_k, sb_v)
    o_sb = o_sb.transpose(0, 2, 1, 3).reshape(b_, s_, SB_WIDTH)

    qkv = jax.nn.silu(causal_depthwise_conv(proj[..., OFF_GDN_QKV:OFF_GDN_Z], conv_w))
    gq = l2_normalize(heads(qkv[..., :GDN_WIDTH], GDN_HEADS))
    gk = l2_normalize(heads(qkv[..., GDN_WIDTH:2 * GDN_WIDTH], GDN_HEADS))
    gv = heads(qkv[..., 2 * GDN_WIDTH:], GDN_HEADS).astype(jnp.float32)
    gz = proj[..., OFF_GDN_Z:OFF_GDN_A].reshape(b_, s_, GDN_HEADS, GDN_HEAD_DIM)
    ga = proj[..., OFF_GDN_A:OFF_GDN_B].astype(jnp.float32)
    gb = proj[..., OFF_GDN_B:IN_COLS].astype(jnp.float32)
    beta = jax.nn.sigmoid(gb).transpose(0, 2, 1)
    log_decay = -(jnp.exp(a_log.astype(jnp.float32))
                  * jax.nn.softplus(ga + dt_bias.astype(jnp.float32))).transpose(0, 2, 1)
    o_gdn = chunk_gated_delta_rule(gq, gk, gv, log_decay, beta)
    o_gdn = o_gdn.transpose(0, 2, 1, 3)
    o_gdn = rms_norm(o_gdn, out_norm_w) * jax.nn.silu(gz.astype(jnp.float32))
    o_gdn = o_gdn.reshape(b_, s_, GDN_WIDTH).astype(h.dtype)

    return jnp.concatenate([o_sb, o_gdn], axis=-1) @ w_o


def hierarchical_moe(h, w_group, b_group, w_expert, b_expert, w1, w3, w2):
    b_, s_, d_ = h.shape
    xt = h.reshape(-1, d_)
    n_tok = xt.shape[0]
    group_logits = (xt @ w_group + b_group).astype(jnp.float32)
    group_prob = jax.nn.softmax(group_logits, axis=-1)
    top_group = jnp.argmax(group_logits, axis=-1)
    group_gate = jnp.take_along_axis(group_prob, top_group[:, None], axis=1)
    expert_logits = (xt @ w_expert + b_expert).astype(jnp.float32).reshape(n_tok, N_GROUPS, EXPERTS_PER_GROUP)
    in_group = jnp.take_along_axis(expert_logits, top_group[:, None, None], axis=1)[:, 0]
    top_p, top_local = lax.top_k(jax.nn.softmax(in_group, axis=-1), TOP_K)
    gate = group_gate * top_p / jnp.sum(top_p, axis=-1, keepdims=True)
    expert_id = top_group[:, None] * EXPERTS_PER_GROUP + top_local

    n_assign = n_tok * TOP_K
    flat_e = expert_id.reshape(-1)
    flat_tok = jnp.repeat(jnp.arange(n_tok, dtype=jnp.int32), TOP_K)
    order = jnp.argsort(flat_e)
    sorted_e = flat_e[order]
    sorted_tok = flat_tok[order]
    sorted_gate = gate.reshape(-1)[order]
    counts = jnp.bincount(flat_e, length=N_EXPERTS)
    padded = ((counts + MOE_BLOCK - 1) // MOE_BLOCK) * MOE_BLOCK
    start = jnp.cumsum(counts) - counts
    padded_end = jnp.cumsum(padded)
    padded_start = padded_end - padded
    dest = padded_start[sorted_e] + jnp.arange(n_assign) - start[sorted_e]
    n_blocks = (n_assign + MOE_BLOCK - 1) // MOE_BLOCK + N_EXPERTS
    cap = n_blocks * MOE_BLOCK
    slot_tok = jnp.full((cap,), n_tok, dtype=jnp.int32).at[dest].set(sorted_tok)
    x_pad = jnp.concatenate([xt, jnp.zeros((1, d_), xt.dtype)], axis=0)
    x_slots = x_pad[slot_tok].reshape(n_blocks, MOE_BLOCK, d_)
    block_expert = jnp.minimum(
        jnp.searchsorted(padded_end, jnp.arange(n_blocks) * MOE_BLOCK, side='right'), N_EXPERTS - 1)

    def expert_block(args):
        xb, e = args
        hid = jax.nn.silu(xb @ w1[e]) * (xb @ w3[e])
        return hid @ w2[e]

    y_slots = lax.map(expert_block, (x_slots, block_expert)).reshape(cap, d_)
    y_assign = y_slots[dest] * sorted_gate[:, None].astype(y_slots.dtype)
    y = jnp.zeros((n_tok, d_), h.dtype).at[sorted_tok].add(y_assign.astype(h.dtype))
    return y.reshape(b_, s_, d_)


def setup_inputs(seed: int = 0) -> dict:
    key = jax.random.key(seed)
    ks = jax.random.split(key, 20)
    L = DEPTH
    f32 = jnp.float32

    def nrm(k, shape, fan_in):
        return jax.random.normal(k, shape, f32) * (fan_in ** -0.5)

    def gain(k, shape):
        return 1.0 + 0.02 * jax.random.normal(k, shape, f32)

    dt = jnp.exp(jax.random.uniform(ks[8], (L, GDN_HEADS), f32, math.log(1e-3), math.log(1e-1)))
    return {
        "x": jax.random.normal(ks[0], (BATCH, SEQ, D_MODEL), f32),
        "norm1_w": gain(ks[1], (L, D_MODEL)),
        "w_in": nrm(ks[2], (L, D_MODEL, IN_COLS), D_MODEL),
        "sb_q_norm_w": gain(ks[3], (L, SB_HEAD_DIM)),
        "sb_k_norm_w": gain(ks[4], (L, SB_HEAD_DIM)),
        "gdn_conv_w": nrm(ks[5], (L, GDN_CONV, 3 * GDN_WIDTH), GDN_CONV),
        "gdn_a_log": jnp.log(jax.random.uniform(ks[6], (L, GDN_HEADS), f32, 1.0, 16.0)),
        "gdn_dt_bias": dt + jnp.log(-jnp.expm1(-dt)),
        "gdn_out_norm_w": gain(ks[7], (L, GDN_HEAD_DIM)),
        "w_o": nrm(ks[9], (L, MIX_WIDTH, D_MODEL), MIX_WIDTH),
        "norm2_w": gain(ks[10], (L, D_MODEL)),
        "w_group": nrm(ks[11], (L, D_MODEL, N_GROUPS), D_MODEL),
        "b_group": 0.01 * jax.random.normal(ks[12], (L, N_GROUPS), f32),
        "w_expert": nrm(ks[13], (L, D_MODEL, N_EXPERTS), D_MODEL),
        "b_expert": 0.01 * jax.random.normal(ks[14], (L, N_EXPERTS), f32),
        "w1": nrm(ks[15], (L, N_EXPERTS, D_MODEL, D_EXPERT), D_MODEL),
        "w3": nrm(ks[16], (L, N_EXPERTS, D_MODEL, D_EXPERT), D_MODEL),
        "w2": nrm(ks[17], (L, N_EXPERTS, D_EXPERT, D_MODEL), D_EXPERT),
    }


def reference(x, norm1_w, w_in, sb_q_norm_w, sb_k_norm_w, gdn_conv_w, gdn_a_log, gdn_dt_bias,
              gdn_out_norm_w, w_o, norm2_w, w_group, b_group, w_expert, b_expert, w1, w3, w2):
    for l in range(DEPTH):
        h = rms_norm(x, norm1_w[l])
        x = x + hybrid_mixer(h, w_in[l], sb_q_norm_w[l], sb_k_norm_w[l], gdn_conv_w[l],
                             gdn_a_log[l], gdn_dt_bias[l], gdn_out_norm_w[l], w_o[l])
        h = rms_norm(x, norm2_w[l])
        x = x + hierarchical_moe(h, w_group[l], b_group[l], w_expert[l], b_expert[l],
                                 w1[l], w3[l], w2[l])
    return x
```

```python
import functools
import math

import jax
import jax.numpy as jnp
from jax import lax
from jax.experimental import pallas as pl
from jax.experimental.pallas import tpu as pltpu

F32 = jnp.float32
BF16 = jnp.bfloat16
I32 = jnp.int32

HEAD_DIM = 128
GDN_CONV = 4
GDN_CHUNK = 64
N_GROUPS = 4
EXPERTS_PER_GROUP = 8
N_EXPERTS = N_GROUPS * EXPERTS_PER_GROUP
MOE_BLOCK = 128
RMS_EPS = 1e-6
ROUTER_ROWS = 48
VMEM_LIMIT = 56 * 1024 * 1024


def _cparams(sem, vmem=VMEM_LIMIT):
    return pltpu.CompilerParams(dimension_semantics=sem, vmem_limit_bytes=vmem)


def _dot(a, b, dims=(((1,), (0,)), ((), ()))):
    return lax.dot_general(a, b, dims, preferred_element_type=F32)


_NT = (((1,), (1,)), ((), ()))
_TN = (((0,), (0,)), ((), ()))


def _split(a):
    hi = a.astype(BF16)
    lo = (a - hi.astype(F32)).astype(BF16)
    return hi, lo


def _dot3(a, b, dims=(((1,), (0,)), ((), ()))):
    ah, al = _split(a)
    bh, bl = _split(b)
    return _dot(ah, bh, dims) + (_dot(ah, bl, dims) + _dot(al, bh, dims))


def _softplus(x):
    return jnp.maximum(x, 0.0) + jnp.log(1.0 + jnp.exp(-jnp.abs(x)))


def _sigmoid(x):
    return 1.0 / (1.0 + jnp.exp(-x))


def _inproj_kernel(x_ref, nw_ref, w_ref, wab_ref, qkw_ref, alog_ref, dt_ref,
                   proj_ref, g_ref, beta_ref, h_ref, *, n_qk_tiles, n_heads, rows):
    j = pl.program_id(1)
    tm = x_ref.shape[0]

    @pl.when(j == 0)
    def _():
        def norm_rows(r, carry):
            rs = pl.multiple_of(r * rows, rows)
            x = x_ref[pl.ds(rs, rows), :]
            ms = jnp.mean(x * x, axis=-1, keepdims=True)
            h = x * lax.rsqrt(ms + RMS_EPS) * nw_ref[...]
            h_ref[pl.ds(rs, rows), :] = h.astype(BF16)
            return carry
        lax.fori_loop(0, tm // rows, norm_rows, 0)
        ab = _dot(wab_ref[...], h_ref[...], _NT)
        ga = ab[0:n_heads]
        gb = ab[n_heads:2 * n_heads]
        g_ref[...] = -jnp.exp(alog_ref[...]) * _softplus(ga + dt_ref[...])
        beta_ref[...] = _sigmoid(gb)

    acc = _dot(h_ref[...], w_ref[...])

    @pl.when(j < n_qk_tiles)
    def _():
        wsel = jnp.where(j < n_qk_tiles // 2, qkw_ref[0:1, :], qkw_ref[1:2, :])
        for hh in range(acc.shape[1] // HEAD_DIM):
            sl = slice(hh * HEAD_DIM, (hh + 1) * HEAD_DIM)
            a = acc[:, sl]
            ms = jnp.mean(a * a, axis=-1, keepdims=True)
            proj_ref[:, sl] = (a * lax.rsqrt(ms + RMS_EPS) * wsel[:, sl]).astype(BF16)

    @pl.when(j >= n_qk_tiles)
    def _():
        proj_ref[...] = acc.astype(BF16)


def _inproj(x, norm_w, w_main, wab_t, qk_w, a_log, dt_bias, *, sb_width, tm, tn):
    T, D = x.shape
    N = w_main.shape[1]
    n_heads = a_log.shape[0]
    n_qk_tiles = 2 * sb_width // tn
    kern = functools.partial(_inproj_kernel, n_qk_tiles=n_qk_tiles, n_heads=n_heads,
                             rows=min(tm, 256))
    return pl.pallas_call(
        kern,
        grid=(T // tm, N // tn),
        in_specs=[
            pl.BlockSpec((tm, D), lambda i, j: (i, 0)),
            pl.BlockSpec((1, D), lambda i, j: (0, 0)),
            pl.BlockSpec((D, tn), lambda i, j: (0, j)),
            pl.BlockSpec((2 * n_heads, D), lambda i, j: (0, 0)),
            pl.BlockSpec((2, tn), lambda i, j: (0, 0)),
            pl.BlockSpec((n_heads, 1), lambda i, j: (0, 0)),
            pl.BlockSpec((n_heads, 1), lambda i, j: (0, 0)),
        ],
        out_specs=[
            pl.BlockSpec((tm, tn), lambda i, j: (i, j)),
            pl.BlockSpec((n_heads, tm), lambda i, j: (0, i)),
            pl.BlockSpec((n_heads, tm), lambda i, j: (0, i)),
        ],
        out_shape=[
            jax.ShapeDtypeStruct((T, N), BF16),
            jax.ShapeDtypeStruct((n_heads, T), F32),
            jax.ShapeDtypeStruct((n_heads, T), F32),
        ],
        scratch_shapes=[pltpu.VMEM((tm, D), BF16)],
        compiler_params=_cparams(("arbitrary", "arbitrary")),
        name="inproj",
    )(x, norm_w, w_main, wab_t, qk_w, a_log, dt_bias)


def _sb_kernel(q_ref, k_ref, v_ref, o_ref, *, tq, tk):
    i = pl.program_id(2)
    q = q_ref[0]
    n_kt = (i + 1) * (tq // tk)
    row = lax.broadcasted_iota(I32, (tk, tk), 0)
    col = lax.broadcasted_iota(I32, (tk, tk), 1)
    upper_incl = (row >= col).astype(BF16)
    q_pos = i * tq + lax.broadcasted_iota(I32, (tq, tk), 0)
    k_off = lax.broadcasted_iota(I32, (tq, tk), 1)

    def body(jj, carry):
        acc, c = carry
        j = n_kt - 1 - jj
        ks = pl.multiple_of(j * tk, tk)
        k = k_ref[0, pl.ds(ks, tk), :]
        v = v_ref[0, pl.ds(ks, tk), :]
        z = _dot(q, k, _NT)
        causal = (k_off + ks) < q_pos
        log_1mb = jnp.where(causal, -_softplus(z), 0.0)
        lh, ll = _split(log_1mb)
        incl = _dot(lh, upper_incl) + _dot(ll, upper_incl)
        w = jnp.where(causal, jnp.exp(z + incl + c), 0.0)
        acc = acc + _dot(w.astype(BF16), v)
        return acc, c + incl[:, 0:1]

    acc0 = jnp.zeros((tq, HEAD_DIM), F32)
    c0 = jnp.zeros((tq, 1), F32)
    acc, _ = lax.fori_loop(0, n_kt, body, (acc0, c0))
    o_ref[0] = acc.astype(o_ref.dtype)


def _sb_attention(proj3, *, n_heads, tq, tk):
    B, S, _ = proj3.shape
    kern = functools.partial(_sb_kernel, tq=tq, tk=tk)
    return pl.pallas_call(
        kern,
        grid=(B, n_heads, S // tq),
        in_specs=[
            pl.BlockSpec((1, tq, HEAD_DIM), lambda b, h, i: (b, i, h)),
            pl.BlockSpec((1, S, HEAD_DIM), lambda b, h, i: (b, 0, n_heads + h)),
            pl.BlockSpec((1, S, HEAD_DIM), lambda b, h, i: (b, 0, 2 * n_heads + h)),
        ],
        out_specs=pl.BlockSpec((1, tq, HEAD_DIM), lambda b, h, i: (b, i, h)),
        out_shape=jax.ShapeDtypeStruct((B, S, n_heads * HEAD_DIM), BF16),
        compiler_params=_cparams(("arbitrary", "arbitrary", "arbitrary")),
        name="sb_attention",
    )(proj3, proj3, proj3)


def _silu(x):
    return x * _sigmoid(x)


def _gdn_kernel(xq_ref, xk_ref, xv_ref, z_ref, wq_ref, wk_ref, wv_ref, g_ref, b_ref, ow_ref,
                o_ref, xs_ref, state_ref, *, tt):
    t = pl.program_id(2)
    C = GDN_CHUNK
    HALO = 8

    @pl.when(t == 0)
    def _():
        xs_ref[:, 0:HALO, :] = jnp.zeros((3, HALO, HEAD_DIM), F32)
        state_ref[...] = jnp.zeros_like(state_ref)

    @pl.when(t > 0)
    def _():
        xs_ref[:, 0:HALO, :] = xs_ref[:, tt:tt + HALO, :]

    xs_ref[0, HALO:, :] = xq_ref[0].astype(F32)
    xs_ref[1, HALO:, :] = xk_ref[0].astype(F32)
    xs_ref[2, HALO:, :] = xv_ref[0].astype(F32)

    def conv_silu(idx, w_ref):
        acc = None
        for kk in range(GDN_CONV):
            term = xs_ref[idx, pl.ds(HALO - GDN_CONV + 1 + kk, tt), :] * w_ref[kk:kk + 1, :]
            acc = term if acc is None else acc + term
        return _silu(acc)

    def l2n(a):
        return a * lax.rsqrt(jnp.sum(a * a, axis=-1, keepdims=True) + RMS_EPS)

    q_all = l2n(conv_silu(0, wq_ref)) * (1.0 / math.sqrt(HEAD_DIM))
    k_all = l2n(conv_silu(1, wk_ref))
    v_all = conv_silu(2, wv_ref)

    ri = lax.broadcasted_iota(I32, (C, C), 0)
    ci = lax.broadcasted_iota(I32, (C, C), 1)
    eye = ri == ci
    lower_incl = ri >= ci
    lower_strict = ri > ci
    blk16 = (ri >> 4) == (ci >> 4)
    blk32 = (ri >> 5) == (ci >> 5)
    eye_f = eye.astype(F32)

    def col_of(row):
        return jnp.sum(jnp.where(eye, jnp.broadcast_to(row, (C, C)), 0.0), axis=1, keepdims=True)

    n_chunks = tt // C
    pre = []
    for c in range(n_chunks):
        sl = slice(c * C, (c + 1) * C)
        q, k, v = q_all[sl], k_all[sl], v_all[sl]
        g_row = g_ref[0, 0, c:c + 1, :]
        b_row = b_ref[0, 0, c:c + 1, :]
        g_col = col_of(g_row)
        beta_col = col_of(b_row)
        gc_col = jnp.sum(jnp.where(lower_incl, jnp.broadcast_to(g_row, (C, C)), 0.0),
                         axis=1, keepdims=True)
        gc_row = jnp.sum(jnp.where(ri <= ci, jnp.broadcast_to(g_col, (C, C)), 0.0),
                         axis=0, keepdims=True)
        decay = jnp.where(lower_incl, jnp.exp(jnp.where(lower_incl, gc_col - gc_row, 0.0)), 0.0)
        k_beta = k * beta_col
        v_beta = v * beta_col
        lmat = jnp.where(lower_strict, _dot3(k_beta, k, _NT) * decay, 0.0)
        d16 = jnp.where(blk16, lmat, 0.0)
        p2 = _dot3(d16, d16)
        p4 = _dot3(p2, p2)
        p8 = _dot3(p4, p4)
        x0 = eye_f - d16
        x0 = x0 + _dot3(x0, p2)
        x0 = x0 + _dot3(x0, p4)
        x0 = x0 + _dot3(x0, p8)
        b1 = jnp.where(jnp.logical_and(blk32, jnp.logical_not(blk16)), lmat, 0.0)
        x1 = x0 - _dot3(_dot3(x0, b1), x0)
        b2 = jnp.where(blk32, 0.0, lmat)
        t_inv = x1 - _dot3(_dot3(x1, b2), x1)
        egc = jnp.exp(gc_col)
        u = _dot3(t_inv, v_beta)
        w = _dot3(t_inv, k_beta * egc)
        attn = jnp.where(lower_incl, _dot3(q, k, _NT) * decay, 0.0)
        g_last = gc_col[C - 1:C, :]
        k_dec = k * jnp.exp(g_last - gc_col)
        pre.append((u, w, q * egc, attn, k_dec, jnp.exp(g_last)))

    state = state_ref[...]
    for c in range(n_chunks):
        u, w, qg, attn, k_dec, eg_last = pre[c]
        v_new = u - _dot3(w, state)
        o = _dot3(qg, state) + _dot3(attn, v_new)
        state = state * eg_last + _dot3(k_dec, v_new, _TN)
        ms = jnp.mean(o * o, axis=-1, keepdims=True)
        zc = z_ref[0, c * C:(c + 1) * C, :].astype(F32)
        o_ref[0, c * C:(c + 1) * C, :] = (o * lax.rsqrt(ms + RMS_EPS) * ow_ref[...] * _silu(zc)).astype(o_ref.dtype)
    state_ref[...] = state


def _gdn(proj3, conv_w, g4, beta4, out_norm_w, *, n_heads, col0, tt):
    B, S, _ = proj3.shape
    kern = functools.partial(_gdn_kernel, tt=tt)
    cb = col0 // HEAD_DIM
    nct = tt // GDN_CHUNK
    xspec = lambda off: pl.BlockSpec((1, tt, HEAD_DIM), lambda b, h, t: (b, t, cb + off * n_heads + h))
    wspec = lambda off: pl.BlockSpec((GDN_CONV, HEAD_DIM), lambda b, h, t: (0, off * n_heads + h))
    gspec = pl.BlockSpec((1, 1, nct, GDN_CHUNK), lambda b, h, t: (h, b, t, 0))
    return pl.pallas_call(
        kern,
        grid=(B, n_heads, S // tt),
        in_specs=[xspec(0), xspec(1), xspec(2), xspec(3), wspec(0), wspec(1), wspec(2),
                  gspec, gspec, pl.BlockSpec((1, HEAD_DIM), lambda b, h, t: (0, 0))],
        out_specs=pl.BlockSpec((1, tt, HEAD_DIM), lambda b, h, t: (b, t, h)),
        out_shape=jax.ShapeDtypeStruct((B, S, n_heads * HEAD_DIM), BF16),
        scratch_shapes=[pltpu.VMEM((3, tt + 8, HEAD_DIM), F32), pltpu.VMEM((HEAD_DIM, HEAD_DIM), F32)],
        compiler_params=_cparams(("arbitrary", "arbitrary", "arbitrary")),
        name="gdn",
    )(proj3, proj3, proj3, proj3, conv_w, conv_w, conv_w, g4, beta4, out_norm_w)


def _outproj_kernel(a_ref, b_ref, wa_ref, wb_ref, x_ref, o_ref):
    o_ref[...] = x_ref[...] + _dot(a_ref[...], wa_ref[...]) + _dot(b_ref[...], wb_ref[...])


def _outproj(o_sb, o_gdn, w_top, w_bot, x, *, tm):
    T, D = x.shape
    wa, wb = o_sb.shape[1], o_gdn.shape[1]
    return pl.pallas_call(
        _outproj_kernel,
        grid=(T // tm,),
        in_specs=[
            pl.BlockSpec((tm, wa), lambda i: (i, 0)),
            pl.BlockSpec((tm, wb), lambda i: (i, 0)),
            pl.BlockSpec((wa, D), lambda i: (0, 0)),
            pl.BlockSpec((wb, D), lambda i: (0, 0)),
            pl.BlockSpec((tm, D), lambda i: (i, 0)),
        ],
        out_specs=pl.BlockSpec((tm, D), lambda i: (i, 0)),
        out_shape=jax.ShapeDtypeStruct((T, D), F32),
        compiler_params=_cparams(("arbitrary",)),
        name="outproj",
    )(o_sb, o_gdn, w_top, w_bot, x)


def _router_kernel(x_ref, nw_ref, wr_ref, br_ref, h_ref, eid_ref, gate_ref):
    x = x_ref[...]
    tm = x.shape[0]
    ms = jnp.mean(x * x, axis=-1, keepdims=True)
    h = x * lax.rsqrt(ms + RMS_EPS) * nw_ref[...]
    h_ref[...] = h
    logits = _dot3(wr_ref[...], h, _NT) + br_ref[...]
    best = logits[0:1]
    gidx = jnp.zeros((1, tm), I32)
    for g in range(1, N_GROUPS):
        better = logits[g:g + 1] > best
        gidx = jnp.where(better, g, gidx)
        best = jnp.where(better, logits[g:g + 1], best)
    gsum = jnp.zeros((1, tm), F32)
    for g in range(N_GROUPS):
        gsum = gsum + jnp.exp(logits[g:g + 1] - best)
    group_gate = 1.0 / gsum
    E = EXPERTS_PER_GROUP
    in_group = jnp.zeros((E, tm), F32)
    for g in range(N_GROUPS):
        in_group = jnp.where(gidx == g, logits[8 + g * E:8 + (g + 1) * E], in_group)
    sub = lax.broadcasted_iota(I32, (E, tm), 0)
    m1 = jnp.max(in_group, axis=0, keepdims=True)
    i1 = jnp.min(jnp.where(in_group == m1, sub, E), axis=0, keepdims=True)
    rest = jnp.where(sub == i1, -jnp.inf, in_group)
    m2 = jnp.max(rest, axis=0, keepdims=True)
    i2 = jnp.min(jnp.where(rest == m2, sub, E), axis=0, keepdims=True)
    e2 = jnp.exp(m2 - m1)
    inv = group_gate / (1.0 + e2)
    eid_ref[0:1, :] = gidx * E + i1
    eid_ref[1:2, :] = gidx * E + i2
    gate_ref[0:1, :] = inv
    gate_ref[1:2, :] = inv * e2


def _router(x, norm_w, wr_t, br, *, tm):
    T, D = x.shape
    return pl.pallas_call(
        _router_kernel,
        grid=(T // tm,),
        in_specs=[
            pl.BlockSpec((tm, D), lambda i: (i, 0)),
            pl.BlockSpec((1, D), lambda i: (0, 0)),
            pl.BlockSpec((ROUTER_ROWS, D), lambda i: (0, 0)),
            pl.BlockSpec((ROUTER_ROWS, 1), lambda i: (0, 0)),
        ],
        out_specs=[
            pl.BlockSpec((tm, D), lambda i: (i, 0)),
            pl.BlockSpec((2, tm), lambda i: (0, i)),
            pl.BlockSpec((2, tm), lambda i: (0, i)),
        ],
        out_shape=[
            jax.ShapeDtypeStruct((T, D), F32),
            jax.ShapeDtypeStruct((2, T), I32),
            jax.ShapeDtypeStruct((2, T), F32),
        ],
        compiler_params=_cparams(("arbitrary",)),
        name="router",
    )(x, norm_w, wr_t, br)


def _plan_kernel(eid_ref, dest_ref, blk_ref, cnt_ref, *, tl, n_blk_pad):
    p = pl.program_id(0)
    i = pl.program_id(1)
    NE = N_EXPERTS
    e0 = eid_ref[0:1, :]
    e1 = eid_ref[1:2, :]
    sub = lax.broadcasted_iota(I32, (NE, tl), 0)
    hot0 = sub == e0
    hot1 = sub == e1
    onehot = jnp.logical_or(hot0, hot1).astype(BF16)
    ones = jnp.ones((tl, HEAD_DIM), BF16)

    @pl.when(jnp.logical_and(p == 0, i == 0))
    def _():
        cnt_ref[0] = jnp.zeros((NE, HEAD_DIM), F32)

    @pl.when(p == 0)
    def _():
        cnt_ref[0] += _dot(onehot, ones)

    @pl.when(jnp.logical_and(p == 1, i == 0))
    def _():
        cnt = cnt_ref[0]
        padded = jnp.floor((cnt + (MOE_BLOCK - 1)) * (1.0 / MOE_BLOCK)) * MOE_BLOCK
        er = lax.broadcasted_iota(I32, (NE, NE), 0)
        ec = lax.broadcasted_iota(I32, (NE, NE), 1)
        start = _dot3((ec < er).astype(F32), padded)
        cnt_ref[1] = start
        end_col = (start + padded)[:, 0:1]
        pos = (lax.broadcasted_iota(I32, (NE, n_blk_pad), 1) * MOE_BLOCK).astype(F32)
        n_before = jnp.sum((end_col <= pos).astype(I32), axis=0, keepdims=True)
        blk_ref[0:1, :] = jnp.minimum(n_before, NE - 1)
        total = jnp.max(end_col, axis=0, keepdims=True)
        blk_ref[1:2, :] = jnp.broadcast_to((total * (1.0 / MOE_BLOCK)).astype(I32), (1, n_blk_pad))
        cnt_ref[0] = jnp.zeros((NE, HEAD_DIM), F32)

    @pl.when(p == 1)
    def _():
        r = lax.broadcasted_iota(I32, (tl, tl), 0)
        c = lax.broadcasted_iota(I32, (tl, tl), 1)
        before = (r < c).astype(BF16)
        run = cnt_ref[0][:, 0:1] + cnt_ref[1][:, 0:1]
        slot = _dot(onehot, before) + run
        dest_ref[0:1, :] = jnp.sum(jnp.where(hot0, slot, 0.0), axis=0, keepdims=True).astype(I32)
        dest_ref[1:2, :] = jnp.sum(jnp.where(hot1, slot, 0.0), axis=0, keepdims=True).astype(I32)
        cnt_ref[0] += _dot(onehot, ones)


def _plan(eid, *, tl, n_blk_pad):
    T = eid.shape[1]
    kern = functools.partial(_plan_kernel, tl=tl, n_blk_pad=n_blk_pad)
    return pl.pallas_call(
        kern,
        grid=(2, T // tl),
        in_specs=[pl.BlockSpec((2, tl), lambda p, i: (0, i))],
        out_specs=[
            pl.BlockSpec((2, tl), lambda p, i: (0, i * p)),
            pl.BlockSpec((2, n_blk_pad), lambda p, i: (0, 0)),
        ],
        out_shape=[
            jax.ShapeDtypeStruct((2, T), I32),
            jax.ShapeDtypeStruct((2, n_blk_pad), I32),
        ],
        scratch_shapes=[pltpu.VMEM((2, N_EXPERTS, HEAD_DIM), F32)],
        compiler_params=_cparams(("arbitrary", "arbitrary")),
        name="plan",
    )(eid)


def _scatter_kernel(dest_ref, blk_ref, h_ref, xs_ref, zero_ref, sem, zsem, *, tt, n_blocks):
    i = pl.program_id(0)

    @pl.when(i == 0)
    def _():
        zero_ref[...] = jnp.zeros_like(zero_ref)
        n_valid = blk_ref[1, 0]

        def clear(b, n):
            last = blk_ref[0, b] != blk_ref[0, jnp.minimum(b + 1, n_blocks - 1)]
            do = jnp.logical_or(b >= n_valid - 1, last)

            @pl.when(do)
            def _():
                pltpu.make_async_copy(zero_ref, xs_ref.at[pl.ds(b * MOE_BLOCK, MOE_BLOCK)], zsem).start()
            return n + do.astype(I32)
        n_started = lax.fori_loop(0, n_blocks, clear, 0)

        def drain(b, carry):
            pltpu.make_async_copy(zero_ref, xs_ref.at[pl.ds(0, MOE_BLOCK)], zsem).wait()
            return carry
        lax.fori_loop(0, n_started, drain, 0)

    def issue(r, carry):
        src = h_ref.at[pl.ds(i * tt + r, 1)]
        pltpu.make_async_copy(src, xs_ref.at[pl.ds(dest_ref[0, 0, 2 * r], 1)], sem).start()
        pltpu.make_async_copy(src, xs_ref.at[pl.ds(dest_ref[0, 0, 2 * r + 1], 1)], sem).start()
        return carry
    lax.fori_loop(0, tt, issue, 0)

    def drain_rows(r, carry):
        pltpu.make_async_copy(h_ref.at[pl.ds(0, 1)], xs_ref.at[pl.ds(0, 1)], sem).wait()
        return carry
    lax.fori_loop(0, 2 * tt, drain_rows, 0)


def _scatter_rows(dest_flat, blk, h, *, cap, tt):
    T, D = h.shape
    n_blocks = cap // MOE_BLOCK
    kern = functools.partial(_scatter_kernel, tt=tt, n_blocks=n_blocks)
    return pl.pallas_call(
        kern,
        grid=(T // tt,),
        in_specs=[
            pl.BlockSpec((1, 1, 2 * tt), lambda i: (i, 0, 0), memory_space=pltpu.SMEM),
            pl.BlockSpec(memory_space=pltpu.SMEM),
            pl.BlockSpec(memory_space=pl.ANY),
        ],
        out_specs=pl.BlockSpec(memory_space=pl.ANY),
        out_shape=jax.ShapeDtypeStruct((cap, D), h.dtype),
        scratch_shapes=[pltpu.VMEM((MOE_BLOCK, D), h.dtype), pltpu.SemaphoreType.DMA, pltpu.SemaphoreType.DMA],
        compiler_params=_cparams(("arbitrary",)),
        name="scatter_rows",
    )(dest_flat, blk, h)


def _expert_kernel(blk_ref, x_ref, w1_ref, w3_ref, w2_ref, y_ref):
    b = pl.program_id(0)

    @pl.when(b < blk_ref[1, 0])
    def _():
        x = x_ref[...].astype(BF16)
        hid = _silu(_dot(x, w1_ref[0])) * _dot(x, w3_ref[0])
        y_ref[...] = _dot(hid.astype(BF16), w2_ref[0]).astype(y_ref.dtype)

    @pl.when(b >= blk_ref[1, 0])
    def _():
        y_ref[...] = jnp.zeros_like(y_ref)


def _experts(blk, x_slots, w1, w3, w2):
    cap, D = x_slots.shape
    n_blocks = cap // MOE_BLOCK
    F = w1.shape[2]

    def row_blk(b, blk_ref):
        return jnp.minimum(b, blk_ref[1, 0] - 1)

    grid_spec = pltpu.PrefetchScalarGridSpec(
        num_scalar_prefetch=1,
        grid=(n_blocks,),
        in_specs=[
            pl.BlockSpec((MOE_BLOCK, D), lambda b, blk_ref: (row_blk(b, blk_ref), 0)),
            pl.BlockSpec((1, D, F), lambda b, blk_ref: (blk_ref[0, row_blk(b, blk_ref)], 0, 0)),
            pl.BlockSpec((1, D, F), lambda b, blk_ref: (blk_ref[0, row_blk(b, blk_ref)], 0, 0)),
            pl.BlockSpec((1, F, D), lambda b, blk_ref: (blk_ref[0, row_blk(b, blk_ref)], 0, 0)),
        ],
        out_specs=pl.BlockSpec((MOE_BLOCK, D), lambda b, blk_ref: (b, 0)),
    )
    return pl.pallas_call(
        _expert_kernel,
        grid_spec=grid_spec,
        out_shape=jax.ShapeDtypeStruct((cap, D), F32),
        compiler_params=_cparams(("arbitrary",)),
        name="experts",
    )(blk, x_slots, w1, w3, w2)


def _combine_kernel(dest_ref, ys_ref, x_ref, gate_ref, o_ref, buf_ref, sem, *, tt):
    def issue(r, carry):
        pltpu.make_async_copy(ys_ref.at[pl.ds(dest_ref[0, 0, 2 * r], 1)], buf_ref.at[0, pl.ds(r, 1)], sem).start()
        pltpu.make_async_copy(ys_ref.at[pl.ds(dest_ref[0, 0, 2 * r + 1], 1)], buf_ref.at[1, pl.ds(r, 1)], sem).start()
        return carry
    lax.fori_loop(0, tt, issue, 0)

    def drain(r, carry):
        pltpu.make_async_copy(ys_ref.at[pl.ds(0, 1)], buf_ref.at[0, pl.ds(0, 1)], sem).wait()
        return carry
    lax.fori_loop(0, 2 * tt, drain, 0)
    g = gate_ref[...]
    o_ref[...] = x_ref[...] + (buf_ref[0].astype(F32) * g[:, 0:1] + buf_ref[1].astype(F32) * g[:, 1:2])


def _combine(dest_flat, y_slots, x, gate_t, *, tt):
    T, D = x.shape
    kern = functools.partial(_combine_kernel, tt=tt)
    return pl.pallas_call(
        kern,
        grid=(T // tt,),
        in_specs=[
            pl.BlockSpec((1, 1, 2 * tt), lambda i: (i, 0, 0), memory_space=pltpu.SMEM),
            pl.BlockSpec(memory_space=pl.ANY),
            pl.BlockSpec((tt, D), lambda i: (i, 0)),
            pl.BlockSpec((tt, 2), lambda i: (i, 0)),
        ],
        out_specs=pl.BlockSpec((tt, D), lambda i: (i, 0)),
        out_shape=jax.ShapeDtypeStruct((T, D), F32),
        scratch_shapes=[pltpu.VMEM((2, tt, D), y_slots.dtype), pltpu.SemaphoreType.DMA],
        compiler_params=_cparams(("arbitrary",)),
        name="combine",
    )(dest_flat, y_slots, x, gate_t)


def _mixer(x2, batch, norm_w, w_in, q_norm_w, k_norm_w, conv_w, a_log, dt_bias, out_norm_w, w_o,
           *, tm, tn, tq, tk, tt, tm_out):
    T, D = x2.shape
    S = T // batch
    n_gdn = a_log.shape[0]
    gdn_width = n_gdn * HEAD_DIM
    n_main = w_in.shape[1] - 2 * n_gdn
    sb_width = (n_main - 4 * gdn_width) // 3
    n_sb = sb_width // HEAD_DIM
    w_main = w_in[:, :n_main].astype(BF16)
    wab_t = w_in[:, n_main:].T.astype(BF16)
    reps = tn // HEAD_DIM
    qk_w = jnp.stack([jnp.tile(q_norm_w, reps) * (1.0 / math.sqrt(HEAD_DIM)), jnp.tile(k_norm_w, reps)])
    proj, g, beta = _inproj(x2, norm_w.reshape(1, D), w_main, wab_t, qk_w,
                            a_log.reshape(-1, 1), dt_bias.reshape(-1, 1), sb_width=sb_width, tm=tm, tn=tn)
    proj3 = proj.reshape(batch, S, n_main)
    o_sb = _sb_attention(proj3, n_heads=n_sb, tq=tq, tk=tk)
    g4 = g.reshape(n_gdn, batch, S // GDN_CHUNK, GDN_CHUNK)
    beta4 = beta.reshape(n_gdn, batch, S // GDN_CHUNK, GDN_CHUNK)
    o_gdn = _gdn(proj3, conv_w, g4, beta4, out_norm_w.reshape(1, HEAD_DIM), n_heads=n_gdn,
                 col0=3 * sb_width, tt=tt)
    w_ob = w_o.astype(BF16)
    return _outproj(o_sb.reshape(T, sb_width), o_gdn.reshape(T, gdn_width),
                    w_ob[:sb_width], w_ob[sb_width:], x2, tm=tm_out)


def _moe(x2, norm_w, w_group, b_group, w_expert, b_expert, w1, w3, w2, *, tm_r, tl, tt):
    T, D = x2.shape
    tail = ROUTER_ROWS - 8 - N_EXPERTS
    wr_t = jnp.concatenate([w_group, jnp.zeros((D, 8 - N_GROUPS), F32), w_expert,
                            jnp.zeros((D, tail), F32)], axis=1).T
    br = jnp.concatenate([b_group, jnp.zeros((8 - N_GROUPS,), F32), b_expert,
                          jnp.zeros((tail,), F32)]).reshape(-1, 1)
    h, eid, gate = _router(x2, norm_w.reshape(1, D), wr_t, br, tm=tm_r)
    n_blocks = (2 * T + MOE_BLOCK - 1) // MOE_BLOCK + N_EXPERTS
    n_blk_pad = ((n_blocks + 127) // 128) * 128
    dest, blk = _plan(eid, tl=tl, n_blk_pad=n_blk_pad)
    dest_flat = dest.T.reshape(T // tt, 1, 2 * tt)
    x_slots = _scatter_rows(dest_flat, blk, h, cap=n_blocks * MOE_BLOCK, tt=tt)
    y_slots = _experts(blk, x_slots, w1.astype(BF16), w3.astype(BF16), w2.astype(BF16))
    return _combine(dest_flat, y_slots, x2, gate.T, tt=tt)


def _forward(x, norm1_w, w_in, sb_q_norm_w, sb_k_norm_w, gdn_conv_w, gdn_a_log, gdn_dt_bias,
             gdn_out_norm_w, w_o, norm2_w, w_group, b_group, w_expert, b_expert, w1, w3, w2, *, tiles):
    batch, S, D = x.shape
    x2 = x.reshape(batch * S, D)
    for l in range(norm1_w.shape[0]):
        x2 = _mixer(x2, batch, norm1_w[l], w_in[l], sb_q_norm_w[l], sb_k_norm_w[l], gdn_conv_w[l],
                    gdn_a_log[l], gdn_dt_bias[l], gdn_out_norm_w[l], w_o[l], **tiles["mixer"])
        x2 = _moe(x2, norm2_w[l], w_group[l], b_group[l], w_expert[l], b_expert[l],
                  w1[l], w3[l], w2[l], **tiles["moe"])
    return x2.reshape(batch, S, D)


_TILES = {
    "mixer": dict(tm=1024, tn=512, tq=256, tk=128, tt=512, tm_out=512),
    "moe": dict(tm_r=512, tl=512, tt=256),
}


def kernel(x, norm1_w, w_in, sb_q_norm_w, sb_k_norm_w, gdn_conv_w, gdn_a_log, gdn_dt_bias, gdn_out_norm_w, w_o, norm2_w, w_group, b_group, w_expert, b_expert, w1, w3, w2):
    return _forward(x, norm1_w, w_in, sb_q_norm_w, sb_k_norm_w, gdn_conv_w, gdn_a_log, gdn_dt_bias,
                    gdn_out_norm_w, w_o, norm2_w, w_group, b_group, w_expert, b_expert, w1, w3, w2,
                    tiles=_TILES)
```

```python
import functools
import math

import jax
import jax.numpy as jnp
from jax import lax
from jax.experimental import pallas as pl
from jax.experimental.pallas import tpu as pltpu

F32 = jnp.float32
BF16 = jnp.bfloat16
I32 = jnp.int32

HEAD_DIM = 128
GDN_CONV = 4
GDN_CHUNK = 64
N_GROUPS = 4
EXPERTS_PER_GROUP = 8
N_EXPERTS = N_GROUPS * EXPERTS_PER_GROUP
MOE_BLOCK = 128
RMS_EPS = 1e-6
ROUTER_ROWS = 48
VMEM_LIMIT = 56 * 1024 * 1024


def _cparams(sem, vmem=VMEM_LIMIT):
    return pltpu.CompilerParams(dimension_semantics=sem, vmem_limit_bytes=vmem)


def _dot(a, b, dims=(((1,), (0,)), ((), ()))):
    return lax.dot_general(a, b, dims, preferred_element_type=F32)


_NT = (((1,), (1,)), ((), ()))
_TN = (((0,), (0,)), ((), ()))


def _split(a):
    hi = a.astype(BF16)
    lo = (a - hi.astype(F32)).astype(BF16)
    return hi, lo


def _dot3(a, b, dims=(((1,), (0,)), ((), ()))):
    ah, al = _split(a)
    bh, bl = _split(b)
    return _dot(ah, bh, dims) + (_dot(ah, bl, dims) + _dot(al, bh, dims))


def _softplus(x):
    return jnp.maximum(x, 0.0) + jnp.log(1.0 + jnp.exp(-jnp.abs(x)))


def _sigmoid(x):
    return 1.0 / (1.0 + jnp.exp(-x))


def _inproj_kernel(x_ref, nw_ref, w_ref, wab_ref, qkw_ref, alog_ref, dt_ref,
                   proj_ref, g_ref, beta_ref, h_ref, *, n_qk_tiles, n_heads, rows):
    j = pl.program_id(1)
    tm = x_ref.shape[0]

    @pl.when(j == 0)
    def _():
        def norm_rows(r, carry):
            rs = pl.multiple_of(r * rows, rows)
            x = x_ref[pl.ds(rs, rows), :]
            ms = jnp.mean(x * x, axis=-1, keepdims=True)
            h = x * lax.rsqrt(ms + RMS_EPS) * nw_ref[...]
            h_ref[pl.ds(rs, rows), :] = h.astype(BF16)
            return carry
        lax.fori_loop(0, tm // rows, norm_rows, 0)
        ab = _dot(wab_ref[...], h_ref[...], _NT)
        ga = ab[0:n_heads]
        gb = ab[n_heads:2 * n_heads]
        g_ref[...] = -jnp.exp(alog_ref[...]) * _softplus(ga + dt_ref[...])
        beta_ref[...] = _sigmoid(gb)

    acc = _dot(h_ref[...], w_ref[...])

    @pl.when(j < n_qk_tiles)
    def _():
        wsel = jnp.where(j < n_qk_tiles // 2, qkw_ref[0:1, :], qkw_ref[1:2, :])
        for hh in range(acc.shape[1] // HEAD_DIM):
            sl = slice(hh * HEAD_DIM, (hh + 1) * HEAD_DIM)
            a = acc[:, sl]
            ms = jnp.mean(a * a, axis=-1, keepdims=True)
            proj_ref[:, sl] = (a * lax.rsqrt(ms + RMS_EPS) * wsel[:, sl]).astype(BF16)

    @pl.when(j >= n_qk_tiles)
    def _():
        proj_ref[...] = acc.astype(BF16)


def _inproj(x, norm_w, w_main, wab_t, qk_w, a_log, dt_bias, *, sb_width, tm, tn):
    T, D = x.shape
    N = w_main.shape[1]
    n_heads = a_log.shape[0]
    n_qk_tiles = 2 * sb_width // tn
    kern = functools.partial(_inproj_kernel, n_qk_tiles=n_qk_tiles, n_heads=n_heads,
                             rows=min(tm, 256))
    return pl.pallas_call(
        kern,
        grid=(T // tm, N // tn),
        in_specs=[
            pl.BlockSpec((tm, D), lambda i, j: (i, 0)),
            pl.BlockSpec((1, D), lambda i, j: (0, 0)),
            pl.BlockSpec((D, tn), lambda i, j: (0, j)),
            pl.BlockSpec((2 * n_heads, D), lambda i, j: (0, 0)),
            pl.BlockSpec((2, tn), lambda i, j: (0, 0)),
            pl.BlockSpec((n_heads, 1), lambda i, j: (0, 0)),
            pl.BlockSpec((n_heads, 1), lambda i, j: (0, 0)),
        ],
        out_specs=[
            pl.BlockSpec((tm, tn), lambda i, j: (i, j)),
            pl.BlockSpec((n_heads, tm), lambda i, j: (0, i)),
            pl.BlockSpec((n_heads, tm), lambda i, j: (0, i)),
        ],
        out_shape=[
            jax.ShapeDtypeStruct((T, N), BF16),
            jax.ShapeDtypeStruct((n_heads, T), F32),
            jax.ShapeDtypeStruct((n_heads, T), F32),
        ],
        scratch_shapes=[pltpu.VMEM((tm, D), BF16)],
        compiler_params=_cparams(("arbitrary", "arbitrary")),
        name="inproj",
    )(x, norm_w, w_main, wab_t, qk_w, a_log, dt_bias)


def _sb_kernel(q_ref, k_ref, v_ref, o_ref, *, tq, tk, hp):
    i = pl.program_id(2)
    heads = range(hp)
    hs = [slice(h * HEAD_DIM, (h + 1) * HEAD_DIM) for h in heads]
    q = [q_ref[0, :, hs[h]] for h in heads]
    row = lax.broadcasted_iota(I32, (tk, tk), 0)
    col = lax.broadcasted_iota(I32, (tk, tk), 1)
    upper_incl = (row >= col).astype(BF16)
    rel = lax.broadcasted_iota(I32, (tq, tk), 1) - lax.broadcasted_iota(I32, (tq, tk), 0)

    def block(ks, carry, masked):
        acc, c = carry
        z = [_dot(q[h], k_ref[0, pl.ds(ks, tk), hs[h]], _NT) for h in heads]
        log_1mb = [-_softplus(z[h]) for h in heads]
        if masked:
            causal = rel + (ks - i * tq) < 0
            log_1mb = [jnp.where(causal, log_1mb[h], 0.0) for h in heads]
        parts = [_split(log_1mb[h]) for h in heads]
        incl = [_dot(parts[h][0], upper_incl) + _dot(parts[h][1], upper_incl) for h in heads]
        w = [jnp.exp(z[h] + incl[h] + c[h]) for h in heads]
        if masked:
            w = [jnp.where(causal, w[h], 0.0) for h in heads]
        acc = [acc[h] + _dot(_bf(w[h]), v_ref[0, pl.ds(ks, tk), hs[h]]) for h in heads]
        c = [c[h] + incl[h][:, 0:1] for h in heads]
        return acc, c

    carry = ([jnp.zeros((tq, HEAD_DIM), F32) for _ in heads], [jnp.zeros((tq, 1), F32) for _ in heads])
    n_diag = tq // tk
    for d in range(n_diag - 1, -1, -1):
        carry = block(pl.multiple_of(i * tq + d * tk, tk), carry, True)

    def body(jj, carry):
        return block(pl.multiple_of((i * n_diag - 1 - jj) * tk, tk), carry, False)

    acc, _ = lax.fori_loop(0, i * n_diag, body, carry)
    for h in heads:
        o_ref[0, :, hs[h]] = acc[h].astype(o_ref.dtype)


def _sb_attention(proj3, *, n_heads, tq, tk, hp):
    B, S, _ = proj3.shape
    kern = functools.partial(_sb_kernel, tq=tq, tk=tk, hp=hp)
    ng = n_heads // hp
    wd = hp * HEAD_DIM
    return pl.pallas_call(
        kern,
        grid=(B, ng, S // tq),
        in_specs=[
            pl.BlockSpec((1, tq, wd), lambda b, h, i: (b, i, h)),
            pl.BlockSpec((1, S, wd), lambda b, h, i: (b, 0, ng + h)),
            pl.BlockSpec((1, S, wd), lambda b, h, i: (b, 0, 2 * ng + h)),
        ],
        out_specs=pl.BlockSpec((1, tq, wd), lambda b, h, i: (b, i, h)),
        out_shape=jax.ShapeDtypeStruct((B, S, n_heads * HEAD_DIM), BF16),
        compiler_params=_cparams(("arbitrary", "arbitrary", "arbitrary")),
        name="sb_attention",
    )(proj3, proj3, proj3)


def _silu(x):
    return x * _sigmoid(x)


SUB = 2 * GDN_CHUNK


def _bf(a):
    return a.astype(BF16)


def _exact_parts(a):
    p1 = a.astype(BF16)
    r1 = a - p1.astype(F32)
    p2 = r1.astype(BF16)
    p3 = (r1 - p2.astype(F32)).astype(BF16)
    return p1, p2, p3


def _gdn_kernel(xq_ref, xk_ref, xv_ref, z_ref, cw_ref, g_ref, b_ref, ow_ref,
                o_ref, xs_ref, qkv_ref, state_ref, *, tt, n_heads):
    t = pl.program_id(1)
    C = GDN_CHUNK
    HALO = 8
    W = n_heads * HEAD_DIM
    heads = range(n_heads)

    @pl.when(t == 0)
    def _():
        xs_ref[:, 0:HALO, :] = jnp.zeros((3, HALO, W), F32)
        state_ref[...] = jnp.zeros_like(state_ref)

    @pl.when(t > 0)
    def _():
        xs_ref[:, 0:HALO, :] = xs_ref[:, tt:tt + HALO, :]

    for idx, x_ref in enumerate((xq_ref, xk_ref, xv_ref)):
        for h in heads:
            cs = slice(h * HEAD_DIM, (h + 1) * HEAD_DIM)
            xs_ref[idx, HALO:, cs] = x_ref[0, :, cs].astype(F32)
            acc = None
            for kk in range(GDN_CONV):
                wrow = cw_ref[kk:kk + 1, idx * W + h * HEAD_DIM: idx * W + (h + 1) * HEAD_DIM]
                term = xs_ref[idx, pl.ds(HALO - GDN_CONV + 1 + kk, tt), cs] * wrow
                acc = term if acc is None else acc + term
            a = _silu(acc)
            if idx < 2:
                a = a * lax.rsqrt(jnp.sum(a * a, axis=-1, keepdims=True) + RMS_EPS)
            if idx == 0:
                a = a * (1.0 / math.sqrt(HEAD_DIM))
            qkv_ref[idx, :, cs] = a

    ri = lax.broadcasted_iota(I32, (SUB, SUB), 0)
    ci = lax.broadcasted_iota(I32, (SUB, SUB), 1)
    same = (ri >> 6) == (ci >> 6)
    lower_incl = jnp.logical_and(same, ri >= ci)
    lower_strict = jnp.logical_and(same, ri > ci)
    blk16 = (ri >> 4) == (ci >> 4)
    blk32 = (ri >> 5) == (ci >> 5)
    only32 = jnp.logical_and(blk32, jnp.logical_not(blk16))
    only64 = jnp.logical_and(same, jnp.logical_not(blk32))
    eye_f = (ri == ci).astype(F32)
    eye_b = (ri == ci).astype(BF16)
    cum_b = jnp.logical_and(same, ri <= ci).astype(BF16)
    first = lax.broadcasted_iota(I32, (SUB, 1), 0) < C

    def sub_tile(s, carry):
        r0 = pl.multiple_of(s * SUB, SUB)
        g_rows = g_ref[:, pl.ds(r0, SUB)]
        b_rows = b_ref[:, pl.ds(r0, SUB)]
        gp = _exact_parts(g_rows)
        gc_rows = _dot(gp[0], cum_b) + _dot(gp[1], cum_b) + _dot(gp[2], cum_b)
        sp = _exact_parts(jnp.concatenate([gc_rows, b_rows], axis=0))
        cols = _dot(eye_b, sp[0], _NT) + _dot(eye_b, sp[1], _NT) + _dot(eye_b, sp[2], _NT)

        q, k, v, beta, decay, egc, kdec, eg_last = [], [], [], [], [], [], [], []
        for h in heads:
            cs = slice(h * HEAD_DIM, (h + 1) * HEAD_DIM)
            q.append(qkv_ref[0, pl.ds(r0, SUB), cs])
            k.append(qkv_ref[1, pl.ds(r0, SUB), cs])
            v.append(qkv_ref[2, pl.ds(r0, SUB), cs])
            gc_col = cols[:, h:h + 1]
            beta.append(cols[:, n_heads + h:n_heads + h + 1])
            decay.append(jnp.where(lower_incl, jnp.exp(jnp.minimum(gc_col - gc_rows[h:h + 1, :], 0.0)), 0.0))
            egc.append(jnp.exp(gc_col))
            g_last = jnp.where(first, gc_col[C - 1:C, :], gc_col[SUB - 1:SUB, :])
            kdec.append(_bf(k[h] * jnp.exp(g_last - gc_col)))
            eg_last.append((jnp.exp(gc_col[C - 1:C, :]), jnp.exp(gc_col[SUB - 1:SUB, :])))

        kb = [k[h] * beta[h] for h in heads]
        k_b = [_bf(k[h]) for h in heads]
        kb_b = [_bf(kb[h]) for h in heads]
        lmat = [jnp.where(lower_strict, _dot(kb_b[h], k_b[h], _NT) * decay[h], 0.0) for h in heads]
        d16 = [jnp.where(blk16, lmat[h], 0.0) for h in heads]
        d16_b = [_bf(d16[h]) for h in heads]
        p2 = [_bf(_dot(d16_b[h], d16_b[h])) for h in heads]
        p4 = [_bf(_dot(p2[h], p2[h])) for h in heads]
        p8 = [_bf(_dot(p4[h], p4[h])) for h in heads]
        x0 = [eye_f - d16[h] for h in heads]
        x0 = [x0[h] + _dot(_bf(x0[h]), p2[h]) for h in heads]
        x0 = [x0[h] + _dot(_bf(x0[h]), p4[h]) for h in heads]
        x0 = [x0[h] + _dot(_bf(x0[h]), p8[h]) for h in heads]
        x0_b = [_bf(x0[h]) for h in heads]
        y1 = [_bf(_dot(x0_b[h], _bf(jnp.where(only32, lmat[h], 0.0)))) for h in heads]
        x1 = [x0[h] - _dot(y1[h], x0_b[h]) for h in heads]
        x1_b = [_bf(x1[h]) for h in heads]
        y2 = [_bf(_dot(x1_b[h], _bf(jnp.where(only64, lmat[h], 0.0)))) for h in heads]
        t_b = [_bf(x1[h] - _dot(y2[h], x1_b[h])) for h in heads]
        u = [_dot(t_b[h], _bf(v[h] * beta[h])) for h in heads]
        w = [_dot(t_b[h], _bf(kb[h] * egc[h])) for h in heads]
        attn = [_bf(jnp.where(lower_incl, _dot(_bf(q[h]), k_b[h], _NT) * decay[h], 0.0)) for h in heads]
        qg = [q[h] * egc[h] for h in heads]

        state = [state_ref[h] for h in heads]
        zeros = jnp.zeros((C, HEAD_DIM), F32)
        for c in range(2):
            rows = slice(c * C, (c + 1) * C)
            wq = [_bf(jnp.concatenate([w[h][rows], qg[h][rows]], axis=0)) for h in heads]
            r = [_dot(wq[h], _bf(state[h])) for h in heads]
            v_new = [u[h][rows] - r[h][0:C] for h in heads]
            v_pad = [_bf(jnp.concatenate([v_new[h], zeros] if c == 0 else [zeros, v_new[h]], axis=0))
                     for h in heads]
            o = [r[h][C:SUB] + _dot(attn[h][rows], v_pad[h]) for h in heads]
            state = [state[h] * eg_last[h][c] + _dot(kdec[h][rows], _bf(v_new[h]), _TN) for h in heads]
            for h in heads:
                cs = slice(h * HEAD_DIM, (h + 1) * HEAD_DIM)
                ms = jnp.mean(o[h] * o[h], axis=-1, keepdims=True)
                zc = z_ref[0, pl.ds(r0 + c * C, C), cs].astype(F32)
                o_ref[0, pl.ds(r0 + c * C, C), cs] = (
                    o[h] * lax.rsqrt(ms + RMS_EPS) * ow_ref[...] * _silu(zc)).astype(o_ref.dtype)
        for h in heads:
            state_ref[h] = state[h]
        return carry

    lax.fori_loop(0, tt // SUB, sub_tile, 0)


def _gdn(proj3, conv_w, g, beta, out_norm_w, *, n_heads, col0, tt):
    B, S, _ = proj3.shape
    W = n_heads * HEAD_DIM
    kern = functools.partial(_gdn_kernel, tt=tt, n_heads=n_heads)
    cb = col0 // W
    nt = S // tt
    xspec = lambda off: pl.BlockSpec((1, tt, W), lambda b, t: (b, t, cb + off))
    gspec = pl.BlockSpec((n_heads, tt), lambda b, t: (0, b * nt + t))
    return pl.pallas_call(
        kern,
        grid=(B, nt),
        in_specs=[xspec(0), xspec(1), xspec(2), xspec(3),
                  pl.BlockSpec((GDN_CONV, 3 * W), lambda b, t: (0, 0)),
                  gspec, gspec, pl.BlockSpec((1, HEAD_DIM), lambda b, t: (0, 0))],
        out_specs=pl.BlockSpec((1, tt, W), lambda b, t: (b, t, 0)),
        out_shape=jax.ShapeDtypeStruct((B, S, W), BF16),
        scratch_shapes=[pltpu.VMEM((3, tt + 8, W), F32), pltpu.VMEM((3, tt, W), F32),
                        pltpu.VMEM((n_heads, HEAD_DIM, HEAD_DIM), F32)],
        compiler_params=_cparams(("arbitrary", "arbitrary")),
        name="gdn",
    )(proj3, proj3, proj3, proj3, conv_w, g, beta, out_norm_w)


def _outproj_kernel(a_ref, b_ref, wa_ref, wb_ref, x_ref, o_ref):
    o_ref[...] = x_ref[...] + _dot(a_ref[...], wa_ref[...]) + _dot(b_ref[...], wb_ref[...])


def _outproj(o_sb, o_gdn, w_top, w_bot, x, *, tm):
    T, D = x.shape
    wa, wb = o_sb.shape[1], o_gdn.shape[1]
    return pl.pallas_call(
        _outproj_kernel,
        grid=(T // tm,),
        in_specs=[
            pl.BlockSpec((tm, wa), lambda i: (i, 0)),
            pl.BlockSpec((tm, wb), lambda i: (i, 0)),
            pl.BlockSpec((wa, D), lambda i: (0, 0)),
            pl.BlockSpec((wb, D), lambda i: (0, 0)),
            pl.BlockSpec((tm, D), lambda i: (i, 0)),
        ],
        out_specs=pl.BlockSpec((tm, D), lambda i: (i, 0)),
        out_shape=jax.ShapeDtypeStruct((T, D), F32),
        compiler_params=_cparams(("arbitrary",)),
        name="outproj",
    )(o_sb, o_gdn, w_top, w_bot, x)


def _router_kernel(x_ref, nw_ref, wr_ref, br_ref, h_ref, eid_ref, gate_ref):
    x = x_ref[...]
    tm = x.shape[0]
    ms = jnp.mean(x * x, axis=-1, keepdims=True)
    h = x * lax.rsqrt(ms + RMS_EPS) * nw_ref[...]
    h_ref[...] = h
    logits = _dot3(wr_ref[...], h, _NT) + br_ref[...]
    best = logits[0:1]
    gidx = jnp.zeros((1, tm), I32)
    for g in range(1, N_GROUPS):
        better = logits[g:g + 1] > best
        gidx = jnp.where(better, g, gidx)
        best = jnp.where(better, logits[g:g + 1], best)
    gsum = jnp.zeros((1, tm), F32)
    for g in range(N_GROUPS):
        gsum = gsum + jnp.exp(logits[g:g + 1] - best)
    group_gate = 1.0 / gsum
    E = EXPERTS_PER_GROUP
    in_group = jnp.zeros((E, tm), F32)
    for g in range(N_GROUPS):
        in_group = jnp.where(gidx == g, logits[8 + g * E:8 + (g + 1) * E], in_group)
    sub = lax.broadcasted_iota(I32, (E, tm), 0)
    m1 = jnp.max(in_group, axis=0, keepdims=True)
    i1 = jnp.min(jnp.where(in_group == m1, sub, E), axis=0, keepdims=True)
    rest = jnp.where(sub == i1, -jnp.inf, in_group)
    m2 = jnp.max(rest, axis=0, keepdims=True)
    i2 = jnp.min(jnp.where(rest == m2, sub, E), axis=0, keepdims=True)
    e2 = jnp.exp(m2 - m1)
    inv = group_gate / (1.0 + e2)
    eid_ref[0:1, :] = gidx * E + i1
    eid_ref[1:2, :] = gidx * E + i2
    gate_ref[0:1, :] = inv
    gate_ref[1:2, :] = inv * e2


def _router(x, norm_w, wr_t, br, *, tm):
    T, D = x.shape
    return pl.pallas_call(
        _router_kernel,
        grid=(T // tm,),
        in_specs=[
            pl.BlockSpec((tm, D), lambda i: (i, 0)),
            pl.BlockSpec((1, D), lambda i: (0, 0)),
            pl.BlockSpec((ROUTER_ROWS, D), lambda i: (0, 0)),
            pl.BlockSpec((ROUTER_ROWS, 1), lambda i: (0, 0)),
        ],
        out_specs=[
            pl.BlockSpec((tm, D), lambda i: (i, 0)),
            pl.BlockSpec((2, tm), lambda i: (0, i)),
            pl.BlockSpec((2, tm), lambda i: (0, i)),
        ],
        out_shape=[
            jax.ShapeDtypeStruct((T, D), F32),
            jax.ShapeDtypeStruct((2, T), I32),
            jax.ShapeDtypeStruct((2, T), F32),
        ],
        compiler_params=_cparams(("arbitrary",)),
        name="router",
    )(x, norm_w, wr_t, br)


def _plan_kernel(eid_ref, dest_ref, blk_ref, cnt_ref, *, tl, n_blk_pad):
    p = pl.program_id(0)
    i = pl.program_id(1)
    NE = N_EXPERTS
    e0 = eid_ref[0:1, :]
    e1 = eid_ref[1:2, :]
    sub = lax.broadcasted_iota(I32, (NE, tl), 0)
    hot0 = sub == e0
    hot1 = sub == e1
    onehot = jnp.logical_or(hot0, hot1).astype(BF16)
    ones = jnp.ones((tl, HEAD_DIM), BF16)

    @pl.when(jnp.logical_and(p == 0, i == 0))
    def _():
        cnt_ref[0] = jnp.zeros((NE, HEAD_DIM), F32)

    @pl.when(p == 0)
    def _():
        cnt_ref[0] += _dot(onehot, ones)

    @pl.when(jnp.logical_and(p == 1, i == 0))
    def _():
        cnt = cnt_ref[0]
        padded = jnp.floor((cnt + (MOE_BLOCK - 1)) * (1.0 / MOE_BLOCK)) * MOE_BLOCK
        er = lax.broadcasted_iota(I32, (NE, NE), 0)
        ec = lax.broadcasted_iota(I32, (NE, NE), 1)
        start = _dot3((ec < er).astype(F32), padded)
        cnt_ref[1] = start
        end_col = (start + padded)[:, 0:1]
        pos = (lax.broadcasted_iota(I32, (NE, n_blk_pad), 1) * MOE_BLOCK).astype(F32)
        n_before = jnp.sum((end_col <= pos).astype(I32), axis=0, keepdims=True)
        blk_ref[0:1, :] = jnp.minimum(n_before, NE - 1)
        total = jnp.max(end_col, axis=0, keepdims=True)
        blk_ref[1:2, :] = jnp.broadcast_to((total * (1.0 / MOE_BLOCK)).astype(I32), (1, n_blk_pad))
        cnt_ref[0] = jnp.zeros((NE, HEAD_DIM), F32)

    @pl.when(p == 1)
    def _():
        r = lax.broadcasted_iota(I32, (tl, tl), 0)
        c = lax.broadcasted_iota(I32, (tl, tl), 1)
        before = (r < c).astype(BF16)
        run = cnt_ref[0][:, 0:1] + cnt_ref[1][:, 0:1]
        slot = _dot(onehot, before) + run
        dest_ref[0:1, :] = jnp.sum(jnp.where(hot0, slot, 0.0), axis=0, keepdims=True).astype(I32)
        dest_ref[1:2, :] = jnp.sum(jnp.where(hot1, slot, 0.0), axis=0, keepdims=True).astype(I32)
        cnt_ref[0] += _dot(onehot, ones)


def _plan(eid, *, tl, n_blk_pad):
    T = eid.shape[1]
    kern = functools.partial(_plan_kernel, tl=tl, n_blk_pad=n_blk_pad)
    return pl.pallas_call(
        kern,
        grid=(2, T // tl),
        in_specs=[pl.BlockSpec((2, tl), lambda p, i: (0, i))],
        out_specs=[
            pl.BlockSpec((2, tl), lambda p, i: (0, i * p)),
            pl.BlockSpec((2, n_blk_pad), lambda p, i: (0, 0)),
        ],
        out_shape=[
            jax.ShapeDtypeStruct((2, T), I32),
            jax.ShapeDtypeStruct((2, n_blk_pad), I32),
        ],
        scratch_shapes=[pltpu.VMEM((2, N_EXPERTS, HEAD_DIM), F32)],
        compiler_params=_cparams(("arbitrary", "arbitrary")),
        name="plan",
    )(eid)


def _scatter_kernel(dest_ref, blk_ref, h_ref, xs_ref, zero_ref, sem, zsem, *, tt, n_blocks):
    i = pl.program_id(0)

    @pl.when(i == 0)
    def _():
        zero_ref[...] = jnp.zeros_like(zero_ref)
        n_valid = blk_ref[1, 0]

        def clear(b, n):
            last = blk_ref[0, b] != blk_ref[0, jnp.minimum(b + 1, n_blocks - 1)]
            do = jnp.logical_or(b >= n_valid - 1, last)

            @pl.when(do)
            def _():
                pltpu.make_async_copy(zero_ref, xs_ref.at[pl.ds(b * MOE_BLOCK, MOE_BLOCK)], zsem).start()
            return n + do.astype(I32)
        n_started = lax.fori_loop(0, n_blocks, clear, 0)

        def drain(b, carry):
            pltpu.make_async_copy(zero_ref, xs_ref.at[pl.ds(0, MOE_BLOCK)], zsem).wait()
            return carry
        lax.fori_loop(0, n_started, drain, 0)

    def issue(r, carry):
        src = h_ref.at[pl.ds(r, 1)]
        pltpu.make_async_copy(src, xs_ref.at[pl.ds(dest_ref[0, 0, 2 * r], 1)], sem).start()
        pltpu.make_async_copy(src, xs_ref.at[pl.ds(dest_ref[0, 0, 2 * r + 1], 1)], sem).start()
        return carry
    lax.fori_loop(0, tt, issue, 0)

    def drain_rows(r, carry):
        pltpu.make_async_copy(h_ref.at[pl.ds(0, 1)], xs_ref.at[pl.ds(0, 1)], sem).wait()
        return carry
    lax.fori_loop(0, 2 * tt, drain_rows, 0)


def _scatter_rows(dest_flat, blk, h, *, cap, tt):
    T, D = h.shape
    n_blocks = cap // MOE_BLOCK
    kern = functools.partial(_scatter_kernel, tt=tt, n_blocks=n_blocks)
    return pl.pallas_call(
        kern,
        grid=(T // tt,),
        in_specs=[
            pl.BlockSpec((1, 1, 2 * tt), lambda i: (i, 0, 0), memory_space=pltpu.SMEM),
            pl.BlockSpec(memory_space=pltpu.SMEM),
            pl.BlockSpec((tt, D), lambda i: (i, 0)),
        ],
        out_specs=pl.BlockSpec(memory_space=pl.ANY),
        out_shape=jax.ShapeDtypeStruct((cap, D), h.dtype),
        scratch_shapes=[pltpu.VMEM((MOE_BLOCK, D), h.dtype), pltpu.SemaphoreType.DMA, pltpu.SemaphoreType.DMA],
        compiler_params=_cparams(("arbitrary",)),
        name="scatter_rows",
    )(dest_flat, blk, h)


def _expert_kernel(blk_ref, x_ref, w1_ref, w3_ref, w2_ref, y_ref):
    b = pl.program_id(0)

    @pl.when(b < blk_ref[1, 0])
    def _():
        x = x_ref[...].astype(BF16)
        hid = _silu(_dot(x, w1_ref[0])) * _dot(x, w3_ref[0])
        y_ref[...] = _dot(hid.astype(BF16), w2_ref[0]).astype(y_ref.dtype)

    @pl.when(b >= blk_ref[1, 0])
    def _():
        y_ref[...] = jnp.zeros_like(y_ref)


def _experts(blk, x_slots, w1, w3, w2):
    cap, D = x_slots.shape
    n_blocks = cap // MOE_BLOCK
    F = w1.shape[2]

    def row_blk(b, blk_ref):
        return jnp.minimum(b, blk_ref[1, 0] - 1)

    grid_spec = pltpu.PrefetchScalarGridSpec(
        num_scalar_prefetch=1,
        grid=(n_blocks,),
        in_specs=[
            pl.BlockSpec((MOE_BLOCK, D), lambda b, blk_ref: (row_blk(b, blk_ref), 0)),
            pl.BlockSpec((1, D, F), lambda b, blk_ref: (blk_ref[0, row_blk(b, blk_ref)], 0, 0)),
            pl.BlockSpec((1, D, F), lambda b, blk_ref: (blk_ref[0, row_blk(b, blk_ref)], 0, 0)),
            pl.BlockSpec((1, F, D), lambda b, blk_ref: (blk_ref[0, row_blk(b, blk_ref)], 0, 0)),
        ],
        out_specs=pl.BlockSpec((MOE_BLOCK, D), lambda b, blk_ref: (b, 0)),
    )
    return pl.pallas_call(
        _expert_kernel,
        grid_spec=grid_spec,
        out_shape=jax.ShapeDtypeStruct((cap, D), F32),
        compiler_params=_cparams(("arbitrary",)),
        name="experts",
    )(blk, x_slots, w1, w3, w2)


def _combine_kernel(dest_ref, ys_ref, x_ref, gate_ref, o_ref, buf_ref, sem, *, tt):
    def issue(r, carry):
        pltpu.make_async_copy(ys_ref.at[pl.ds(dest_ref[0, 0, 2 * r], 1)], buf_ref.at[0, pl.ds(r, 1)], sem).start()
        pltpu.make_async_copy(ys_ref.at[pl.ds(dest_ref[0, 0, 2 * r + 1], 1)], buf_ref.at[1, pl.ds(r, 1)], sem).start()
        return carry
    lax.fori_loop(0, tt, issue, 0)

    def drain(r, carry):
        pltpu.make_async_copy(ys_ref.at[pl.ds(0, 1)], buf_ref.at[0, pl.ds(0, 1)], sem).wait()
        return carry
    lax.fori_loop(0, 2 * tt, drain, 0)
    g = gate_ref[...]
    o_ref[...] = x_ref[...] + (buf_ref[0].astype(F32) * g[:, 0:1] + buf_ref[1].astype(F32) * g[:, 1:2])


def _combine(dest_flat, y_slots, x, gate_t, *, tt):
    T, D = x.shape
    kern = functools.partial(_combine_kernel, tt=tt)
    return pl.pallas_call(
        kern,
        grid=(T // tt,),
        in_specs=[
            pl.BlockSpec((1, 1, 2 * tt), lambda i: (i, 0, 0), memory_space=pltpu.SMEM),
            pl.BlockSpec(memory_space=pl.ANY),
            pl.BlockSpec((tt, D), lambda i: (i, 0)),
            pl.BlockSpec((tt, 2), lambda i: (i, 0)),
        ],
        out_specs=pl.BlockSpec((tt, D), lambda i: (i, 0)),
        out_shape=jax.ShapeDtypeStruct((T, D), F32),
        scratch_shapes=[pltpu.VMEM((2, tt, D), y_slots.dtype), pltpu.SemaphoreType.DMA],
        compiler_params=_cparams(("arbitrary",)),
        name="combine",
    )(dest_flat, y_slots, x, gate_t)


def _mixer(x2, batch, norm_w, w_in, q_norm_w, k_norm_w, conv_w, a_log, dt_bias, out_norm_w, w_o,
           *, tm, tn, tq, tk, hp, tt, tm_out):
    T, D = x2.shape
    S = T // batch
    n_gdn = a_log.shape[0]
    gdn_width = n_gdn * HEAD_DIM
    n_main = w_in.shape[1] - 2 * n_gdn
    sb_width = (n_main - 4 * gdn_width) // 3
    n_sb = sb_width // HEAD_DIM
    w_main = w_in[:, :n_main].astype(BF16)
    wab_t = w_in[:, n_main:].T.astype(BF16)
    reps = tn // HEAD_DIM
    qk_w = jnp.stack([jnp.tile(q_norm_w, reps) * (1.0 / math.sqrt(HEAD_DIM)), jnp.tile(k_norm_w, reps)])
    proj, g, beta = _inproj(x2, norm_w.reshape(1, D), w_main, wab_t, qk_w,
                            a_log.reshape(-1, 1), dt_bias.reshape(-1, 1), sb_width=sb_width, tm=tm, tn=tn)
    proj3 = proj.reshape(batch, S, n_main)
    o_sb = _sb_attention(proj3, n_heads=n_sb, tq=tq, tk=tk, hp=hp)
    o_gdn = _gdn(proj3, conv_w, g, beta, out_norm_w.reshape(1, HEAD_DIM), n_heads=n_gdn,
                 col0=3 * sb_width, tt=tt)
    w_ob = w_o.astype(BF16)
    return _outproj(o_sb.reshape(T, sb_width), o_gdn.reshape(T, gdn_width),
                    w_ob[:sb_width], w_ob[sb_width:], x2, tm=tm_out)


def _moe(x2, norm_w, w_group, b_group, w_expert, b_expert, w1, w3, w2, *, tm_r, tl, tt):
    T, D = x2.shape
    tail = ROUTER_ROWS - 8 - N_EXPERTS
    wr_t = jnp.concatenate([w_group, jnp.zeros((D, 8 - N_GROUPS), F32), w_expert,
                            jnp.zeros((D, tail), F32)], axis=1).T
    br = jnp.concatenate([b_group, jnp.zeros((8 - N_GROUPS,), F32), b_expert,
                          jnp.zeros((tail,), F32)]).reshape(-1, 1)
    h, eid, gate = _router(x2, norm_w.reshape(1, D), wr_t, br, tm=tm_r)
    n_blocks = (2 * T + MOE_BLOCK - 1) // MOE_BLOCK + N_EXPERTS
    n_blk_pad = ((n_blocks + 127) // 128) * 128
    dest, blk = _plan(eid, tl=tl, n_blk_pad=n_blk_pad)
    dest_flat = dest.T.reshape(T // tt, 1, 2 * tt)
    x_slots = _scatter_rows(dest_flat, blk, h, cap=n_blocks * MOE_BLOCK, tt=tt)
    y_slots = _experts(blk, x_slots, w1.astype(BF16), w3.astype(BF16), w2.astype(BF16))
    return _combine(dest_flat, y_slots, x2, gate.T, tt=tt)


def _forward(x, norm1_w, w_in, sb_q_norm_w, sb_k_norm_w, gdn_conv_w, gdn_a_log, gdn_dt_bias,
             gdn_out_norm_w, w_o, norm2_w, w_group, b_group, w_expert, b_expert, w1, w3, w2, *, tiles):
    batch, S, D = x.shape
    x2 = x.reshape(batch * S, D)
    for l in range(norm1_w.shape[0]):
        x2 = _mixer(x2, batch, norm1_w[l], w_in[l], sb_q_norm_w[l], sb_k_norm_w[l], gdn_conv_w[l],
                    gdn_a_log[l], gdn_dt_bias[l], gdn_out_norm_w[l], w_o[l], **tiles["mixer"])
        x2 = _moe(x2, norm2_w[l], w_group[l], b_group[l], w_expert[l], b_expert[l],
                  w1[l], w3[l], w2[l], **tiles["moe"])
    return x2.reshape(batch, S, D)


_TILES = {
    "mixer": dict(tm=1024, tn=512, tq=256, tk=256, hp=2, tt=512, tm_out=512),
    "moe": dict(tm_r=512, tl=512, tt=256),
}


def kernel(x, norm1_w, w_in, sb_q_norm_w, sb_k_norm_w, gdn_conv_w, gdn_a_log, gdn_dt_bias, gdn_out_norm_w, w_o, norm2_w, w_group, b_group, w_expert, b_expert, w1, w3, w2):
    return _forward(x, norm1_w, w_in, sb_q_norm_w, sb_k_norm_w, gdn_conv_w, gdn_a_log, gdn_dt_bias,
                    gdn_out_norm_w, w_o, norm2_w, w_group, b_group, w_expert, b_expert, w1, w3, w2,
                    tiles=_TILES)
```

```python
import functools
import math

import jax
import jax.numpy as jnp
from jax import lax
from jax.experimental import pallas as pl
from jax.experimental.pallas import tpu as pltpu

F32 = jnp.float32
BF16 = jnp.bfloat16
I32 = jnp.int32

HEAD_DIM = 128
GDN_CONV = 4
GDN_CHUNK = 64
N_GROUPS = 4
EXPERTS_PER_GROUP = 8
N_EXPERTS = N_GROUPS * EXPERTS_PER_GROUP
MOE_BLOCK = 128
RMS_EPS = 1e-6
EXP_UNDERFLOW = -104.0
ROUTER_ROWS = 48
VMEM_LIMIT = 56 * 1024 * 1024


def _cparams(sem, vmem=VMEM_LIMIT):
    return pltpu.CompilerParams(dimension_semantics=sem, vmem_limit_bytes=vmem)


def _dot(a, b, dims=(((1,), (0,)), ((), ()))):
    return lax.dot_general(a, b, dims, preferred_element_type=F32)


_NT = (((1,), (1,)), ((), ()))
_TN = (((0,), (0,)), ((), ()))


def _split(a):
    hi = a.astype(BF16)
    lo = (a - hi.astype(F32)).astype(BF16)
    return hi, lo


def _dot3(a, b, dims=(((1,), (0,)), ((), ()))):
    ah, al = _split(a)
    bh, bl = _split(b)
    return _dot(ah, bh, dims) + (_dot(ah, bl, dims) + _dot(al, bh, dims))


def _softplus(x):
    return jnp.maximum(x, 0.0) + jnp.log(1.0 + jnp.exp(-jnp.abs(x)))


def _sigmoid(x):
    return 1.0 / (1.0 + jnp.exp(-x))


def _inproj_kernel(x_ref, nw_ref, w_ref, wab_ref, qkw_ref, alog_ref, dt_ref,
                   proj_ref, g_ref, beta_ref, h_ref, *, n_qk_tiles, n_heads, rows):
    j = pl.program_id(1)
    tm = x_ref.shape[0]

    @pl.when(j == 0)
    def _():
        def norm_rows(r, carry):
            rs = pl.multiple_of(r * rows, rows)
            x = x_ref[pl.ds(rs, rows), :]
            ms = jnp.mean(x * x, axis=-1, keepdims=True)
            h = x * lax.rsqrt(ms + RMS_EPS) * nw_ref[...]
            h_ref[pl.ds(rs, rows), :] = h.astype(BF16)
            return carry
        lax.fori_loop(0, tm // rows, norm_rows, 0)
        ab = _dot(wab_ref[...], h_ref[...], _NT)
        ga = ab[0:n_heads]
        gb = ab[n_heads:2 * n_heads]
        g_ref[...] = -jnp.exp(alog_ref[...]) * _softplus(ga + dt_ref[...])
        beta_ref[...] = _sigmoid(gb)

    acc = _dot(h_ref[...], w_ref[...].astype(BF16))

    @pl.when(j < n_qk_tiles)
    def _():
        wsel = jnp.where(j < n_qk_tiles // 2, qkw_ref[0:1, :], qkw_ref[1:2, :])
        for hh in range(acc.shape[1] // HEAD_DIM):
            sl = slice(hh * HEAD_DIM, (hh + 1) * HEAD_DIM)
            a = acc[:, sl]
            ms = jnp.mean(a * a, axis=-1, keepdims=True)
            proj_ref[:, sl] = (a * lax.rsqrt(ms + RMS_EPS) * wsel[:, sl]).astype(BF16)

    @pl.when(j >= n_qk_tiles)
    def _():
        proj_ref[...] = acc.astype(BF16)


def _inproj(x, norm_w, w_main, wab_t, qk_w, a_log, dt_bias, *, sb_width, n_main, tm, tn):
    T, D = x.shape
    N = n_main
    n_heads = a_log.shape[0]
    n_qk_tiles = 2 * sb_width // tn
    kern = functools.partial(_inproj_kernel, n_qk_tiles=n_qk_tiles, n_heads=n_heads,
                             rows=min(tm, 256))
    return pl.pallas_call(
        kern,
        grid=(T // tm, N // tn),
        in_specs=[
            pl.BlockSpec((tm, D), lambda i, j: (i, 0)),
            pl.BlockSpec((1, D), lambda i, j: (0, 0)),
            pl.BlockSpec((D, tn), lambda i, j: (0, j)),
            pl.BlockSpec((2 * n_heads, D), lambda i, j: (0, 0)),
            pl.BlockSpec((2, tn), lambda i, j: (0, 0)),
            pl.BlockSpec((n_heads, 1), lambda i, j: (0, 0)),
            pl.BlockSpec((n_heads, 1), lambda i, j: (0, 0)),
        ],
        out_specs=[
            pl.BlockSpec((tm, tn), lambda i, j: (i, j)),
            pl.BlockSpec((n_heads, tm), lambda i, j: (0, i)),
            pl.BlockSpec((n_heads, tm), lambda i, j: (0, i)),
        ],
        out_shape=[
            jax.ShapeDtypeStruct((T, N), BF16),
            jax.ShapeDtypeStruct((n_heads, T), F32),
            jax.ShapeDtypeStruct((n_heads, T), F32),
        ],
        scratch_shapes=[pltpu.VMEM((tm, D), BF16)],
        compiler_params=_cparams(("arbitrary", "arbitrary")),
        name="inproj",
    )(x, norm_w, w_main, wab_t, qk_w, a_log, dt_bias)


def _sb_kernel(q_ref, k_ref, v_ref, o_ref, *, tq, tk, hp):
    i = pl.program_id(2)
    heads = range(hp)
    hs = [slice(h * HEAD_DIM, (h + 1) * HEAD_DIM) for h in heads]
    q = [q_ref[0, :, hs[h]] for h in heads]
    row = lax.broadcasted_iota(I32, (tk, tk), 0)
    col = lax.broadcasted_iota(I32, (tk, tk), 1)
    upper_incl = (row >= col).astype(BF16)
    rel = lax.broadcasted_iota(I32, (tq, tk), 1) - lax.broadcasted_iota(I32, (tq, tk), 0)

    def block(ks, carry, masked):
        acc, c = carry
        z = [_dot(q[h], k_ref[0, pl.ds(ks, tk), hs[h]], _NT) for h in heads]
        log_1mb = [-_softplus(z[h]) for h in heads]
        if masked:
            causal = rel + (ks - i * tq) < 0
            log_1mb = [jnp.where(causal, log_1mb[h], 0.0) for h in heads]
        parts = [_split(log_1mb[h]) for h in heads]
        incl = [_dot(parts[h][0], upper_incl) + _dot(parts[h][1], upper_incl) for h in heads]
        w = [jnp.exp(z[h] + incl[h] + c[h]) for h in heads]
        if masked:
            w = [jnp.where(causal, w[h], 0.0) for h in heads]
        acc = [acc[h] + _dot(_bf(w[h]), v_ref[0, pl.ds(ks, tk), hs[h]]) for h in heads]
        c = [c[h] + incl[h][:, 0:1] for h in heads]
        return acc, c

    carry = ([jnp.zeros((tq, HEAD_DIM), F32) for _ in heads], [jnp.zeros((tq, 1), F32) for _ in heads])
    n_diag = tq // tk
    for d in range(n_diag - 1, -1, -1):
        carry = block(pl.multiple_of(i * tq + d * tk, tk), carry, True)

    def c_max(c):
        m = c[0]
        for h in heads[1:]:
            m = jnp.maximum(m, c[h])
        return jnp.max(m)

    def cond(state):
        jj, live, _ = state
        return jnp.logical_and(jj < i * n_diag, live)

    def body(state):
        jj, _, carry = state
        carry = block(pl.multiple_of((i * n_diag - 1 - jj) * tk, tk), carry, False)
        return jj + 1, c_max(carry[1]) > EXP_UNDERFLOW, carry

    _, _, (acc, _) = lax.while_loop(cond, body, (jnp.int32(0), c_max(carry[1]) > EXP_UNDERFLOW, carry))
    for h in heads:
        o_ref[0, :, hs[h]] = acc[h].astype(o_ref.dtype)


def _sb_attention(proj3, *, n_heads, tq, tk, hp):
    B, S, _ = proj3.shape
    kern = functools.partial(_sb_kernel, tq=tq, tk=tk, hp=hp)
    ng = n_heads // hp
    wd = hp * HEAD_DIM
    return pl.pallas_call(
        kern,
        grid=(B, ng, S // tq),
        in_specs=[
            pl.BlockSpec((1, tq, wd), lambda b, h, i: (b, i, h)),
            pl.BlockSpec((1, S, wd), lambda b, h, i: (b, 0, ng + h)),
            pl.BlockSpec((1, S, wd), lambda b, h, i: (b, 0, 2 * ng + h)),
        ],
        out_specs=pl.BlockSpec((1, tq, wd), lambda b, h, i: (b, i, h)),
        out_shape=jax.ShapeDtypeStruct((B, S, n_heads * HEAD_DIM), BF16),
        compiler_params=_cparams(("arbitrary", "arbitrary", "arbitrary")),
        name="sb_attention",
    )(proj3, proj3, proj3)


def _silu(x):
    return x * _sigmoid(x)


SUB = 2 * GDN_CHUNK


def _bf(a):
    return a.astype(BF16)


def _exact_parts(a):
    p1 = a.astype(BF16)
    r1 = a - p1.astype(F32)
    p2 = r1.astype(BF16)
    p3 = (r1 - p2.astype(F32)).astype(BF16)
    return p1, p2, p3


def _gdn_kernel(xq_ref, xk_ref, xv_ref, z_ref, cw_ref, g_ref, b_ref, ow_ref,
                o_ref, xs_ref, qkv_ref, state_ref, *, tt, n_heads):
    t = pl.program_id(1)
    C = GDN_CHUNK
    HALO = 8
    W = n_heads * HEAD_DIM
    heads = range(n_heads)

    @pl.when(t == 0)
    def _():
        xs_ref[:, 0:HALO, :] = jnp.zeros((3, HALO, W), F32)
        state_ref[...] = jnp.zeros_like(state_ref)

    @pl.when(t > 0)
    def _():
        xs_ref[:, 0:HALO, :] = xs_ref[:, tt:tt + HALO, :]

    for idx, x_ref in enumerate((xq_ref, xk_ref, xv_ref)):
        for h in heads:
            cs = slice(h * HEAD_DIM, (h + 1) * HEAD_DIM)
            xs_ref[idx, HALO:, cs] = x_ref[0, :, cs].astype(F32)
            acc = None
            for kk in range(GDN_CONV):
                wrow = cw_ref[kk:kk + 1, idx * W + h * HEAD_DIM: idx * W + (h + 1) * HEAD_DIM]
                term = xs_ref[idx, pl.ds(HALO - GDN_CONV + 1 + kk, tt), cs] * wrow
                acc = term if acc is None else acc + term
            a = _silu(acc)
            if idx < 2:
                a = a * lax.rsqrt(jnp.sum(a * a, axis=-1, keepdims=True) + RMS_EPS)
            if idx == 0:
                a = a * (1.0 / math.sqrt(HEAD_DIM))
            qkv_ref[idx, :, cs] = a

    ri = lax.broadcasted_iota(I32, (SUB, SUB), 0)
    ci = lax.broadcasted_iota(I32, (SUB, SUB), 1)
    same = (ri >> 6) == (ci >> 6)
    lower_incl = jnp.logical_and(same, ri >= ci)
    lower_strict = jnp.logical_and(same, ri > ci)
    blk16 = (ri >> 4) == (ci >> 4)
    blk32 = (ri >> 5) == (ci >> 5)
    only32 = jnp.logical_and(blk32, jnp.logical_not(blk16))
    only64 = jnp.logical_and(same, jnp.logical_not(blk32))
    eye_f = (ri == ci).astype(F32)
    eye_b = (ri == ci).astype(BF16)
    cum_b = jnp.logical_and(same, ri <= ci).astype(BF16)
    first = lax.broadcasted_iota(I32, (SUB, 1), 0) < C

    def sub_tile(s, carry):
        r0 = pl.multiple_of(s * SUB, SUB)
        g_rows = g_ref[:, pl.ds(r0, SUB)]
        b_rows = b_ref[:, pl.ds(r0, SUB)]
        gp = _exact_parts(g_rows)
        gc_rows = _dot(gp[0], cum_b) + _dot(gp[1], cum_b) + _dot(gp[2], cum_b)
        sp = _exact_parts(jnp.concatenate([gc_rows, b_rows], axis=0))
        cols = _dot(eye_b, sp[0], _NT) + _dot(eye_b, sp[1], _NT) + _dot(eye_b, sp[2], _NT)

        q, k, v, beta, decay, egc, kdec, eg_last = [], [], [], [], [], [], [], []
        for h in heads:
            cs = slice(h * HEAD_DIM, (h + 1) * HEAD_DIM)
            q.append(qkv_ref[0, pl.ds(r0, SUB), cs])
            k.append(qkv_ref[1, pl.ds(r0, SUB), cs])
            v.append(qkv_ref[2, pl.ds(r0, SUB), cs])
            gc_col = cols[:, h:h + 1]
            beta.append(cols[:, n_heads + h:n_heads + h + 1])
            decay.append(jnp.where(lower_incl, jnp.exp(jnp.minimum(gc_col - gc_rows[h:h + 1, :], 0.0)), 0.0))
            egc.append(jnp.exp(gc_col))
            g_last = jnp.where(first, gc_col[C - 1:C, :], gc_col[SUB - 1:SUB, :])
            kdec.append(_bf(k[h] * jnp.exp(g_last - gc_col)))
            eg_last.append((jnp.exp(gc_col[C - 1:C, :]), jnp.exp(gc_col[SUB - 1:SUB, :])))

        kb = [k[h] * beta[h] for h in heads]
        k_b = [_bf(k[h]) for h in heads]
        kb_b = [_bf(kb[h]) for h in heads]
        lmat = [jnp.where(lower_strict, _dot(kb_b[h], k_b[h], _NT) * decay[h], 0.0) for h in heads]
        d16 = [jnp.where(blk16, lmat[h], 0.0) for h in heads]
        d16_b = [_bf(d16[h]) for h in heads]
        p2 = [_bf(_dot(d16_b[h], d16_b[h])) for h in heads]
        p4 = [_bf(_dot(p2[h], p2[h])) for h in heads]
        p8 = [_bf(_dot(p4[h], p4[h])) for h in heads]
        x0 = [eye_f - d16[h] for h in heads]
        x0 = [x0[h] + _dot(_bf(x0[h]), p2[h]) for h in heads]
        x0 = [x0[h] + _dot(_bf(x0[h]), p4[h]) for h in heads]
        x0 = [x0[h] + _dot(_bf(x0[h]), p8[h]) for h in heads]
        x0_b = [_bf(x0[h]) for h in heads]
        y1 = [_bf(_dot(x0_b[h], _bf(jnp.where(only32, lmat[h], 0.0)))) for h in heads]
        x1 = [x0[h] - _dot(y1[h], x0_b[h]) for h in heads]
        x1_b = [_bf(x1[h]) for h in heads]
        y2 = [_bf(_dot(x1_b[h], _bf(jnp.where(only64, lmat[h], 0.0)))) for h in heads]
        t_b = [_bf(x1[h] - _dot(y2[h], x1_b[h])) for h in heads]
        u = [_dot(t_b[h], _bf(v[h] * beta[h])) for h in heads]
        w = [_dot(t_b[h], _bf(kb[h] * egc[h])) for h in heads]
        attn = [_bf(jnp.where(lower_incl, _dot(_bf(q[h]), k_b[h], _NT) * decay[h], 0.0)) for h in heads]
        qg = [q[h] * egc[h] for h in heads]

        state = [state_ref[h] for h in heads]
        zeros = jnp.zeros((C, HEAD_DIM), F32)
        for c in range(2):
            rows = slice(c * C, (c + 1) * C)
            wq = [_bf(jnp.concatenate([w[h][rows], qg[h][rows]], axis=0)) for h in heads]
            r = [_dot(wq[h], _bf(state[h])) for h in heads]
            v_new = [u[h][rows] - r[h][0:C] for h in heads]
            v_pad = [_bf(jnp.concatenate([v_new[h], zeros] if c == 0 else [zeros, v_new[h]], axis=0))
                     for h in heads]
            o = [r[h][C:SUB] + _dot(attn[h][rows], v_pad[h]) for h in heads]
            state = [state[h] * eg_last[h][c] + _dot(kdec[h][rows], _bf(v_new[h]), _TN) for h in heads]
            for h in heads:
                cs = slice(h * HEAD_DIM, (h + 1) * HEAD_DIM)
                ms = jnp.mean(o[h] * o[h], axis=-1, keepdims=True)
                zc = z_ref[0, pl.ds(r0 + c * C, C), cs].astype(F32)
                o_ref[0, pl.ds(r0 + c * C, C), cs] = (
                    o[h] * lax.rsqrt(ms + RMS_EPS) * ow_ref[...] * _silu(zc)).astype(o_ref.dtype)
        for h in heads:
            state_ref[h] = state[h]
        return carry

    lax.fori_loop(0, tt // SUB, sub_tile, 0)


def _gdn(proj3, conv_w, g, beta, out_norm_w, *, n_heads, col0, tt):
    B, S, _ = proj3.shape
    W = n_heads * HEAD_DIM
    kern = functools.partial(_gdn_kernel, tt=tt, n_heads=n_heads)
    cb = col0 // W
    nt = S // tt
    xspec = lambda off: pl.BlockSpec((1, tt, W), lambda b, t: (b, t, cb + off))
    gspec = pl.BlockSpec((n_heads, tt), lambda b, t: (0, b * nt + t))
    return pl.pallas_call(
        kern,
        grid=(B, nt),
        in_specs=[xspec(0), xspec(1), xspec(2), xspec(3),
                  pl.BlockSpec((GDN_CONV, 3 * W), lambda b, t: (0, 0)),
                  gspec, gspec, pl.BlockSpec((1, HEAD_DIM), lambda b, t: (0, 0))],
        out_specs=pl.BlockSpec((1, tt, W), lambda b, t: (b, t, 0)),
        out_shape=jax.ShapeDtypeStruct((B, S, W), BF16),
        scratch_shapes=[pltpu.VMEM((3, tt + 8, W), F32), pltpu.VMEM((3, tt, W), F32),
                        pltpu.VMEM((n_heads, HEAD_DIM, HEAD_DIM), F32)],
        compiler_params=_cparams(("arbitrary", "arbitrary")),
        name="gdn",
    )(proj3, proj3, proj3, proj3, conv_w, g, beta, out_norm_w)


def _outproj_kernel(a_ref, b_ref, wa_ref, wb_ref, x_ref, o_ref):
    o_ref[...] = x_ref[...] + _dot(a_ref[...], wa_ref[...]) + _dot(b_ref[...], wb_ref[...])


def _outproj(o_sb, o_gdn, w_top, w_bot, x, *, tm):
    T, D = x.shape
    wa, wb = o_sb.shape[1], o_gdn.shape[1]
    return pl.pallas_call(
        _outproj_kernel,
        grid=(T // tm,),
        in_specs=[
            pl.BlockSpec((tm, wa), lambda i: (i, 0)),
            pl.BlockSpec((tm, wb), lambda i: (i, 0)),
            pl.BlockSpec((wa, D), lambda i: (0, 0)),
            pl.BlockSpec((wb, D), lambda i: (0, 0)),
            pl.BlockSpec((tm, D), lambda i: (i, 0)),
        ],
        out_specs=pl.BlockSpec((tm, D), lambda i: (i, 0)),
        out_shape=jax.ShapeDtypeStruct((T, D), F32),
        compiler_params=_cparams(("arbitrary",)),
        name="outproj",
    )(o_sb, o_gdn, w_top, w_bot, x)


def _router_kernel(x_ref, nw_ref, wr_ref, br_ref, h_ref, eid_ref, gate_ref):
    x = x_ref[...]
    tm = x.shape[0]
    ms = jnp.mean(x * x, axis=-1, keepdims=True)
    h = x * lax.rsqrt(ms + RMS_EPS) * nw_ref[...]
    h_ref[...] = h
    logits = _dot3(wr_ref[...], h, _NT) + br_ref[...]
    best = logits[0:1]
    gidx = jnp.zeros((1, tm), I32)
    for g in range(1, N_GROUPS):
        better = logits[g:g + 1] > best
        gidx = jnp.where(better, g, gidx)
        best = jnp.where(better, logits[g:g + 1], best)
    gsum = jnp.zeros((1, tm), F32)
    for g in range(N_GROUPS):
        gsum = gsum + jnp.exp(logits[g:g + 1] - best)
    group_gate = 1.0 / gsum
    E = EXPERTS_PER_GROUP
    in_group = jnp.zeros((E, tm), F32)
    for g in range(N_GROUPS):
        in_group = jnp.where(gidx == g, logits[8 + g * E:8 + (g + 1) * E], in_group)
    sub = lax.broadcasted_iota(I32, (E, tm), 0)
    m1 = jnp.max(in_group, axis=0, keepdims=True)
    i1 = jnp.min(jnp.where(in_group == m1, sub, E), axis=0, keepdims=True)
    rest = jnp.where(sub == i1, -jnp.inf, in_group)
    m2 = jnp.max(rest, axis=0, keepdims=True)
    i2 = jnp.min(jnp.where(rest == m2, sub, E), axis=0, keepdims=True)
    e2 = jnp.exp(m2 - m1)
    inv = group_gate / (1.0 + e2)
    eid_ref[0:1, :] = gidx * E + i1
    eid_ref[1:2, :] = gidx * E + i2
    gate_ref[0:1, :] = inv
    gate_ref[1:2, :] = inv * e2


def _router(x, norm_w, wr_t, br, *, tm):
    T, D = x.shape
    return pl.pallas_call(
        _router_kernel,
        grid=(T // tm,),
        in_specs=[
            pl.BlockSpec((tm, D), lambda i: (i, 0)),
            pl.BlockSpec((1, D), lambda i: (0, 0)),
            pl.BlockSpec((ROUTER_ROWS, D), lambda i: (0, 0)),
            pl.BlockSpec((ROUTER_ROWS, 1), lambda i: (0, 0)),
        ],
        out_specs=[
            pl.BlockSpec((tm, D), lambda i: (i, 0)),
            pl.BlockSpec((2, tm), lambda i: (0, i)),
            pl.BlockSpec((2, tm), lambda i: (0, i)),
        ],
        out_shape=[
            jax.ShapeDtypeStruct((T, D), F32),
            jax.ShapeDtypeStruct((2, T), I32),
            jax.ShapeDtypeStruct((2, T), F32),
        ],
        compiler_params=_cparams(("arbitrary",)),
        name="router",
    )(x, norm_w, wr_t, br)


def _plan_kernel(eid_ref, dest_ref, blk_ref, cnt_ref, *, tl, n_blk_pad):
    p = pl.program_id(0)
    i = pl.program_id(1)
    NE = N_EXPERTS
    e0 = eid_ref[0:1, :]
    e1 = eid_ref[1:2, :]
    sub = lax.broadcasted_iota(I32, (NE, tl), 0)
    hot0 = sub == e0
    hot1 = sub == e1
    onehot = jnp.logical_or(hot0, hot1).astype(BF16)
    ones = jnp.ones((tl, HEAD_DIM), BF16)

    @pl.when(jnp.logical_and(p == 0, i == 0))
    def _():
        cnt_ref[0] = jnp.zeros((NE, HEAD_DIM), F32)

    @pl.when(p == 0)
    def _():
        cnt_ref[0] += _dot(onehot, ones)

    @pl.when(jnp.logical_and(p == 1, i == 0))
    def _():
        cnt = cnt_ref[0]
        padded = jnp.floor((cnt + (MOE_BLOCK - 1)) * (1.0 / MOE_BLOCK)) * MOE_BLOCK
        er = lax.broadcasted_iota(I32, (NE, NE), 0)
        ec = lax.broadcasted_iota(I32, (NE, NE), 1)
        start = _dot3((ec < er).astype(F32), padded)
        cnt_ref[1] = start
        end_col = (start + padded)[:, 0:1]
        pos = (lax.broadcasted_iota(I32, (NE, n_blk_pad), 1) * MOE_BLOCK).astype(F32)
        n_before = jnp.sum((end_col <= pos).astype(I32), axis=0, keepdims=True)
        blk_ref[0:1, :] = jnp.minimum(n_before, NE - 1)
        total = jnp.max(end_col, axis=0, keepdims=True)
        blk_ref[1:2, :] = jnp.broadcast_to((total * (1.0 / MOE_BLOCK)).astype(I32), (1, n_blk_pad))
        cnt_ref[0] = jnp.zeros((NE, HEAD_DIM), F32)

    @pl.when(p == 1)
    def _():
        r = lax.broadcasted_iota(I32, (tl, tl), 0)
        c = lax.broadcasted_iota(I32, (tl, tl), 1)
        before = (r < c).astype(BF16)
        run = cnt_ref[0][:, 0:1] + cnt_ref[1][:, 0:1]
        slot = _dot(onehot, before) + run
        dest_ref[0:1, :] = jnp.sum(jnp.where(hot0, slot, 0.0), axis=0, keepdims=True).astype(I32)
        dest_ref[1:2, :] = jnp.sum(jnp.where(hot1, slot, 0.0), axis=0, keepdims=True).astype(I32)
        cnt_ref[0] += _dot(onehot, ones)


def _plan(eid, *, tl, n_blk_pad):
    T = eid.shape[1]
    kern = functools.partial(_plan_kernel, tl=tl, n_blk_pad=n_blk_pad)
    return pl.pallas_call(
        kern,
        grid=(2, T // tl),
        in_specs=[pl.BlockSpec((2, tl), lambda p, i: (0, i))],
        out_specs=[
            pl.BlockSpec((2, tl), lambda p, i: (0, i * p)),
            pl.BlockSpec((2, n_blk_pad), lambda p, i: (0, 0)),
        ],
        out_shape=[
            jax.ShapeDtypeStruct((2, T), I32),
            jax.ShapeDtypeStruct((2, n_blk_pad), I32),
        ],
        scratch_shapes=[pltpu.VMEM((2, N_EXPERTS, HEAD_DIM), F32)],
        compiler_params=_cparams(("arbitrary", "arbitrary")),
        name="plan",
    )(eid)


def _scatter_kernel(dest_ref, blk_ref, h_ref, xs_ref, zero_ref, sem, zsem, *, tt, n_blocks):
    i = pl.program_id(0)

    @pl.when(i == 0)
    def _():
        zero_ref[...] = jnp.zeros_like(zero_ref)
        n_valid = blk_ref[1, 0]

        def clear(b, n):
            last = blk_ref[0, b] != blk_ref[0, jnp.minimum(b + 1, n_blocks - 1)]
            do = jnp.logical_or(b >= n_valid - 1, last)

            @pl.when(do)
            def _():
                pltpu.make_async_copy(zero_ref, xs_ref.at[pl.ds(b * MOE_BLOCK, MOE_BLOCK)], zsem).start()
            return n + do.astype(I32)
        n_started = lax.fori_loop(0, n_blocks, clear, 0)

        def drain(b, carry):
            pltpu.make_async_copy(zero_ref, xs_ref.at[pl.ds(0, MOE_BLOCK)], zsem).wait()
            return carry
        lax.fori_loop(0, n_started, drain, 0)

    def issue(r, carry):
        src = h_ref.at[pl.ds(r, 1)]
        pltpu.make_async_copy(src, xs_ref.at[pl.ds(dest_ref[0, 0, 2 * r], 1)], sem).start()
        pltpu.make_async_copy(src, xs_ref.at[pl.ds(dest_ref[0, 0, 2 * r + 1], 1)], sem).start()
        return carry
    lax.fori_loop(0, tt, issue, 0)

    def drain_rows(r, carry):
        pltpu.make_async_copy(h_ref.at[pl.ds(0, 1)], xs_ref.at[pl.ds(0, 1)], sem).wait()
        return carry
    lax.fori_loop(0, 2 * tt, drain_rows, 0)


def _scatter_rows(dest_flat, blk, h, *, cap, tt):
    T, D = h.shape
    n_blocks = cap // MOE_BLOCK
    kern = functools.partial(_scatter_kernel, tt=tt, n_blocks=n_blocks)
    return pl.pallas_call(
        kern,
        grid=(T // tt,),
        in_specs=[
            pl.BlockSpec((1, 1, 2 * tt), lambda i: (i, 0, 0), memory_space=pltpu.SMEM),
            pl.BlockSpec(memory_space=pltpu.SMEM),
            pl.BlockSpec((tt, D), lambda i: (i, 0)),
        ],
        out_specs=pl.BlockSpec(memory_space=pl.ANY),
        out_shape=jax.ShapeDtypeStruct((cap, D), h.dtype),
        scratch_shapes=[pltpu.VMEM((MOE_BLOCK, D), h.dtype), pltpu.SemaphoreType.DMA, pltpu.SemaphoreType.DMA],
        compiler_params=_cparams(("arbitrary",)),
        name="scatter_rows",
    )(dest_flat, blk, h)


def _expert_kernel(blk_ref, x_ref, w1_ref, w3_ref, w2_ref, y_ref, w1b_ref, w3b_ref, w2b_ref):
    b = pl.program_id(0)
    valid = b < blk_ref[1, 0]
    new_expert = jnp.logical_or(b == 0, blk_ref[0, b] != blk_ref[0, jnp.maximum(b - 1, 0)])

    @pl.when(jnp.logical_and(valid, new_expert))
    def _():
        w1b_ref[...] = w1_ref[0].astype(BF16)
        w3b_ref[...] = w3_ref[0].astype(BF16)
        w2b_ref[...] = w2_ref[0].astype(BF16)

    @pl.when(valid)
    def _():
        x = x_ref[...].astype(BF16)
        hid = _silu(_dot(x, w1b_ref[...])) * _dot(x, w3b_ref[...])
        y_ref[...] = _dot(hid.astype(BF16), w2b_ref[...]).astype(y_ref.dtype)

    @pl.when(b >= blk_ref[1, 0])
    def _():
        y_ref[...] = jnp.zeros_like(y_ref)


def _experts(blk, x_slots, w1, w3, w2):
    cap, D = x_slots.shape
    n_blocks = cap // MOE_BLOCK
    F = w1.shape[2]

    def row_blk(b, blk_ref):
        return jnp.minimum(b, blk_ref[1, 0] - 1)

    grid_spec = pltpu.PrefetchScalarGridSpec(
        num_scalar_prefetch=1,
        grid=(n_blocks,),
        in_specs=[
            pl.BlockSpec((MOE_BLOCK, D), lambda b, blk_ref: (row_blk(b, blk_ref), 0)),
            pl.BlockSpec((1, D, F), lambda b, blk_ref: (blk_ref[0, row_blk(b, blk_ref)], 0, 0)),
            pl.BlockSpec((1, D, F), lambda b, blk_ref: (blk_ref[0, row_blk(b, blk_ref)], 0, 0)),
            pl.BlockSpec((1, F, D), lambda b, blk_ref: (blk_ref[0, row_blk(b, blk_ref)], 0, 0)),
        ],
        out_specs=pl.BlockSpec((MOE_BLOCK, D), lambda b, blk_ref: (b, 0)),
        scratch_shapes=[pltpu.VMEM((D, F), BF16), pltpu.VMEM((D, F), BF16), pltpu.VMEM((F, D), BF16)],
    )
    return pl.pallas_call(
        _expert_kernel,
        grid_spec=grid_spec,
        out_shape=jax.ShapeDtypeStruct((cap, D), F32),
        compiler_params=_cparams(("arbitrary",)),
        name="experts",
    )(blk, x_slots, w1, w3, w2)


def _combine_kernel(dest_ref, dnext_ref, ys_ref, x_ref, gate_ref, o_ref, buf_ref, sems, *, tt):
    i = pl.program_id(0)
    slot = lax.rem(i, 2)

    def issue(d_ref, s):
        def one(r, carry):
            for kk in range(2):
                pltpu.make_async_copy(ys_ref.at[pl.ds(d_ref[0, 0, 2 * r + kk], 1)],
                                      buf_ref.at[s, kk, pl.ds(r, 1)], sems.at[s]).start()
            return carry
        lax.fori_loop(0, tt, one, 0)

    @pl.when(i == 0)
    def _():
        issue(dest_ref, 0)

    @pl.when(i + 1 < pl.num_programs(0))
    def _():
        issue(dnext_ref, 1 - slot)

    def drain(r, carry):
        pltpu.make_async_copy(ys_ref.at[pl.ds(0, 1)], buf_ref.at[slot, 0, pl.ds(0, 1)], sems.at[slot]).wait()
        return carry
    lax.fori_loop(0, 2 * tt, drain, 0)
    g = gate_ref[...]
    o_ref[...] = x_ref[...] + (buf_ref[slot, 0] * g[:, 0:1] + buf_ref[slot, 1] * g[:, 1:2])


def _combine(dest_flat, y_slots, x, gate_t, *, tt):
    T, D = x.shape
    n = T // tt
    kern = functools.partial(_combine_kernel, tt=tt)
    return pl.pallas_call(
        kern,
        grid=(n,),
        in_specs=[
            pl.BlockSpec((1, 1, 2 * tt), lambda i: (i, 0, 0), memory_space=pltpu.SMEM),
            pl.BlockSpec((1, 1, 2 * tt), lambda i: (jnp.minimum(i + 1, n - 1), 0, 0), memory_space=pltpu.SMEM),
            pl.BlockSpec(memory_space=pl.ANY),
            pl.BlockSpec((tt, D), lambda i: (i, 0)),
            pl.BlockSpec((tt, 2), lambda i: (i, 0)),
        ],
        out_specs=pl.BlockSpec((tt, D), lambda i: (i, 0)),
        out_shape=jax.ShapeDtypeStruct((T, D), F32),
        scratch_shapes=[pltpu.VMEM((2, 2, tt, D), y_slots.dtype), pltpu.SemaphoreType.DMA((2,))],
        compiler_params=_cparams(("arbitrary",)),
        name="combine",
    )(dest_flat, dest_flat, y_slots, x, gate_t)


def _mixer(x2, batch, norm_w, w_in, q_norm_w, k_norm_w, conv_w, a_log, dt_bias, out_norm_w, w_o,
           *, tm, tn, tq, tk, hp, tt, tm_out):
    T, D = x2.shape
    S = T // batch
    n_gdn = a_log.shape[0]
    gdn_width = n_gdn * HEAD_DIM
    n_main = w_in.shape[1] - 2 * n_gdn
    sb_width = (n_main - 4 * gdn_width) // 3
    n_sb = sb_width // HEAD_DIM
    wab_t = w_in[:, n_main:].T.astype(BF16)
    reps = tn // HEAD_DIM
    qk_w = jnp.stack([jnp.tile(q_norm_w, reps) * (1.0 / math.sqrt(HEAD_DIM)), jnp.tile(k_norm_w, reps)])
    proj, g, beta = _inproj(x2, norm_w.reshape(1, D), w_in, wab_t, qk_w,
                            a_log.reshape(-1, 1), dt_bias.reshape(-1, 1), sb_width=sb_width,
                            n_main=n_main, tm=tm, tn=tn)
    proj3 = proj.reshape(batch, S, n_main)
    o_sb = _sb_attention(proj3, n_heads=n_sb, tq=tq, tk=tk, hp=hp)
    o_gdn = _gdn(proj3, conv_w, g, beta, out_norm_w.reshape(1, HEAD_DIM), n_heads=n_gdn,
                 col0=3 * sb_width, tt=tt)
    w_ob = w_o.astype(BF16)
    return _outproj(o_sb.reshape(T, sb_width), o_gdn.reshape(T, gdn_width),
                    w_ob[:sb_width], w_ob[sb_width:], x2, tm=tm_out)


def _moe(x2, norm_w, w_group, b_group, w_expert, b_expert, w1, w3, w2, *, tm_r, tl, tt):
    T, D = x2.shape
    tail = ROUTER_ROWS - 8 - N_EXPERTS
    wr_t = jnp.concatenate([w_group, jnp.zeros((D, 8 - N_GROUPS), F32), w_expert,
                            jnp.zeros((D, tail), F32)], axis=1).T
    br = jnp.concatenate([b_group, jnp.zeros((8 - N_GROUPS,), F32), b_expert,
                          jnp.zeros((tail,), F32)]).reshape(-1, 1)
    h, eid, gate = _router(x2, norm_w.reshape(1, D), wr_t, br, tm=tm_r)
    n_blocks = (2 * T + MOE_BLOCK - 1) // MOE_BLOCK + N_EXPERTS
    n_blk_pad = ((n_blocks + 127) // 128) * 128
    dest, blk = _plan(eid, tl=tl, n_blk_pad=n_blk_pad)
    dest_flat = dest.T.reshape(T // tt, 1, 2 * tt)
    x_slots = _scatter_rows(dest_flat, blk, h, cap=n_blocks * MOE_BLOCK, tt=tt)
    y_slots = _experts(blk, x_slots, w1, w3, w2)
    return _combine(dest_flat, y_slots, x2, gate.T, tt=tt)


def _forward(x, norm1_w, w_in, sb_q_norm_w, sb_k_norm_w, gdn_conv_w, gdn_a_log, gdn_dt_bias,
             gdn_out_norm_w, w_o, norm2_w, w_group, b_group, w_expert, b_expert, w1, w3, w2, *, tiles):
    batch, S, D = x.shape
    x2 = x.reshape(batch * S, D)
    for l in range(norm1_w.shape[0]):
        x2 = _mixer(x2, batch, norm1_w[l], w_in[l], sb_q_norm_w[l], sb_k_norm_w[l], gdn_conv_w[l],
                    gdn_a_log[l], gdn_dt_bias[l], gdn_out_norm_w[l], w_o[l], **tiles["mixer"])
        x2 = _moe(x2, norm2_w[l], w_group[l], b_group[l], w_expert[l], b_expert[l],
                  w1[l], w3[l], w2[l], **tiles["moe"])
    return x2.reshape(batch, S, D)


_TILES = {
    "mixer": dict(tm=1024, tn=512, tq=256, tk=256, hp=2, tt=512, tm_out=512),
    "moe": dict(tm_r=512, tl=512, tt=256),
}


def kernel(x, norm1_w, w_in, sb_q_norm_w, sb_k_norm_w, gdn_conv_w, gdn_a_log, gdn_dt_bias, gdn_out_norm_w, w_o, norm2_w, w_group, b_group, w_expert, b_expert, w1, w3, w2):
    return _forward(x, norm1_w, w_in, sb_q_norm_w, sb_k_norm_w, gdn_conv_w, gdn_a_log, gdn_dt_bias,
                    gdn_out_norm_w, w_o, norm2_w, w_group, b_group, w_expert, b_expert, w1, w3, w2,
                    tiles=_TILES)
```

```python
import functools
import math

import jax
import jax.numpy as jnp
from jax import lax
from jax.experimental import pallas as pl
from jax.experimental.pallas import tpu as pltpu

F32 = jnp.float32
BF16 = jnp.bfloat16
I32 = jnp.int32

HEAD_DIM = 128
GDN_CONV = 4
GDN_CHUNK = 64
N_GROUPS = 4
EXPERTS_PER_GROUP = 8
N_EXPERTS = N_GROUPS * EXPERTS_PER_GROUP
MOE_BLOCK = 128
RMS_EPS = 1e-6
EXP_UNDERFLOW = -104.0
ROUTER_ROWS = 48
VMEM_LIMIT = 56 * 1024 * 1024


def _cparams(sem, vmem=VMEM_LIMIT):
    return pltpu.CompilerParams(dimension_semantics=sem, vmem_limit_bytes=vmem)


def _dot(a, b, dims=(((1,), (0,)), ((), ()))):
    return lax.dot_general(a, b, dims, preferred_element_type=F32)


_NT = (((1,), (1,)), ((), ()))
_TN = (((0,), (0,)), ((), ()))


def _split(a):
    hi = a.astype(BF16)
    lo = (a - hi.astype(F32)).astype(BF16)
    return hi, lo


def _dot3(a, b, dims=(((1,), (0,)), ((), ()))):
    ah, al = _split(a)
    bh, bl = _split(b)
    return _dot(ah, bh, dims) + (_dot(ah, bl, dims) + _dot(al, bh, dims))


U32 = jnp.uint32
PACK_LANES = 2 * HEAD_DIM


def _pack_words(a):
    half = a.shape[1] // 2
    bits = lax.bitcast_convert_type(a.astype(BF16).astype(F32), U32)
    return [(bits[:, half + s * HEAD_DIM:half + (s + 1) * HEAD_DIM] & jnp.uint32(0xFFFF0000))
            | (bits[:, s * HEAD_DIM:(s + 1) * HEAD_DIM] >> 16) for s in range(half // HEAD_DIM)]


def _unpack_words(words):
    lo = [lax.bitcast_convert_type(w << 16, F32) for w in words]
    hi = [lax.bitcast_convert_type(w & jnp.uint32(0xFFFF0000), F32) for w in words]
    return lo, hi


def _softplus(x):
    return jnp.maximum(x, 0.0) + jnp.log(1.0 + jnp.exp(-jnp.abs(x)))


def _sigmoid(x):
    return 1.0 / (1.0 + jnp.exp(-x))


def _inproj_kernel(x_ref, nw_ref, w_ref, wab_ref, qkw_ref, alog_ref, dt_ref,
                   proj_ref, g_ref, beta_ref, h_ref, *, n_qk_tiles, n_heads, rows):
    j = pl.program_id(1)
    tm = x_ref.shape[0]

    @pl.when(j == 0)
    def _():
        def norm_rows(r, carry):
            rs = pl.multiple_of(r * rows, rows)
            x = x_ref[pl.ds(rs, rows), :]
            ms = jnp.mean(x * x, axis=-1, keepdims=True)
            h = x * lax.rsqrt(ms + RMS_EPS) * nw_ref[...]
            h_ref[pl.ds(rs, rows), :] = h.astype(BF16)
            return carry
        lax.fori_loop(0, tm // rows, norm_rows, 0)
        ab = _dot(wab_ref[...], h_ref[...], _NT)
        ga = ab[0:n_heads]
        gb = ab[n_heads:2 * n_heads]
        g_ref[...] = -jnp.exp(alog_ref[...]) * _softplus(ga + dt_ref[...])
        beta_ref[...] = _sigmoid(gb)

    acc = _dot(h_ref[...], w_ref[0].astype(BF16))

    @pl.when(j < n_qk_tiles)
    def _():
        wsel = jnp.where(j < n_qk_tiles // 2, qkw_ref[0:1, :], qkw_ref[1:2, :])
        for hh in range(acc.shape[1] // HEAD_DIM):
            sl = slice(hh * HEAD_DIM, (hh + 1) * HEAD_DIM)
            a = acc[:, sl]
            ms = jnp.mean(a * a, axis=-1, keepdims=True)
            proj_ref[:, sl] = (a * lax.rsqrt(ms + RMS_EPS) * wsel[:, sl]).astype(BF16)

    @pl.when(j >= n_qk_tiles)
    def _():
        proj_ref[...] = acc.astype(BF16)


def _inproj(x, norm_w, w_main, wab_t, qk_w, a_log, dt_bias, *, layer, sb_width, n_main, tm, tn):
    T, D = x.shape
    N = n_main
    n_heads = a_log.shape[0]
    n_qk_tiles = 2 * sb_width // tn
    kern = functools.partial(_inproj_kernel, n_qk_tiles=n_qk_tiles, n_heads=n_heads,
                             rows=min(tm, 256))
    return pl.pallas_call(
        kern,
        grid=(T // tm, N // tn),
        in_specs=[
            pl.BlockSpec((tm, D), lambda i, j: (i, 0)),
            pl.BlockSpec((1, D), lambda i, j: (0, 0)),
            pl.BlockSpec((1, D, tn), lambda i, j: (layer, 0, j)),
            pl.BlockSpec((2 * n_heads, D), lambda i, j: (0, 0)),
            pl.BlockSpec((2, tn), lambda i, j: (0, 0)),
            pl.BlockSpec((n_heads, 1), lambda i, j: (0, 0)),
            pl.BlockSpec((n_heads, 1), lambda i, j: (0, 0)),
        ],
        out_specs=[
            pl.BlockSpec((tm, tn), lambda i, j: (i, j)),
            pl.BlockSpec((n_heads, tm), lambda i, j: (0, i)),
            pl.BlockSpec((n_heads, tm), lambda i, j: (0, i)),
        ],
        out_shape=[
            jax.ShapeDtypeStruct((T, N), BF16),
            jax.ShapeDtypeStruct((n_heads, T), F32),
            jax.ShapeDtypeStruct((n_heads, T), F32),
        ],
        scratch_shapes=[pltpu.VMEM((tm, D), BF16)],
        compiler_params=_cparams(("arbitrary", "arbitrary")),
        name="inproj",
    )(x, norm_w, w_main, wab_t, qk_w, a_log, dt_bias)


def _sb_kernel(q_ref, k_ref, v_ref, o_ref, *, tq, tk, hp):
    i = pl.program_id(2)
    heads = range(hp)
    hs = [slice(h * HEAD_DIM, (h + 1) * HEAD_DIM) for h in heads]
    q = [q_ref[0, :, hs[h]] for h in heads]
    row = lax.broadcasted_iota(I32, (tk, tk), 0)
    col = lax.broadcasted_iota(I32, (tk, tk), 1)
    upper_incl = (row >= col).astype(BF16)
    rel = lax.broadcasted_iota(I32, (tq, tk), 1) - lax.broadcasted_iota(I32, (tq, tk), 0)

    def block(ks, carry, masked):
        acc, c = carry
        z = [_dot(q[h], k_ref[0, pl.ds(ks, tk), hs[h]], _NT) for h in heads]
        log_1mb = [-_softplus(z[h]) for h in heads]
        if masked:
            causal = rel + (ks - i * tq) < 0
            log_1mb = [jnp.where(causal, log_1mb[h], 0.0) for h in heads]
        parts = [_split(log_1mb[h]) for h in heads]
        incl = [_dot(parts[h][0], upper_incl) + _dot(parts[h][1], upper_incl) for h in heads]
        w = [jnp.exp(z[h] + incl[h] + c[h]) for h in heads]
        if masked:
            w = [jnp.where(causal, w[h], 0.0) for h in heads]
        acc = [acc[h] + _dot(_bf(w[h]), v_ref[0, pl.ds(ks, tk), hs[h]]) for h in heads]
        c = [c[h] + incl[h][:, 0:1] for h in heads]
        return acc, c

    carry = ([jnp.zeros((tq, HEAD_DIM), F32) for _ in heads], [jnp.zeros((tq, 1), F32) for _ in heads])
    n_diag = tq // tk
    for d in range(n_diag - 1, -1, -1):
        carry = block(pl.multiple_of(i * tq + d * tk, tk), carry, True)

    def c_max(c):
        m = c[0]
        for h in heads[1:]:
            m = jnp.maximum(m, c[h])
        return jnp.max(m)

    def cond(state):
        jj, live, _ = state
        return jnp.logical_and(jj < i * n_diag, live)

    def body(state):
        jj, _, carry = state
        carry = block(pl.multiple_of((i * n_diag - 1 - jj) * tk, tk), carry, False)
        return jj + 1, c_max(carry[1]) > EXP_UNDERFLOW, carry

    _, _, (acc, _) = lax.while_loop(cond, body, (jnp.int32(0), c_max(carry[1]) > EXP_UNDERFLOW, carry))
    for h in heads:
        o_ref[0, :, hs[h]] = acc[h].astype(o_ref.dtype)


def _sb_attention(proj3, *, n_heads, tq, tk, hp):
    B, S, _ = proj3.shape
    kern = functools.partial(_sb_kernel, tq=tq, tk=tk, hp=hp)
    ng = n_heads // hp
    wd = hp * HEAD_DIM
    return pl.pallas_call(
        kern,
        grid=(B, ng, S // tq),
        in_specs=[
            pl.BlockSpec((1, tq, wd), lambda b, h, i: (b, i, h)),
            pl.BlockSpec((1, S, wd), lambda b, h, i: (b, 0, ng + h)),
            pl.BlockSpec((1, S, wd), lambda b, h, i: (b, 0, 2 * ng + h)),
        ],
        out_specs=pl.BlockSpec((1, tq, wd), lambda b, h, i: (b, i, h)),
        out_shape=jax.ShapeDtypeStruct((B, S, n_heads * HEAD_DIM), BF16),
        compiler_params=_cparams(("arbitrary", "arbitrary", "arbitrary")),
        name="sb_attention",
    )(proj3, proj3, proj3)


def _silu(x):
    return x * _sigmoid(x)


SUB = 2 * GDN_CHUNK


def _bf(a):
    return a.astype(BF16)


def _exact_parts(a):
    p1 = a.astype(BF16)
    r1 = a - p1.astype(F32)
    p2 = r1.astype(BF16)
    p3 = (r1 - p2.astype(F32)).astype(BF16)
    return p1, p2, p3


def _gdn_kernel(xq_ref, xk_ref, xv_ref, z_ref, cw_ref, g_ref, b_ref, ow_ref,
                o_ref, xs_ref, qkv_ref, state_ref, *, tt, n_heads):
    t = pl.program_id(1)
    C = GDN_CHUNK
    HALO = 8
    W = n_heads * HEAD_DIM
    heads = range(n_heads)

    @pl.when(t == 0)
    def _():
        xs_ref[:, 0:HALO, :] = jnp.zeros((3, HALO, W), F32)
        state_ref[...] = jnp.zeros_like(state_ref)

    @pl.when(t > 0)
    def _():
        xs_ref[:, 0:HALO, :] = xs_ref[:, tt:tt + HALO, :]

    for idx, x_ref in enumerate((xq_ref, xk_ref, xv_ref)):
        for h in heads:
            cs = slice(h * HEAD_DIM, (h + 1) * HEAD_DIM)
            xs_ref[idx, HALO:, cs] = x_ref[0, :, cs].astype(F32)
            acc = None
            for kk in range(GDN_CONV):
                wrow = cw_ref[kk:kk + 1, idx * W + h * HEAD_DIM: idx * W + (h + 1) * HEAD_DIM]
                term = xs_ref[idx, pl.ds(HALO - GDN_CONV + 1 + kk, tt), cs] * wrow
                acc = term if acc is None else acc + term
            a = _silu(acc)
            if idx < 2:
                a = a * lax.rsqrt(jnp.sum(a * a, axis=-1, keepdims=True) + RMS_EPS)
            if idx == 0:
                a = a * (1.0 / math.sqrt(HEAD_DIM))
            qkv_ref[idx, :, cs] = a

    ri = lax.broadcasted_iota(I32, (SUB, SUB), 0)
    ci = lax.broadcasted_iota(I32, (SUB, SUB), 1)
    same = (ri >> 6) == (ci >> 6)
    lower_incl = jnp.logical_and(same, ri >= ci)
    lower_strict = jnp.logical_and(same, ri > ci)
    blk16 = (ri >> 4) == (ci >> 4)
    blk32 = (ri >> 5) == (ci >> 5)
    only32 = jnp.logical_and(blk32, jnp.logical_not(blk16))
    only64 = jnp.logical_and(same, jnp.logical_not(blk32))
    eye_f = (ri == ci).astype(F32)
    eye_b = (ri == ci).astype(BF16)
    cum_b = jnp.logical_and(same, ri <= ci).astype(BF16)
    first = lax.broadcasted_iota(I32, (SUB, 1), 0) < C

    def sub_tile(s, carry):
        r0 = pl.multiple_of(s * SUB, SUB)
        g_rows = g_ref[:, pl.ds(r0, SUB)]
        b_rows = b_ref[:, pl.ds(r0, SUB)]
        gp = _exact_parts(g_rows)
        gc_rows = _dot(gp[0], cum_b) + _dot(gp[1], cum_b) + _dot(gp[2], cum_b)
        sp = _exact_parts(jnp.concatenate([gc_rows, b_rows], axis=0))
        cols = _dot(eye_b, sp[0], _NT) + _dot(eye_b, sp[1], _NT) + _dot(eye_b, sp[2], _NT)

        q, k, v, beta, decay, egc, kdec, eg_last = [], [], [], [], [], [], [], []
        for h in heads:
            cs = slice(h * HEAD_DIM, (h + 1) * HEAD_DIM)
            q.append(qkv_ref[0, pl.ds(r0, SUB), cs])
            k.append(qkv_ref[1, pl.ds(r0, SUB), cs])
            v.append(qkv_ref[2, pl.ds(r0, SUB), cs])
            gc_col = cols[:, h:h + 1]
            beta.append(cols[:, n_heads + h:n_heads + h + 1])
            decay.append(jnp.where(lower_incl, jnp.exp(jnp.minimum(gc_col - gc_rows[h:h + 1, :], 0.0)), 0.0))
            egc.append(jnp.exp(gc_col))
            g_last = jnp.where(first, gc_col[C - 1:C, :], gc_col[SUB - 1:SUB, :])
            kdec.append(_bf(k[h] * jnp.exp(g_last - gc_col)))
            eg_last.append((jnp.exp(gc_col[C - 1:C, :]), jnp.exp(gc_col[SUB - 1:SUB, :])))

        kb = [k[h] * beta[h] for h in heads]
        k_b = [_bf(k[h]) for h in heads]
        kb_b = [_bf(kb[h]) for h in heads]
        lmat = [jnp.where(lower_strict, _dot(kb_b[h], k_b[h], _NT) * decay[h], 0.0) for h in heads]
        d16 = [jnp.where(blk16, lmat[h], 0.0) for h in heads]
        d16_b = [_bf(d16[h]) for h in heads]
        p2 = [_bf(_dot(d16_b[h], d16_b[h])) for h in heads]
        p4 = [_bf(_dot(p2[h], p2[h])) for h in heads]
        p8 = [_bf(_dot(p4[h], p4[h])) for h in heads]
        x0 = [eye_f - d16[h] for h in heads]
        x0 = [x0[h] + _dot(_bf(x0[h]), p2[h]) for h in heads]
        x0 = [x0[h] + _dot(_bf(x0[h]), p4[h]) for h in heads]
        x0 = [x0[h] + _dot(_bf(x0[h]), p8[h]) for h in heads]
        x0_b = [_bf(x0[h]) for h in heads]
        y1 = [_bf(_dot(x0_b[h], _bf(jnp.where(only32, lmat[h], 0.0)))) for h in heads]
        x1 = [x0[h] - _dot(y1[h], x0_b[h]) for h in heads]
        x1_b = [_bf(x1[h]) for h in heads]
        y2 = [_bf(_dot(x1_b[h], _bf(jnp.where(only64, lmat[h], 0.0)))) for h in heads]
        t_b = [_bf(x1[h] - _dot(y2[h], x1_b[h])) for h in heads]
        u = [_dot(t_b[h], _bf(v[h] * beta[h])) for h in heads]
        w = [_dot(t_b[h], _bf(kb[h] * egc[h])) for h in heads]
        attn = [_bf(jnp.where(lower_incl, _dot(_bf(q[h]), k_b[h], _NT) * decay[h], 0.0)) for h in heads]
        qg = [q[h] * egc[h] for h in heads]

        state = [state_ref[h] for h in heads]
        zeros = jnp.zeros((C, HEAD_DIM), F32)
        for c in range(2):
            rows = slice(c * C, (c + 1) * C)
            wq = [_bf(jnp.concatenate([w[h][rows], qg[h][rows]], axis=0)) for h in heads]
            r = [_dot(wq[h], _bf(state[h])) for h in heads]
            v_new = [u[h][rows] - r[h][0:C] for h in heads]
            v_pad = [_bf(jnp.concatenate([v_new[h], zeros] if c == 0 else [zeros, v_new[h]], axis=0))
                     for h in heads]
            o = [r[h][C:SUB] + _dot(attn[h][rows], v_pad[h]) for h in heads]
            state = [state[h] * eg_last[h][c] + _dot(kdec[h][rows], _bf(v_new[h]), _TN) for h in heads]
            for h in heads:
                cs = slice(h * HEAD_DIM, (h + 1) * HEAD_DIM)
                ms = jnp.mean(o[h] * o[h], axis=-1, keepdims=True)
                zc = z_ref[0, pl.ds(r0 + c * C, C), cs].astype(F32)
                o_ref[0, pl.ds(r0 + c * C, C), cs] = (
                    o[h] * lax.rsqrt(ms + RMS_EPS) * ow_ref[...] * _silu(zc)).astype(o_ref.dtype)
        for h in heads:
            state_ref[h] = state[h]
        return carry

    lax.fori_loop(0, tt // SUB, sub_tile, 0)


def _gdn(proj3, conv_w, g, beta, out_norm_w, *, n_heads, col0, tt):
    B, S, _ = proj3.shape
    W = n_heads * HEAD_DIM
    kern = functools.partial(_gdn_kernel, tt=tt, n_heads=n_heads)
    cb = col0 // W
    nt = S // tt
    xspec = lambda off: pl.BlockSpec((1, tt, W), lambda b, t: (b, t, cb + off))
    gspec = pl.BlockSpec((n_heads, tt), lambda b, t: (0, b * nt + t))
    return pl.pallas_call(
        kern,
        grid=(B, nt),
        in_specs=[xspec(0), xspec(1), xspec(2), xspec(3),
                  pl.BlockSpec((GDN_CONV, 3 * W), lambda b, t: (0, 0)),
                  gspec, gspec, pl.BlockSpec((1, HEAD_DIM), lambda b, t: (0, 0))],
        out_specs=pl.BlockSpec((1, tt, W), lambda b, t: (b, t, 0)),
        out_shape=jax.ShapeDtypeStruct((B, S, W), BF16),
        scratch_shapes=[pltpu.VMEM((3, tt + 8, W), F32), pltpu.VMEM((3, tt, W), F32),
                        pltpu.VMEM((n_heads, HEAD_DIM, HEAD_DIM), F32)],
        compiler_params=_cparams(("arbitrary", "arbitrary")),
        name="gdn",
    )(proj3, proj3, proj3, proj3, conv_w, g, beta, out_norm_w)


def _outproj_kernel(a_ref, b_ref, wa_ref, wb_ref, x_ref, o_ref):
    o_ref[...] = x_ref[...] + _dot(a_ref[...], wa_ref[...]) + _dot(b_ref[...], wb_ref[...])


def _outproj(o_sb, o_gdn, w_o, x, *, tm):
    T, D = x.shape
    wa, wb = o_sb.shape[1], o_gdn.shape[1]
    assert wa == wb and w_o.shape == (wa + wb, D)
    return pl.pallas_call(
        _outproj_kernel,
        grid=(T // tm,),
        in_specs=[
            pl.BlockSpec((tm, wa), lambda i: (i, 0)),
            pl.BlockSpec((tm, wb), lambda i: (i, 0)),
            pl.BlockSpec((wa, D), lambda i: (0, 0)),
            pl.BlockSpec((wb, D), lambda i: (1, 0)),
            pl.BlockSpec((tm, D), lambda i: (i, 0)),
        ],
        out_specs=pl.BlockSpec((tm, D), lambda i: (i, 0)),
        out_shape=jax.ShapeDtypeStruct((T, D), F32),
        compiler_params=_cparams(("arbitrary",)),
        name="outproj",
    )(o_sb, o_gdn, w_o, w_o, x)


def _router_kernel(x_ref, nw_ref, wr_ref, br_ref, h_ref, eid_ref, gate_ref):
    x = x_ref[...]
    tm = x.shape[0]
    ms = jnp.mean(x * x, axis=-1, keepdims=True)
    h = x * lax.rsqrt(ms + RMS_EPS) * nw_ref[...]
    words = _pack_words(h)
    for s, word in enumerate(words):
        h_ref[pl.ds(s, tm, stride=len(words)), :] = word
    logits = _dot3(wr_ref[...], h, _NT) + br_ref[...]
    best = logits[0:1]
    gidx = jnp.zeros((1, tm), I32)
    for g in range(1, N_GROUPS):
        better = logits[g:g + 1] > best
        gidx = jnp.where(better, g, gidx)
        best = jnp.where(better, logits[g:g + 1], best)
    gsum = jnp.zeros((1, tm), F32)
    for g in range(N_GROUPS):
        gsum = gsum + jnp.exp(logits[g:g + 1] - best)
    group_gate = 1.0 / gsum
    E = EXPERTS_PER_GROUP
    in_group = jnp.zeros((E, tm), F32)
    for g in range(N_GROUPS):
        in_group = jnp.where(gidx == g, logits[8 + g * E:8 + (g + 1) * E], in_group)
    sub = lax.broadcasted_iota(I32, (E, tm), 0)
    m1 = jnp.max(in_group, axis=0, keepdims=True)
    i1 = jnp.min(jnp.where(in_group == m1, sub, E), axis=0, keepdims=True)
    rest = jnp.where(sub == i1, -jnp.inf, in_group)
    m2 = jnp.max(rest, axis=0, keepdims=True)
    i2 = jnp.min(jnp.where(rest == m2, sub, E), axis=0, keepdims=True)
    e2 = jnp.exp(m2 - m1)
    inv = group_gate / (1.0 + e2)
    eid_ref[0:1, :] = gidx * E + i1
    eid_ref[1:2, :] = gidx * E + i2
    gate_ref[0:1, :] = inv
    gate_ref[1:2, :] = inv * e2


def _router(x, norm_w, wr_t, br, *, tm):
    T, D = x.shape
    R = D // PACK_LANES
    return pl.pallas_call(
        _router_kernel,
        grid=(T // tm,),
        in_specs=[
            pl.BlockSpec((tm, D), lambda i: (i, 0)),
            pl.BlockSpec((1, D), lambda i: (0, 0)),
            pl.BlockSpec((ROUTER_ROWS, D), lambda i: (0, 0)),
            pl.BlockSpec((ROUTER_ROWS, 1), lambda i: (0, 0)),
        ],
        out_specs=[
            pl.BlockSpec((tm * R, HEAD_DIM), lambda i: (i, 0)),
            pl.BlockSpec((2, tm), lambda i: (0, i)),
            pl.BlockSpec((2, tm), lambda i: (0, i)),
        ],
        out_shape=[
            jax.ShapeDtypeStruct((T * R, HEAD_DIM), U32),
            jax.ShapeDtypeStruct((2, T), I32),
            jax.ShapeDtypeStruct((2, T), F32),
        ],
        compiler_params=_cparams(("arbitrary",)),
        name="router",
    )(x, norm_w, wr_t, br)


def _plan_kernel(eid_ref, dest_ref, blk_ref, cnt_ref, *, tl, n_blk_pad):
    p = pl.program_id(0)
    i = pl.program_id(1)
    NE = N_EXPERTS
    e0 = eid_ref[0:1, :]
    e1 = eid_ref[1:2, :]
    sub = lax.broadcasted_iota(I32, (NE, tl), 0)
    hot0 = sub == e0
    hot1 = sub == e1
    onehot = jnp.logical_or(hot0, hot1).astype(BF16)
    ones = jnp.ones((tl, HEAD_DIM), BF16)

    @pl.when(jnp.logical_and(p == 0, i == 0))
    def _():
        cnt_ref[0] = jnp.zeros((NE, HEAD_DIM), F32)

    @pl.when(p == 0)
    def _():
        cnt_ref[0] += _dot(onehot, ones)

    @pl.when(jnp.logical_and(p == 1, i == 0))
    def _():
        cnt = cnt_ref[0]
        padded = jnp.floor((cnt + (MOE_BLOCK - 1)) * (1.0 / MOE_BLOCK)) * MOE_BLOCK
        er = lax.broadcasted_iota(I32, (NE, NE), 0)
        ec = lax.broadcasted_iota(I32, (NE, NE), 1)
        start = _dot3((ec < er).astype(F32), padded)
        cnt_ref[1] = start
        end_col = (start + padded)[:, 0:1]
        pos = (lax.broadcasted_iota(I32, (NE, n_blk_pad), 1) * MOE_BLOCK).astype(F32)
        n_before = jnp.sum((end_col <= pos).astype(I32), axis=0, keepdims=True)
        blk_ref[0:1, :] = jnp.minimum(n_before, NE - 1)
        total = jnp.max(end_col, axis=0, keepdims=True)
        blk_ref[1:2, :] = jnp.broadcast_to((total * (1.0 / MOE_BLOCK)).astype(I32), (1, n_blk_pad))
        cnt_ref[0] = jnp.zeros((NE, HEAD_DIM), F32)

    @pl.when(p == 1)
    def _():
        r = lax.broadcasted_iota(I32, (tl, tl), 0)
        c = lax.broadcasted_iota(I32, (tl, tl), 1)
        before = (r < c).astype(BF16)
        run = cnt_ref[0][:, 0:1] + cnt_ref[1][:, 0:1]
        slot = _dot(onehot, before) + run
        dest_ref[0:1, :] = jnp.sum(jnp.where(hot0, slot, 0.0), axis=0, keepdims=True).astype(I32)
        dest_ref[1:2, :] = jnp.sum(jnp.where(hot1, slot, 0.0), axis=0, keepdims=True).astype(I32)
        cnt_ref[0] += _dot(onehot, ones)


def _plan(eid, *, tl, n_blk_pad):
    T = eid.shape[1]
    kern = functools.partial(_plan_kernel, tl=tl, n_blk_pad=n_blk_pad)
    return pl.pallas_call(
        kern,
        grid=(2, T // tl),
        in_specs=[pl.BlockSpec((2, tl), lambda p, i: (0, i))],
        out_specs=[
            pl.BlockSpec((2, tl), lambda p, i: (0, i * p)),
            pl.BlockSpec((2, n_blk_pad), lambda p, i: (0, 0)),
        ],
        out_shape=[
            jax.ShapeDtypeStruct((2, T), I32),
            jax.ShapeDtypeStruct((2, n_blk_pad), I32),
        ],
        scratch_shapes=[pltpu.VMEM((2, N_EXPERTS, HEAD_DIM), F32)],
        compiler_params=_cparams(("arbitrary", "arbitrary")),
        name="plan",
    )(eid)


def _scatter_kernel(dest_ref, blk_ref, h_ref, xs_ref, zero_ref, sem, zsem, *, tt, n_blocks, R):
    i = pl.program_id(0)
    blk_rows = MOE_BLOCK * R

    @pl.when(i == 0)
    def _():
        zero_ref[...] = jnp.zeros_like(zero_ref)
        n_valid = blk_ref[1, 0]

        def clear(b, n):
            last = blk_ref[0, b] != blk_ref[0, jnp.minimum(b + 1, n_blocks - 1)]
            do = jnp.logical_or(b >= n_valid - 1, last)

            @pl.when(do)
            def _():
                pltpu.make_async_copy(zero_ref, xs_ref.at[pl.ds(pl.multiple_of(b * blk_rows, blk_rows), blk_rows)],
                                      zsem).start()
            return n + do.astype(I32)
        n_started = lax.fori_loop(0, n_blocks, clear, 0)

        def drain(b, carry):
            pltpu.make_async_copy(zero_ref, xs_ref.at[pl.ds(0, blk_rows)], zsem).wait()
            return carry
        lax.fori_loop(0, n_started, drain, 0)

    def issue(r, carry):
        src = h_ref.at[pl.ds(pl.multiple_of(r * R, R), R)]
        for kk in range(2):
            dst = xs_ref.at[pl.ds(pl.multiple_of(dest_ref[0, 0, 2 * r + kk] * R, R), R)]
            pltpu.make_async_copy(src, dst, sem).start(priority=kk)
        return carry
    lax.fori_loop(0, tt, issue, 0, unroll=4)

    for kk in range(2):
        pltpu.make_async_copy(h_ref, xs_ref.at[pl.ds(0, tt * R)], sem).wait()


def _scatter_rows(dest_flat, blk, h_packed, *, cap, tt, R):
    T = h_packed.shape[0] // R
    n_blocks = cap // MOE_BLOCK
    kern = functools.partial(_scatter_kernel, tt=tt, n_blocks=n_blocks, R=R)
    return pl.pallas_call(
        kern,
        grid=(T // tt,),
        in_specs=[
            pl.BlockSpec((1, 1, 2 * tt), lambda i: (i, 0, 0), memory_space=pltpu.SMEM),
            pl.BlockSpec(memory_space=pltpu.SMEM),
            pl.BlockSpec((tt * R, HEAD_DIM), lambda i: (i, 0)),
        ],
        out_specs=pl.BlockSpec(memory_space=pl.ANY),
        out_shape=jax.ShapeDtypeStruct((cap * R, HEAD_DIM), U32),
        scratch_shapes=[pltpu.VMEM((MOE_BLOCK * R, HEAD_DIM), U32), pltpu.SemaphoreType.DMA,
                        pltpu.SemaphoreType.DMA],
        compiler_params=_cparams(("arbitrary",)),
        name="scatter_rows",
    )(dest_flat, blk, h_packed)


def _expert_kernel(blk_ref, x_ref, w1_ref, w3_ref, w2_ref, y_ref, w1b_ref, w3b_ref, w2b_ref):
    b = pl.program_id(0)
    valid = b < blk_ref[1, 0]
    new_expert = jnp.logical_or(b == 0, blk_ref[0, b] != blk_ref[0, jnp.maximum(b - 1, 0)])

    @pl.when(jnp.logical_and(valid, new_expert))
    def _():
        w1b_ref[...] = w1_ref[0, 0].astype(BF16)
        w3b_ref[...] = w3_ref[0, 0].astype(BF16)
        w2b_ref[...] = w2_ref[0, 0].astype(BF16)

    @pl.when(valid)
    def _():
        R = x_ref.shape[0] // MOE_BLOCK
        lo, hi = _unpack_words([x_ref[pl.ds(s, MOE_BLOCK, stride=R), :] for s in range(R)])
        x = jnp.concatenate(lo + hi, axis=1).astype(BF16)
        hid = _silu(_dot(x, w1b_ref[...])) * _dot(x, w3b_ref[...])
        y = _dot(hid.astype(BF16), w2b_ref[...])
        for s, word in enumerate(_pack_words(y)):
            y_ref[pl.ds(s, MOE_BLOCK, stride=R), :] = word

    @pl.when(b >= blk_ref[1, 0])
    def _():
        y_ref[...] = jnp.zeros_like(y_ref)


def _experts(blk, x_slots, w1, w3, w2, *, layer, R):
    cap = x_slots.shape[0] // R
    n_blocks = cap // MOE_BLOCK
    D, F = w1.shape[2], w1.shape[3]

    def row_blk(b, blk_ref):
        return jnp.minimum(b, blk_ref[1, 0] - 1)

    def w_idx(b, blk_ref):
        return (layer, blk_ref[0, row_blk(b, blk_ref)], 0, 0)

    grid_spec = pltpu.PrefetchScalarGridSpec(
        num_scalar_prefetch=1,
        grid=(n_blocks,),
        in_specs=[
            pl.BlockSpec((MOE_BLOCK * R, HEAD_DIM), lambda b, blk_ref: (row_blk(b, blk_ref), 0)),
            pl.BlockSpec((1, 1, D, F), w_idx),
            pl.BlockSpec((1, 1, D, F), w_idx),
            pl.BlockSpec((1, 1, F, D), w_idx),
        ],
        out_specs=pl.BlockSpec((MOE_BLOCK * R, HEAD_DIM), lambda b, blk_ref: (b, 0)),
        scratch_shapes=[pltpu.VMEM((D, F), BF16), pltpu.VMEM((D, F), BF16), pltpu.VMEM((F, D), BF16)],
    )
    return pl.pallas_call(
        _expert_kernel,
        grid_spec=grid_spec,
        out_shape=jax.ShapeDtypeStruct((cap * R, HEAD_DIM), U32),
        compiler_params=_cparams(("arbitrary",)),
        name="experts",
    )(blk, x_slots, w1, w3, w2)


def _combine_kernel(dest_ref, dnext_ref, ys_ref, x_ref, gate_ref, o_ref, buf_ref, sems, *, tt, R):
    i = pl.program_id(0)
    slot = lax.rem(i, 2)

    def issue(d_ref, s):
        def one(r, carry):
            for kk in range(2):
                src = ys_ref.at[pl.ds(pl.multiple_of(d_ref[0, 0, 2 * r + kk] * R, R), R)]
                pltpu.make_async_copy(src, buf_ref.at[s, kk, pl.ds(pl.multiple_of(r * R, R), R)],
                                      sems.at[s]).start(priority=kk)
            return carry
        lax.fori_loop(0, tt, one, 0, unroll=4)

    @pl.when(i == 0)
    def _():
        issue(dest_ref, 0)

    @pl.when(i + 1 < pl.num_programs(0))
    def _():
        issue(dnext_ref, 1 - slot)

    for kk in range(2):
        pltpu.make_async_copy(ys_ref.at[pl.ds(0, tt * R)], buf_ref.at[slot, kk], sems.at[slot]).wait()
    g = gate_ref[...]
    half = R * HEAD_DIM
    for s in range(R):
        lo0, hi0 = _unpack_words([buf_ref[slot, 0, pl.ds(s, tt, stride=R), :]])
        lo1, hi1 = _unpack_words([buf_ref[slot, 1, pl.ds(s, tt, stride=R), :]])
        for off, y0, y1 in ((s * HEAD_DIM, lo0[0], lo1[0]), (half + s * HEAD_DIM, hi0[0], hi1[0])):
            cs = slice(off, off + HEAD_DIM)
            o_ref[:, cs] = x_ref[:, cs] + (y0 * g[:, 0:1] + y1 * g[:, 1:2])


def _combine(dest_flat, y_slots, x, gate_t, *, tt, R):
    T, D = x.shape
    n = T // tt
    kern = functools.partial(_combine_kernel, tt=tt, R=R)
    return pl.pallas_call(
        kern,
        grid=(n,),
        in_specs=[
            pl.BlockSpec((1, 1, 2 * tt), lambda i: (i, 0, 0), memory_space=pltpu.SMEM),
            pl.BlockSpec((1, 1, 2 * tt), lambda i: (jnp.minimum(i + 1, n - 1), 0, 0), memory_space=pltpu.SMEM),
            pl.BlockSpec(memory_space=pl.ANY),
            pl.BlockSpec((tt, D), lambda i: (i, 0)),
            pl.BlockSpec((tt, 2), lambda i: (i, 0)),
        ],
        out_specs=pl.BlockSpec((tt, D), lambda i: (i, 0)),
        out_shape=jax.ShapeDtypeStruct((T, D), F32),
        scratch_shapes=[pltpu.VMEM((2, 2, tt * R, HEAD_DIM), U32), pltpu.SemaphoreType.DMA((2,))],
        compiler_params=_cparams(("arbitrary",)),
        name="combine",
    )(dest_flat, dest_flat, y_slots, x, gate_t)


def _mixer(x2, batch, layer, norm_w, w_in, q_norm_w, k_norm_w, conv_w, a_log, dt_bias, out_norm_w, w_o,
           *, tm, tn, tq, tk, hp, tt, tm_out):
    T, D = x2.shape
    S = T // batch
    n_gdn = a_log.shape[0]
    gdn_width = n_gdn * HEAD_DIM
    n_main = w_in.shape[2] - 2 * n_gdn
    sb_width = (n_main - 4 * gdn_width) // 3
    n_sb = sb_width // HEAD_DIM
    wab_t = w_in[layer, :, n_main:].T.astype(BF16)
    reps = tn // HEAD_DIM
    qk_w = jnp.stack([jnp.tile(q_norm_w, reps) * (1.0 / math.sqrt(HEAD_DIM)), jnp.tile(k_norm_w, reps)])
    proj, g, beta = _inproj(x2, norm_w.reshape(1, D), w_in, wab_t, qk_w,
                            a_log.reshape(-1, 1), dt_bias.reshape(-1, 1), layer=layer, sb_width=sb_width,
                            n_main=n_main, tm=tm, tn=tn)
    proj3 = proj.reshape(batch, S, n_main)
    o_sb = _sb_attention(proj3, n_heads=n_sb, tq=tq, tk=tk, hp=hp)
    o_gdn = _gdn(proj3, conv_w, g, beta, out_norm_w.reshape(1, HEAD_DIM), n_heads=n_gdn,
                 col0=3 * sb_width, tt=tt)
    return _outproj(o_sb.reshape(T, sb_width), o_gdn.reshape(T, gdn_width), w_o.astype(BF16), x2, tm=tm_out)


def _moe(x2, layer, norm_w, w_group, b_group, w_expert, b_expert, w1, w3, w2, *, tm_r, tl, tt):
    T, D = x2.shape
    R = D // PACK_LANES
    tail = ROUTER_ROWS - 8 - N_EXPERTS
    wr_t = jnp.concatenate([w_group, jnp.zeros((D, 8 - N_GROUPS), F32), w_expert,
                            jnp.zeros((D, tail), F32)], axis=1).T
    br = jnp.concatenate([b_group, jnp.zeros((8 - N_GROUPS,), F32), b_expert,
                          jnp.zeros((tail,), F32)]).reshape(-1, 1)
    h, eid, gate = _router(x2, norm_w.reshape(1, D), wr_t, br, tm=tm_r)
    n_blocks = (2 * T + MOE_BLOCK - 1) // MOE_BLOCK + N_EXPERTS
    n_blk_pad = ((n_blocks + 127) // 128) * 128
    dest, blk = _plan(eid, tl=tl, n_blk_pad=n_blk_pad)
    dest_flat = dest.T.reshape(T // tt, 1, 2 * tt)
    x_slots = _scatter_rows(dest_flat, blk, h, cap=n_blocks * MOE_BLOCK, tt=tt, R=R)
    y_slots = _experts(blk, x_slots, w1, w3, w2, layer=layer, R=R)
    return _combine(dest_flat, y_slots, x2, gate.T, tt=tt, R=R)


def _forward(x, norm1_w, w_in, sb_q_norm_w, sb_k_norm_w, gdn_conv_w, gdn_a_log, gdn_dt_bias,
             gdn_out_norm_w, w_o, norm2_w, w_group, b_group, w_expert, b_expert, w1, w3, w2, *, tiles):
    batch, S, D = x.shape
    x2 = x.reshape(batch * S, D)
    for l in range(norm1_w.shape[0]):
        x2 = _mixer(x2, batch, l, norm1_w[l], w_in, sb_q_norm_w[l], sb_k_norm_w[l], gdn_conv_w[l],
                    gdn_a_log[l], gdn_dt_bias[l], gdn_out_norm_w[l], w_o[l], **tiles["mixer"])
        x2 = _moe(x2, l, norm2_w[l], w_group[l], b_group[l], w_expert[l], b_expert[l],
                  w1, w3, w2, **tiles["moe"])
    return x2.reshape(batch, S, D)


_TILES = {
    "mixer": dict(tm=1024, tn=512, tq=256, tk=256, hp=2, tt=512, tm_out=512),
    "moe": dict(tm_r=512, tl=512, tt=256),
}


def kernel(x, norm1_w, w_in, sb_q_norm_w, sb_k_norm_w, gdn_conv_w, gdn_a_log, gdn_dt_bias, gdn_out_norm_w, w_o, norm2_w, w_group, b_group, w_expert, b_expert, w1, w3, w2):
    return _forward(x, norm1_w, w_in, sb_q_norm_w, sb_k_norm_w, gdn_conv_w, gdn_a_log, gdn_dt_bias,
                    gdn_out_norm_w, w_o, norm2_w, w_group, b_group, w_expert, b_expert, w1, w3, w2,
                    tiles=_TILES)
```

```python
import functools
import math

import jax
import jax.numpy as jnp
from jax import lax
from jax.experimental import pallas as pl
from jax.experimental.pallas import tpu as pltpu

F32 = jnp.float32
BF16 = jnp.bfloat16
I32 = jnp.int32

HEAD_DIM = 128
GDN_CONV = 4
GDN_CHUNK = 64
N_GROUPS = 4
EXPERTS_PER_GROUP = 8
N_EXPERTS = N_GROUPS * EXPERTS_PER_GROUP
MOE_BLOCK = 256
RMS_EPS = 1e-6
EXP_UNDERFLOW = -104.0
ROUTER_ROWS = 48
VMEM_LIMIT = 56 * 1024 * 1024


def _cparams(sem, vmem=VMEM_LIMIT):
    return pltpu.CompilerParams(dimension_semantics=sem, vmem_limit_bytes=vmem)


def _dot(a, b, dims=(((1,), (0,)), ((), ()))):
    return lax.dot_general(a, b, dims, preferred_element_type=F32)


_NT = (((1,), (1,)), ((), ()))
_TN = (((0,), (0,)), ((), ()))


def _split(a):
    hi = a.astype(BF16)
    lo = (a - hi.astype(F32)).astype(BF16)
    return hi, lo


def _dot3(a, b, dims=(((1,), (0,)), ((), ()))):
    ah, al = _split(a)
    bh, bl = _split(b)
    return _dot(ah, bh, dims) + (_dot(ah, bl, dims) + _dot(al, bh, dims))


U32 = jnp.uint32
PACK_LANES = 2 * HEAD_DIM


def _pack_words(a):
    half = a.shape[1] // 2
    bits = lax.bitcast_convert_type(a.astype(BF16).astype(F32), U32)
    return [(bits[:, half + s * HEAD_DIM:half + (s + 1) * HEAD_DIM] & jnp.uint32(0xFFFF0000))
            | (bits[:, s * HEAD_DIM:(s + 1) * HEAD_DIM] >> 16) for s in range(half // HEAD_DIM)]


def _unpack_words(words):
    lo = [lax.bitcast_convert_type(w << 16, F32) for w in words]
    hi = [lax.bitcast_convert_type(w & jnp.uint32(0xFFFF0000), F32) for w in words]
    return lo, hi


def _softplus(x):
    return jnp.maximum(x, 0.0) + jnp.log(1.0 + jnp.exp(-jnp.abs(x)))


def _sigmoid(x):
    return 1.0 / (1.0 + jnp.exp(-x))


def _inproj_kernel(x_ref, nw_ref, w_ref, wab_ref, qkw_ref, alog_ref, dt_ref,
                   proj_ref, g_ref, beta_ref, h_ref, *, n_qk_tiles, n_heads, rows):
    j = pl.program_id(1)
    tm = x_ref.shape[0]

    @pl.when(j == 0)
    def _():
        def norm_rows(r, carry):
            rs = pl.multiple_of(r * rows, rows)
            x = x_ref[pl.ds(rs, rows), :]
            ms = jnp.mean(x * x, axis=-1, keepdims=True)
            h = x * lax.rsqrt(ms + RMS_EPS) * nw_ref[...]
            h_ref[pl.ds(rs, rows), :] = h.astype(BF16)
            return carry
        lax.fori_loop(0, tm // rows, norm_rows, 0)
        ab = _dot(wab_ref[...], h_ref[...], _NT)
        ga = ab[0:n_heads]
        gb = ab[n_heads:2 * n_heads]
        g_ref[...] = -jnp.exp(alog_ref[...]) * _softplus(ga + dt_ref[...])
        beta_ref[...] = _sigmoid(gb)

    acc = _dot(h_ref[...], w_ref[0].astype(BF16))

    is_qk = j < n_qk_tiles
    wsel = jnp.where(j < n_qk_tiles // 2, qkw_ref[0:1, :], qkw_ref[1:2, :])
    for hh in range(acc.shape[1] // HEAD_DIM):
        sl = slice(hh * HEAD_DIM, (hh + 1) * HEAD_DIM)
        a = acc[:, sl]
        ms = jnp.mean(a * a, axis=-1, keepdims=True)
        normed = a * lax.rsqrt(ms + RMS_EPS) * wsel[:, sl]
        proj_ref[:, sl] = jnp.where(is_qk, normed, a).astype(BF16)


def _inproj(x, norm_w, w_main, wab_t, qk_w, a_log, dt_bias, *, layer, sb_width, n_main, tm, tn):
    T, D = x.shape
    N = n_main
    n_heads = a_log.shape[0]
    n_qk_tiles = 2 * sb_width // tn
    kern = functools.partial(_inproj_kernel, n_qk_tiles=n_qk_tiles, n_heads=n_heads,
                             rows=min(tm, 256))
    return pl.pallas_call(
        kern,
        grid=(T // tm, N // tn),
        in_specs=[
            pl.BlockSpec((tm, D), lambda i, j: (i, 0)),
            pl.BlockSpec((1, D), lambda i, j: (0, 0)),
            pl.BlockSpec((1, D, tn), lambda i, j: (layer, 0, j)),
            pl.BlockSpec((2 * n_heads, D), lambda i, j: (0, 0)),
            pl.BlockSpec((2, tn), lambda i, j: (0, 0)),
            pl.BlockSpec((n_heads, 1), lambda i, j: (0, 0)),
            pl.BlockSpec((n_heads, 1), lambda i, j: (0, 0)),
        ],
        out_specs=[
            pl.BlockSpec((tm, tn), lambda i, j: (i, j)),
            pl.BlockSpec((n_heads, tm), lambda i, j: (0, i)),
            pl.BlockSpec((n_heads, tm), lambda i, j: (0, i)),
        ],
        out_shape=[
            jax.ShapeDtypeStruct((T, N), BF16),
            jax.ShapeDtypeStruct((n_heads, T), F32),
            jax.ShapeDtypeStruct((n_heads, T), F32),
        ],
        scratch_shapes=[pltpu.VMEM((tm, D), BF16)],
        compiler_params=_cparams(("arbitrary", "arbitrary")),
        name="inproj",
    )(x, norm_w, w_main, wab_t, qk_w, a_log, dt_bias)


def _sb_kernel(q_ref, k_ref, v_ref, o_ref, *, tq, tk, hp):
    i = pl.program_id(2)
    heads = range(hp)
    hs = [slice(h * HEAD_DIM, (h + 1) * HEAD_DIM) for h in heads]
    q = [q_ref[0, :, hs[h]] for h in heads]
    row = lax.broadcasted_iota(I32, (tk, tk), 0)
    col = lax.broadcasted_iota(I32, (tk, tk), 1)
    upper_incl = (row >= col).astype(BF16)
    rel = lax.broadcasted_iota(I32, (tq, tk), 1) - lax.broadcasted_iota(I32, (tq, tk), 0)

    def block(ks, carry, masked):
        acc, c = carry
        z = [_dot(q[h], k_ref[0, pl.ds(ks, tk), hs[h]], _NT) for h in heads]
        log_1mb = [-_softplus(z[h]) for h in heads]
        if masked:
            causal = rel + (ks - i * tq) < 0
            log_1mb = [jnp.where(causal, log_1mb[h], 0.0) for h in heads]
        parts = [_split(log_1mb[h]) for h in heads]
        incl = [_dot(parts[h][0], upper_incl) + _dot(parts[h][1], upper_incl) for h in heads]
        w = [jnp.exp(z[h] + incl[h] + c[h]) for h in heads]
        if masked:
            w = [jnp.where(causal, w[h], 0.0) for h in heads]
        acc = [acc[h] + _dot(_bf(w[h]), v_ref[0, pl.ds(ks, tk), hs[h]]) for h in heads]
        c = [c[h] + incl[h][:, 0:1] for h in heads]
        return acc, c

    carry = ([jnp.zeros((tq, HEAD_DIM), F32) for _ in heads], [jnp.zeros((tq, 1), F32) for _ in heads])
    n_diag = tq // tk
    for d in range(n_diag - 1, -1, -1):
        carry = block(pl.multiple_of(i * tq + d * tk, tk), carry, True)

    def c_max(c):
        m = c[0]
        for h in heads[1:]:
            m = jnp.maximum(m, c[h])
        return jnp.max(m)

    def cond(state):
        jj, live, _ = state
        return jnp.logical_and(jj < i * n_diag, live)

    def body(state):
        jj, _, carry = state
        carry = block(pl.multiple_of((i * n_diag - 1 - jj) * tk, tk), carry, False)
        return jj + 1, c_max(carry[1]) > EXP_UNDERFLOW, carry

    _, _, (acc, _) = lax.while_loop(cond, body, (jnp.int32(0), c_max(carry[1]) > EXP_UNDERFLOW, carry))
    for h in heads:
        o_ref[0, :, hs[h]] = acc[h].astype(o_ref.dtype)


def _sb_attention(proj3, *, n_heads, tq, tk, hp):
    B, S, _ = proj3.shape
    kern = functools.partial(_sb_kernel, tq=tq, tk=tk, hp=hp)
    ng = n_heads // hp
    wd = hp * HEAD_DIM
    return pl.pallas_call(
        kern,
        grid=(B, ng, S // tq),
        in_specs=[
            pl.BlockSpec((1, tq, wd), lambda b, h, i: (b, i, h)),
            pl.BlockSpec((1, S, wd), lambda b, h, i: (b, 0, ng + h)),
            pl.BlockSpec((1, S, wd), lambda b, h, i: (b, 0, 2 * ng + h)),
        ],
        out_specs=pl.BlockSpec((1, tq, wd), lambda b, h, i: (b, i, h)),
        out_shape=jax.ShapeDtypeStruct((B, S, n_heads * HEAD_DIM), BF16),
        compiler_params=_cparams(("arbitrary", "arbitrary", "arbitrary")),
        name="sb_attention",
    )(proj3, proj3, proj3)


def _silu(x):
    return x * _sigmoid(x)


SUB = 2 * GDN_CHUNK


def _bf(a):
    return a.astype(BF16)


def _exact_parts(a):
    p1 = a.astype(BF16)
    r1 = a - p1.astype(F32)
    p2 = r1.astype(BF16)
    p3 = (r1 - p2.astype(F32)).astype(BF16)
    return p1, p2, p3


def _gdn_kernel(xq_ref, xk_ref, xv_ref, z_ref, cw_ref, g_ref, b_ref, ow_ref,
                o_ref, xs_ref, qkv_ref, state_ref, *, tt, n_heads):
    t = pl.program_id(1)
    C = GDN_CHUNK
    HALO = 8
    W = n_heads * HEAD_DIM
    heads = range(n_heads)

    @pl.when(t == 0)
    def _():
        xs_ref[:, 0:HALO, :] = jnp.zeros((3, HALO, W), F32)
        state_ref[...] = jnp.zeros_like(state_ref)

    @pl.when(t > 0)
    def _():
        xs_ref[:, 0:HALO, :] = xs_ref[:, tt:tt + HALO, :]

    for idx, x_ref in enumerate((xq_ref, xk_ref, xv_ref)):
        for h in heads:
            cs = slice(h * HEAD_DIM, (h + 1) * HEAD_DIM)
            xs_ref[idx, HALO:, cs] = x_ref[0, :, cs].astype(F32)
            acc = None
            for kk in range(GDN_CONV):
                wrow = cw_ref[kk:kk + 1, idx * W + h * HEAD_DIM: idx * W + (h + 1) * HEAD_DIM]
                term = xs_ref[idx, pl.ds(HALO - GDN_CONV + 1 + kk, tt), cs] * wrow
                acc = term if acc is None else acc + term
            a = _silu(acc)
            if idx < 2:
                inv = lax.rsqrt(jnp.sum(a * a, axis=-1, keepdims=True) + RMS_EPS)
                a = a * (inv * (1.0 / math.sqrt(HEAD_DIM)) if idx == 0 else inv)
            qkv_ref[idx, :, cs] = a

    ri = lax.broadcasted_iota(I32, (SUB, SUB), 0)
    ci = lax.broadcasted_iota(I32, (SUB, SUB), 1)
    same = (ri >> 6) == (ci >> 6)
    lower_incl = jnp.logical_and(same, ri >= ci)
    lower_strict = jnp.logical_and(same, ri > ci)
    blk16 = (ri >> 4) == (ci >> 4)
    blk32 = (ri >> 5) == (ci >> 5)
    only32 = jnp.logical_and(blk32, jnp.logical_not(blk16))
    only64 = jnp.logical_and(same, jnp.logical_not(blk32))
    eye_f = (ri == ci).astype(F32)
    eye_b = (ri == ci).astype(BF16)
    cum_b = jnp.logical_and(same, ri <= ci).astype(BF16)
    first = lax.broadcasted_iota(I32, (SUB, 1), 0) < C

    def sub_tile(s, carry):
        r0 = pl.multiple_of(s * SUB, SUB)
        g_rows = g_ref[:, pl.ds(r0, SUB)]
        b_rows = b_ref[:, pl.ds(r0, SUB)]
        gp = _exact_parts(g_rows)
        gc_rows = _dot(gp[0], cum_b) + _dot(gp[1], cum_b) + _dot(gp[2], cum_b)
        sp = _exact_parts(jnp.concatenate([gc_rows, b_rows], axis=0))
        cols = _dot(eye_b, sp[0], _NT) + _dot(eye_b, sp[1], _NT) + _dot(eye_b, sp[2], _NT)

        q, k, v, beta, decay, egc, kdec, eg_last = [], [], [], [], [], [], [], []
        for h in heads:
            cs = slice(h * HEAD_DIM, (h + 1) * HEAD_DIM)
            q.append(qkv_ref[0, pl.ds(r0, SUB), cs])
            k.append(qkv_ref[1, pl.ds(r0, SUB), cs])
            v.append(qkv_ref[2, pl.ds(r0, SUB), cs])
            gc_col = cols[:, h:h + 1]
            beta.append(cols[:, n_heads + h:n_heads + h + 1])
            decay.append(jnp.where(lower_incl, jnp.exp(jnp.minimum(gc_col - gc_rows[h:h + 1, :], 0.0)), 0.0))
            egc.append(jnp.exp(gc_col))
            g_last = jnp.where(first, gc_col[C - 1:C, :], gc_col[SUB - 1:SUB, :])
            kdec.append(_bf(k[h] * jnp.exp(g_last - gc_col)))
            eg_last.append((jnp.exp(gc_col[C - 1:C, :]), jnp.exp(gc_col[SUB - 1:SUB, :])))

        kb = [k[h] * beta[h] for h in heads]
        k_b = [_bf(k[h]) for h in heads]
        kb_b = [_bf(kb[h]) for h in heads]
        lmat = [jnp.where(lower_strict, _dot(kb_b[h], k_b[h], _NT) * decay[h], 0.0) for h in heads]
        d16 = [jnp.where(blk16, lmat[h], 0.0) for h in heads]
        d16_b = [_bf(d16[h]) for h in heads]
        p2 = [_bf(_dot(d16_b[h], d16_b[h])) for h in heads]
        p4 = [_bf(_dot(p2[h], p2[h])) for h in heads]
        p8 = [_bf(_dot(p4[h], p4[h])) for h in heads]
        x0 = [eye_f - d16[h] for h in heads]
        x0 = [x0[h] + _dot(_bf(x0[h]), p2[h]) for h in heads]
        x0 = [x0[h] + _dot(_bf(x0[h]), p4[h]) for h in heads]
        x0 = [x0[h] + _dot(_bf(x0[h]), p8[h]) for h in heads]
        x0_b = [_bf(x0[h]) for h in heads]
        y1 = [_bf(_dot(x0_b[h], _bf(jnp.where(only32, lmat[h], 0.0)))) for h in heads]
        x1 = [x0[h] - _dot(y1[h], x0_b[h]) for h in heads]
        x1_b = [_bf(x1[h]) for h in heads]
        y2 = [_bf(_dot(x1_b[h], _bf(jnp.where(only64, lmat[h], 0.0)))) for h in heads]
        t_b = [_bf(x1[h] - _dot(y2[h], x1_b[h])) for h in heads]
        u = [_dot(t_b[h], _bf(v[h] * beta[h])) for h in heads]
        w = [_dot(t_b[h], _bf(kb[h] * egc[h])) for h in heads]
        attn = [_bf(jnp.where(lower_incl, _dot(_bf(q[h]), k_b[h], _NT) * decay[h], 0.0)) for h in heads]
        qg = [q[h] * egc[h] for h in heads]

        state = [state_ref[h] for h in heads]
        zeros = jnp.zeros((C, HEAD_DIM), F32)
        for c in range(2):
            rows = slice(c * C, (c + 1) * C)
            wq = [_bf(jnp.concatenate([w[h][rows], qg[h][rows]], axis=0)) for h in heads]
            r = [_dot(wq[h], _bf(state[h])) for h in heads]
            v_new = [u[h][rows] - r[h][0:C] for h in heads]
            v_pad = [_bf(jnp.concatenate([v_new[h], zeros] if c == 0 else [zeros, v_new[h]], axis=0))
                     for h in heads]
            o = [r[h][C:SUB] + _dot(attn[h][rows], v_pad[h]) for h in heads]
            state = [state[h] * eg_last[h][c] + _dot(kdec[h][rows], _bf(v_new[h]), _TN) for h in heads]
            for h in heads:
                cs = slice(h * HEAD_DIM, (h + 1) * HEAD_DIM)
                ms = jnp.mean(o[h] * o[h], axis=-1, keepdims=True)
                zc = z_ref[0, pl.ds(r0 + c * C, C), cs].astype(F32)
                o_ref[0, pl.ds(r0 + c * C, C), cs] = (
                    o[h] * lax.rsqrt(ms + RMS_EPS) * ow_ref[...] * _silu(zc)).astype(o_ref.dtype)
        for h in heads:
            state_ref[h] = state[h]
        return carry

    lax.fori_loop(0, tt // SUB, sub_tile, 0)


def _gdn(proj3, conv_w, g, beta, out_norm_w, *, n_heads, col0, tt):
    B, S, _ = proj3.shape
    W = n_heads * HEAD_DIM
    kern = functools.partial(_gdn_kernel, tt=tt, n_heads=n_heads)
    cb = col0 // W
    nt = S // tt
    xspec = lambda off: pl.BlockSpec((1, tt, W), lambda b, t: (b, t, cb + off))
    gspec = pl.BlockSpec((n_heads, tt), lambda b, t: (0, b * nt + t))
    return pl.pallas_call(
        kern,
        grid=(B, nt),
        in_specs=[xspec(0), xspec(1), xspec(2), xspec(3),
                  pl.BlockSpec((GDN_CONV, 3 * W), lambda b, t: (0, 0)),
                  gspec, gspec, pl.BlockSpec((1, HEAD_DIM), lambda b, t: (0, 0))],
        out_specs=pl.BlockSpec((1, tt, W), lambda b, t: (b, t, 0)),
        out_shape=jax.ShapeDtypeStruct((B, S, W), BF16),
        scratch_shapes=[pltpu.VMEM((3, tt + 8, W), F32), pltpu.VMEM((3, tt, W), F32),
                        pltpu.VMEM((n_heads, HEAD_DIM, HEAD_DIM), F32)],
        compiler_params=_cparams(("arbitrary", "arbitrary")),
        name="gdn",
    )(proj3, proj3, proj3, proj3, conv_w, g, beta, out_norm_w)


def _outproj_kernel(a_ref, b_ref, wa_ref, wb_ref, x_ref, o_ref):
    o_ref[...] = x_ref[...] + _dot(a_ref[...], wa_ref[...]) + _dot(b_ref[...], wb_ref[...])


def _outproj(o_sb, o_gdn, w_o, x, *, tm):
    T, D = x.shape
    wa, wb = o_sb.shape[1], o_gdn.shape[1]
    assert wa == wb and w_o.shape == (wa + wb, D)
    return pl.pallas_call(
        _outproj_kernel,
        grid=(T // tm,),
        in_specs=[
            pl.BlockSpec((tm, wa), lambda i: (i, 0)),
            pl.BlockSpec((tm, wb), lambda i: (i, 0)),
            pl.BlockSpec((wa, D), lambda i: (0, 0)),
            pl.BlockSpec((wb, D), lambda i: (1, 0)),
            pl.BlockSpec((tm, D), lambda i: (i, 0)),
        ],
        out_specs=pl.BlockSpec((tm, D), lambda i: (i, 0)),
        out_shape=jax.ShapeDtypeStruct((T, D), F32),
        compiler_params=_cparams(("arbitrary",)),
        name="outproj",
    )(o_sb, o_gdn, w_o, w_o, x)


def _router_kernel(x_ref, nw_ref, wr_ref, br_ref, h_ref, eid_ref, gate_ref):
    x = x_ref[...]
    tm = x.shape[0]
    ms = jnp.mean(x * x, axis=-1, keepdims=True)
    h = x * lax.rsqrt(ms + RMS_EPS) * nw_ref[...]
    words = _pack_words(h)
    for s, word in enumerate(words):
        h_ref[pl.ds(s, tm, stride=len(words)), :] = word
    logits = _dot3(wr_ref[...], h, _NT) + br_ref[...]
    best = logits[0:1]
    gidx = jnp.zeros((1, tm), I32)
    for g in range(1, N_GROUPS):
        better = logits[g:g + 1] > best
        gidx = jnp.where(better, g, gidx)
        best = jnp.where(better, logits[g:g + 1], best)
    gsum = jnp.zeros((1, tm), F32)
    for g in range(N_GROUPS):
        gsum = gsum + jnp.exp(logits[g:g + 1] - best)
    group_gate = 1.0 / gsum
    E = EXPERTS_PER_GROUP
    in_group = jnp.zeros((E, tm), F32)
    for g in range(N_GROUPS):
        in_group = jnp.where(gidx == g, logits[8 + g * E:8 + (g + 1) * E], in_group)
    sub = lax.broadcasted_iota(I32, (E, tm), 0)
    m1 = jnp.max(in_group, axis=0, keepdims=True)
    i1 = jnp.min(jnp.where(in_group == m1, sub, E), axis=0, keepdims=True)
    rest = jnp.where(sub == i1, -jnp.inf, in_group)
    m2 = jnp.max(rest, axis=0, keepdims=True)
    i2 = jnp.min(jnp.where(rest == m2, sub, E), axis=0, keepdims=True)
    e2 = jnp.exp(m2 - m1)
    inv = group_gate / (1.0 + e2)
    eid_ref[0:1, :] = gidx * E + i1
    eid_ref[1:2, :] = gidx * E + i2
    gate_ref[0:1, :] = inv
    gate_ref[1:2, :] = inv * e2


def _router(x, norm_w, wr_t, br, *, tm):
    T, D = x.shape
    R = D // PACK_LANES
    return pl.pallas_call(
        _router_kernel,
        grid=(T // tm,),
        in_specs=[
            pl.BlockSpec((tm, D), lambda i: (i, 0)),
            pl.BlockSpec((1, D), lambda i: (0, 0)),
            pl.BlockSpec((ROUTER_ROWS, D), lambda i: (0, 0)),
            pl.BlockSpec((ROUTER_ROWS, 1), lambda i: (0, 0)),
        ],
        out_specs=[
            pl.BlockSpec((tm * R, HEAD_DIM), lambda i: (i, 0)),
            pl.BlockSpec((2, tm), lambda i: (0, i)),
            pl.BlockSpec((2, tm), lambda i: (0, i)),
        ],
        out_shape=[
            jax.ShapeDtypeStruct((T * R, HEAD_DIM), U32),
            jax.ShapeDtypeStruct((2, T), I32),
            jax.ShapeDtypeStruct((2, T), F32),
        ],
        compiler_params=_cparams(("arbitrary",)),
        name="router",
    )(x, norm_w, wr_t, br)


def _plan_kernel(eid_ref, dest_ref, blk_ref, cnt_ref, *, tl, n_blk_pad):
    p = pl.program_id(0)
    i = pl.program_id(1)
    NE = N_EXPERTS
    e0 = eid_ref[0:1, :]
    e1 = eid_ref[1:2, :]
    sub = lax.broadcasted_iota(I32, (NE, tl), 0)
    hot0 = sub == e0
    hot1 = sub == e1
    onehot = jnp.logical_or(hot0, hot1).astype(BF16)
    ones = jnp.ones((tl, HEAD_DIM), BF16)

    @pl.when(jnp.logical_and(p == 0, i == 0))
    def _():
        cnt_ref[0] = jnp.zeros((NE, HEAD_DIM), F32)

    @pl.when(p == 0)
    def _():
        cnt_ref[0] += _dot(onehot, ones)

    @pl.when(jnp.logical_and(p == 1, i == 0))
    def _():
        cnt = cnt_ref[0]
        padded = jnp.floor((cnt + (MOE_BLOCK - 1)) * (1.0 / MOE_BLOCK)) * MOE_BLOCK
        er = lax.broadcasted_iota(I32, (NE, NE), 0)
        ec = lax.broadcasted_iota(I32, (NE, NE), 1)
        start = _dot3((ec < er).astype(F32), padded)
        cnt_ref[1] = start
        end_col = (start + padded)[:, 0:1]
        pos = (lax.broadcasted_iota(I32, (NE, n_blk_pad), 1) * MOE_BLOCK).astype(F32)
        n_before = jnp.sum((end_col <= pos).astype(I32), axis=0, keepdims=True)
        blk_ref[0:1, :] = jnp.minimum(n_before, NE - 1)
        total = jnp.max(end_col, axis=0, keepdims=True)
        blk_ref[1:2, :] = jnp.broadcast_to((total * (1.0 / MOE_BLOCK)).astype(I32), (1, n_blk_pad))
        cnt_ref[0] = jnp.zeros((NE, HEAD_DIM), F32)

    @pl.when(p == 1)
    def _():
        r = lax.broadcasted_iota(I32, (tl, tl), 0)
        c = lax.broadcasted_iota(I32, (tl, tl), 1)
        before = (r < c).astype(BF16)
        run = cnt_ref[0][:, 0:1] + cnt_ref[1][:, 0:1]
        slot = _dot(onehot, before) + run
        dest_ref[0:1, :] = jnp.sum(jnp.where(hot0, slot, 0.0), axis=0, keepdims=True).astype(I32)
        dest_ref[1:2, :] = jnp.sum(jnp.where(hot1, slot, 0.0), axis=0, keepdims=True).astype(I32)
        cnt_ref[0] += _dot(onehot, ones)


def _plan(eid, *, tl, n_blk_pad):
    T = eid.shape[1]
    kern = functools.partial(_plan_kernel, tl=tl, n_blk_pad=n_blk_pad)
    return pl.pallas_call(
        kern,
        grid=(2, T // tl),
        in_specs=[pl.BlockSpec((2, tl), lambda p, i: (0, i))],
        out_specs=[
            pl.BlockSpec((2, tl), lambda p, i: (0, i * p)),
            pl.BlockSpec((2, n_blk_pad), lambda p, i: (0, 0)),
        ],
        out_shape=[
            jax.ShapeDtypeStruct((2, T), I32),
            jax.ShapeDtypeStruct((2, n_blk_pad), I32),
        ],
        scratch_shapes=[pltpu.VMEM((2, N_EXPERTS, HEAD_DIM), F32)],
        compiler_params=_cparams(("arbitrary", "arbitrary")),
        name="plan",
    )(eid)


def _scatter_kernel(dest_ref, blk_ref, h_ref, xs_ref, zero_ref, sem, zsem, *, tt, n_blocks, R):
    i = pl.program_id(0)
    blk_rows = MOE_BLOCK * R

    @pl.when(i == 0)
    def _():
        zero_ref[...] = jnp.zeros_like(zero_ref)
        n_valid = blk_ref[1, 0]

        def clear(b, n):
            last = blk_ref[0, b] != blk_ref[0, jnp.minimum(b + 1, n_blocks - 1)]
            do = jnp.logical_or(b >= n_valid - 1, last)

            @pl.when(do)
            def _():
                pltpu.make_async_copy(zero_ref, xs_ref.at[pl.ds(pl.multiple_of(b * blk_rows, blk_rows), blk_rows)],
                                      zsem).start()
            return n + do.astype(I32)
        n_started = lax.fori_loop(0, n_blocks, clear, 0)

        def drain(b, carry):
            pltpu.make_async_copy(zero_ref, xs_ref.at[pl.ds(0, blk_rows)], zsem).wait()
            return carry
        lax.fori_loop(0, n_started, drain, 0)

    def issue(r, carry):
        src = h_ref.at[pl.ds(pl.multiple_of(r * R, R), R)]
        for kk in range(2):
            dst = xs_ref.at[pl.ds(pl.multiple_of(dest_ref[0, 0, 2 * r + kk] * R, R), R)]
            pltpu.make_async_copy(src, dst, sem).start(priority=kk)
        return carry
    lax.fori_loop(0, tt, issue, 0, unroll=4)

    for kk in range(2):
        pltpu.make_async_copy(h_ref, xs_ref.at[pl.ds(0, tt * R)], sem).wait()


def _scatter_rows(dest_flat, blk, h_packed, *, cap, tt, R):
    T = h_packed.shape[0] // R
    n_blocks = cap // MOE_BLOCK
    kern = functools.partial(_scatter_kernel, tt=tt, n_blocks=n_blocks, R=R)
    return pl.pallas_call(
        kern,
        grid=(T // tt,),
        in_specs=[
            pl.BlockSpec((1, 1, 2 * tt), lambda i: (i, 0, 0), memory_space=pltpu.SMEM),
            pl.BlockSpec(memory_space=pltpu.SMEM),
            pl.BlockSpec((tt * R, HEAD_DIM), lambda i: (i, 0)),
        ],
        out_specs=pl.BlockSpec(memory_space=pl.ANY),
        out_shape=jax.ShapeDtypeStruct((cap * R, HEAD_DIM), U32),
        scratch_shapes=[pltpu.VMEM((MOE_BLOCK * R, HEAD_DIM), U32), pltpu.SemaphoreType.DMA,
                        pltpu.SemaphoreType.DMA],
        compiler_params=_cparams(("arbitrary",)),
        name="scatter_rows",
    )(dest_flat, blk, h_packed)


def _expert_kernel(blk_ref, x_ref, w1_ref, w3_ref, w2_ref, y_ref, w1b_ref, w3b_ref, w2b_ref):
    b = pl.program_id(0)
    valid = b < blk_ref[1, 0]
    new_expert = jnp.logical_or(b == 0, blk_ref[0, b] != blk_ref[0, jnp.maximum(b - 1, 0)])

    @pl.when(jnp.logical_and(valid, new_expert))
    def _():
        w1b_ref[...] = w1_ref[0, 0].astype(BF16)
        w3b_ref[...] = w3_ref[0, 0].astype(BF16)
        w2b_ref[...] = w2_ref[0, 0].astype(BF16)

    @pl.when(valid)
    def _():
        R = x_ref.shape[0] // MOE_BLOCK
        lo, hi = _unpack_words([x_ref[pl.ds(s, MOE_BLOCK, stride=R), :] for s in range(R)])
        x = jnp.concatenate(lo + hi, axis=1).astype(BF16)
        hid = _silu(_dot(x, w1b_ref[...])) * _dot(x, w3b_ref[...])
        y = _dot(hid.astype(BF16), w2b_ref[...])
        for s, word in enumerate(_pack_words(y)):
            y_ref[pl.ds(s, MOE_BLOCK, stride=R), :] = word

    @pl.when(b >= blk_ref[1, 0])
    def _():
        y_ref[...] = jnp.zeros_like(y_ref)


def _experts(blk, x_slots, w1, w3, w2, *, layer, R):
    cap = x_slots.shape[0] // R
    n_blocks = cap // MOE_BLOCK
    D, F = w1.shape[2], w1.shape[3]

    def row_blk(b, blk_ref):
        return jnp.minimum(b, blk_ref[1, 0] - 1)

    def w_idx(b, blk_ref):
        return (layer, blk_ref[0, row_blk(b, blk_ref)], 0, 0)

    grid_spec = pltpu.PrefetchScalarGridSpec(
        num_scalar_prefetch=1,
        grid=(n_blocks,),
        in_specs=[
            pl.BlockSpec((MOE_BLOCK * R, HEAD_DIM), lambda b, blk_ref: (row_blk(b, blk_ref), 0)),
            pl.BlockSpec((1, 1, D, F), w_idx),
            pl.BlockSpec((1, 1, D, F), w_idx),
            pl.BlockSpec((1, 1, F, D), w_idx),
        ],
        out_specs=pl.BlockSpec((MOE_BLOCK * R, HEAD_DIM), lambda b, blk_ref: (b, 0)),
        scratch_shapes=[pltpu.VMEM((D, F), BF16), pltpu.VMEM((D, F), BF16), pltpu.VMEM((F, D), BF16)],
    )
    return pl.pallas_call(
        _expert_kernel,
        grid_spec=grid_spec,
        out_shape=jax.ShapeDtypeStruct((cap * R, HEAD_DIM), U32),
        compiler_params=_cparams(("arbitrary",)),
        name="experts",
    )(blk, x_slots, w1, w3, w2)


def _combine_kernel(dest_ref, dnext_ref, ys_ref, x_ref, gate_ref, o_ref, buf_ref, sems, *, tt, R):
    i = pl.program_id(0)
    slot = lax.rem(i, 2)

    def issue(d_ref, s):
        def one(r, carry):
            for kk in range(2):
                src = ys_ref.at[pl.ds(pl.multiple_of(d_ref[0, 0, 2 * r + kk] * R, R), R)]
                pltpu.make_async_copy(src, buf_ref.at[s, kk, pl.ds(pl.multiple_of(r * R, R), R)],
                                      sems.at[s]).start(priority=kk)
            return carry
        lax.fori_loop(0, tt, one, 0, unroll=4)

    @pl.when(i == 0)
    def _():
        issue(dest_ref, 0)

    @pl.when(i + 1 < pl.num_programs(0))
    def _():
        issue(dnext_ref, 1 - slot)

    for kk in range(2):
        pltpu.make_async_copy(ys_ref.at[pl.ds(0, tt * R)], buf_ref.at[slot, kk], sems.at[slot]).wait()
    g = gate_ref[...]
    half = R * HEAD_DIM
    for s in range(R):
        lo0, hi0 = _unpack_words([buf_ref[slot, 0, pl.ds(s, tt, stride=R), :]])
        lo1, hi1 = _unpack_words([buf_ref[slot, 1, pl.ds(s, tt, stride=R), :]])
        for off, y0, y1 in ((s * HEAD_DIM, lo0[0], lo1[0]), (half + s * HEAD_DIM, hi0[0], hi1[0])):
            cs = slice(off, off + HEAD_DIM)
            o_ref[:, cs] = x_ref[:, cs] + (y0 * g[:, 0:1] + y1 * g[:, 1:2])


def _combine(dest_flat, y_slots, x, gate_t, *, tt, R):
    T, D = x.shape
    n = T // tt
    kern = functools.partial(_combine_kernel, tt=tt, R=R)
    return pl.pallas_call(
        kern,
        grid=(n,),
        in_specs=[
            pl.BlockSpec((1, 1, 2 * tt), lambda i: (i, 0, 0), memory_space=pltpu.SMEM),
            pl.BlockSpec((1, 1, 2 * tt), lambda i: (jnp.minimum(i + 1, n - 1), 0, 0), memory_space=pltpu.SMEM),
            pl.BlockSpec(memory_space=pl.ANY),
            pl.BlockSpec((tt, D), lambda i: (i, 0)),
            pl.BlockSpec((tt, 2), lambda i: (i, 0)),
        ],
        out_specs=pl.BlockSpec((tt, D), lambda i: (i, 0)),
        out_shape=jax.ShapeDtypeStruct((T, D), F32),
        scratch_shapes=[pltpu.VMEM((2, 2, tt * R, HEAD_DIM), U32), pltpu.SemaphoreType.DMA((2,))],
        compiler_params=_cparams(("arbitrary",)),
        name="combine",
    )(dest_flat, dest_flat, y_slots, x, gate_t)


def _mixer(x2, batch, layer, norm_w, w_in, q_norm_w, k_norm_w, conv_w, a_log, dt_bias, out_norm_w, w_o,
           *, tm, tn, tq, tk, hp, tt, tm_out):
    T, D = x2.shape
    S = T // batch
    n_gdn = a_log.shape[0]
    gdn_width = n_gdn * HEAD_DIM
    n_main = w_in.shape[2] - 2 * n_gdn
    sb_width = (n_main - 4 * gdn_width) // 3
    n_sb = sb_width // HEAD_DIM
    wab_t = w_in[layer, :, n_main:].T.astype(BF16)
    reps = tn // HEAD_DIM
    qk_w = jnp.stack([jnp.tile(q_norm_w, reps) * (1.0 / math.sqrt(HEAD_DIM)), jnp.tile(k_norm_w, reps)])
    proj, g, beta = _inproj(x2, norm_w.reshape(1, D), w_in, wab_t, qk_w,
                            a_log.reshape(-1, 1), dt_bias.reshape(-1, 1), layer=layer, sb_width=sb_width,
                            n_main=n_main, tm=tm, tn=tn)
    proj3 = proj.reshape(batch, S, n_main)
    o_sb = _sb_attention(proj3, n_heads=n_sb, tq=tq, tk=tk, hp=hp)
    o_gdn = _gdn(proj3, conv_w, g, beta, out_norm_w.reshape(1, HEAD_DIM), n_heads=n_gdn,
                 col0=3 * sb_width, tt=tt)
    return _outproj(o_sb.reshape(T, sb_width), o_gdn.reshape(T, gdn_width), w_o.astype(BF16), x2, tm=tm_out)


def _moe(x2, layer, norm_w, w_group, b_group, w_expert, b_expert, w1, w3, w2, *, tm_r, tl, tt):
    T, D = x2.shape
    R = D // PACK_LANES
    tail = ROUTER_ROWS - 8 - N_EXPERTS
    wr_t = jnp.concatenate([w_group, jnp.zeros((D, 8 - N_GROUPS), F32), w_expert,
                            jnp.zeros((D, tail), F32)], axis=1).T
    br = jnp.concatenate([b_group, jnp.zeros((8 - N_GROUPS,), F32), b_expert,
                          jnp.zeros((tail,), F32)]).reshape(-1, 1)
    h, eid, gate = _router(x2, norm_w.reshape(1, D), wr_t, br, tm=tm_r)
    n_blocks = (2 * T + MOE_BLOCK - 1) // MOE_BLOCK + N_EXPERTS
    n_blk_pad = ((n_blocks + 127) // 128) * 128
    dest, blk = _plan(eid, tl=tl, n_blk_pad=n_blk_pad)
    dest_flat = dest.T.reshape(T // tt, 1, 2 * tt)
    x_slots = _scatter_rows(dest_flat, blk, h, cap=n_blocks * MOE_BLOCK, tt=tt, R=R)
    y_slots = _experts(blk, x_slots, w1, w3, w2, layer=layer, R=R)
    return _combine(dest_flat, y_slots, x2, gate.T, tt=tt, R=R)


def _forward(x, norm1_w, w_in, sb_q_norm_w, sb_k_norm_w, gdn_conv_w, gdn_a_log, gdn_dt_bias,
             gdn_out_norm_w, w_o, norm2_w, w_group, b_group, w_expert, b_expert, w1, w3, w2, *, tiles):
    batch, S, D = x.shape
    x2 = x.reshape(batch * S, D)
    for l in range(norm1_w.shape[0]):
        x2 = _mixer(x2, batch, l, norm1_w[l], w_in, sb_q_norm_w[l], sb_k_norm_w[l], gdn_conv_w[l],
                    gdn_a_log[l], gdn_dt_bias[l], gdn_out_norm_w[l], w_o[l], **tiles["mixer"])
        x2 = _moe(x2, l, norm2_w[l], w_group[l], b_group[l], w_expert[l], b_expert[l],
                  w1, w3, w2, **tiles["moe"])
    return x2.reshape(batch, S, D)


_TILES = {
    "mixer": dict(tm=1024, tn=1024, tq=256, tk=256, hp=2, tt=512, tm_out=512),
    "moe": dict(tm_r=512, tl=512, tt=256),
}


def kernel(x, norm1_w, w_in, sb_q_norm_w, sb_k_norm_w, gdn_conv_w, gdn_a_log, gdn_dt_bias, gdn_out_norm_w, w_o, norm2_w, w_group, b_group, w_expert, b_expert, w1, w3, w2):
    return _forward(x, norm1_w, w_in, sb_q_norm_w, sb_k_norm_w, gdn_conv_w, gdn_a_log, gdn_dt_bias,
                    gdn_out_norm_w, w_o, norm2_w, w_group, b_group, w_expert, b_expert, w1, w3, w2,
                    tiles=_TILES)
```

```python
import functools
import math

import jax
import jax.numpy as jnp
from jax import lax
from jax.experimental import pallas as pl
from jax.experimental.pallas import tpu as pltpu

F32 = jnp.float32
BF16 = jnp.bfloat16
I32 = jnp.int32

HEAD_DIM = 128
GDN_CONV = 4
GDN_CHUNK = 64
N_GROUPS = 4
EXPERTS_PER_GROUP = 8
N_EXPERTS = N_GROUPS * EXPERTS_PER_GROUP
MOE_BLOCK = 512
RMS_EPS = 1e-6
EXP_UNDERFLOW = -104.0
ROUTER_ROWS = 48
VMEM_LIMIT = 56 * 1024 * 1024


def _cparams(sem, vmem=VMEM_LIMIT):
    return pltpu.CompilerParams(dimension_semantics=sem, vmem_limit_bytes=vmem)


def _dot(a, b, dims=(((1,), (0,)), ((), ()))):
    return lax.dot_general(a, b, dims, preferred_element_type=F32)


_NT = (((1,), (1,)), ((), ()))
_TN = (((0,), (0,)), ((), ()))


def _split(a):
    hi = a.astype(BF16)
    lo = (a - hi.astype(F32)).astype(BF16)
    return hi, lo


def _dot3(a, b, dims=(((1,), (0,)), ((), ()))):
    ah, al = _split(a)
    bh, bl = _split(b)
    return _dot(ah, bh, dims) + (_dot(ah, bl, dims) + _dot(al, bh, dims))


U32 = jnp.uint32
PACK_LANES = 2 * HEAD_DIM


def _pack_words(a):
    half = a.shape[1] // 2
    bits = lax.bitcast_convert_type(a.astype(BF16).astype(F32), U32)
    return [(bits[:, half + s * HEAD_DIM:half + (s + 1) * HEAD_DIM] & jnp.uint32(0xFFFF0000))
            | (bits[:, s * HEAD_DIM:(s + 1) * HEAD_DIM] >> 16) for s in range(half // HEAD_DIM)]


def _unpack_words(words):
    lo = [lax.bitcast_convert_type(w << 16, F32) for w in words]
    hi = [lax.bitcast_convert_type(w & jnp.uint32(0xFFFF0000), F32) for w in words]
    return lo, hi


def _softplus(x):
    return jnp.maximum(x, 0.0) + jnp.log(1.0 + jnp.exp(-jnp.abs(x)))


def _sigmoid(x):
    return 1.0 / (1.0 + jnp.exp(-x))


def _inproj_kernel(x_ref, nw_ref, w_ref, wab_ref, qkw_ref, alog_ref, dt_ref,
                   proj_ref, g_ref, beta_ref, h_ref, *, n_qk_tiles, n_heads, rows):
    j = pl.program_id(1)
    tm = x_ref.shape[0]

    @pl.when(j == 0)
    def _():
        def norm_rows(r, carry):
            rs = pl.multiple_of(r * rows, rows)
            x = x_ref[pl.ds(rs, rows), :]
            ms = jnp.mean(x * x, axis=-1, keepdims=True)
            h = x * lax.rsqrt(ms + RMS_EPS) * nw_ref[...]
            h_ref[pl.ds(rs, rows), :] = h.astype(BF16)
            return carry
        lax.fori_loop(0, tm // rows, norm_rows, 0)
        ab = _dot(wab_ref[0].T.astype(BF16), h_ref[...], _NT)
        ga = ab[0:n_heads]
        gb = ab[n_heads:2 * n_heads]
        g_ref[...] = -jnp.exp(alog_ref[...]) * _softplus(ga + dt_ref[...])
        beta_ref[...] = _sigmoid(gb)

    acc = _dot(h_ref[...], w_ref[0].astype(BF16))

    is_qk = j < n_qk_tiles
    wsel = jnp.where(j < n_qk_tiles // 2, qkw_ref[0:1, :], qkw_ref[1:2, :])
    for hh in range(acc.shape[1] // HEAD_DIM):
        sl = slice(hh * HEAD_DIM, (hh + 1) * HEAD_DIM)
        a = acc[:, sl]
        ms = jnp.mean(a * a, axis=-1, keepdims=True)
        normed = a * lax.rsqrt(ms + RMS_EPS) * wsel[:, sl]
        proj_ref[:, sl] = jnp.where(is_qk, normed, a).astype(BF16)


def _inproj(x, norm_w, w_main, qk_w, a_log, dt_bias, *, layer, sb_width, n_main, tm, tn):
    T, D = x.shape
    assert n_main % HEAD_DIM == 0 and w_main.shape[2] == n_main + 2 * a_log.shape[0]
    N = n_main
    n_heads = a_log.shape[0]
    n_qk_tiles = 2 * sb_width // tn
    kern = functools.partial(_inproj_kernel, n_qk_tiles=n_qk_tiles, n_heads=n_heads,
                             rows=min(tm, 256))
    return pl.pallas_call(
        kern,
        grid=(T // tm, N // tn),
        in_specs=[
            pl.BlockSpec((tm, D), lambda i, j: (i, 0)),
            pl.BlockSpec((1, D), lambda i, j: (0, 0)),
            pl.BlockSpec((1, D, tn), lambda i, j: (layer, 0, j)),
            pl.BlockSpec((1, D, HEAD_DIM), lambda i, j: (layer, 0, n_main // HEAD_DIM)),
            pl.BlockSpec((2, tn), lambda i, j: (0, 0)),
            pl.BlockSpec((n_heads, 1), lambda i, j: (0, 0)),
            pl.BlockSpec((n_heads, 1), lambda i, j: (0, 0)),
        ],
        out_specs=[
            pl.BlockSpec((tm, tn), lambda i, j: (i, j)),
            pl.BlockSpec((n_heads, tm), lambda i, j: (0, i)),
            pl.BlockSpec((n_heads, tm), lambda i, j: (0, i)),
        ],
        out_shape=[
            jax.ShapeDtypeStruct((T, N), BF16),
            jax.ShapeDtypeStruct((n_heads, T), F32),
            jax.ShapeDtypeStruct((n_heads, T), F32),
        ],
        scratch_shapes=[pltpu.VMEM((tm, D), BF16)],
        compiler_params=_cparams(("arbitrary", "arbitrary")),
        name="inproj",
    )(x, norm_w, w_main, w_main, qk_w, a_log, dt_bias)


def _sb_kernel(q_ref, k_ref, v_ref, o_ref, *, tq, tk, hp):
    i = pl.program_id(2)
    heads = range(hp)
    hs = [slice(h * HEAD_DIM, (h + 1) * HEAD_DIM) for h in heads]
    q = [q_ref[0, :, hs[h]] for h in heads]
    row = lax.broadcasted_iota(I32, (tk, tk), 0)
    col = lax.broadcasted_iota(I32, (tk, tk), 1)
    upper_incl = (row >= col).astype(BF16)
    rel = lax.broadcasted_iota(I32, (tq, tk), 1) - lax.broadcasted_iota(I32, (tq, tk), 0)

    def block(ks, carry, masked):
        acc, c = carry
        z = [_dot(q[h], k_ref[0, pl.ds(ks, tk), hs[h]], _NT) for h in heads]
        log_1mb = [-_softplus(z[h]) for h in heads]
        if masked:
            causal = rel + (ks - i * tq) < 0
            log_1mb = [jnp.where(causal, log_1mb[h], 0.0) for h in heads]
        parts = [_split(log_1mb[h]) for h in heads]
        incl = [_dot(parts[h][0], upper_incl) + _dot(parts[h][1], upper_incl) for h in heads]
        w = [jnp.exp(z[h] + incl[h] + c[h]) for h in heads]
        if masked:
            w = [jnp.where(causal, w[h], 0.0) for h in heads]
        acc = [acc[h] + _dot(_bf(w[h]), v_ref[0, pl.ds(ks, tk), hs[h]]) for h in heads]
        c = [c[h] + incl[h][:, 0:1] for h in heads]
        return acc, c

    carry = ([jnp.zeros((tq, HEAD_DIM), F32) for _ in heads], [jnp.zeros((tq, 1), F32) for _ in heads])
    n_diag = tq // tk
    for d in range(n_diag - 1, -1, -1):
        carry = block(pl.multiple_of(i * tq + d * tk, tk), carry, True)

    def c_max(c):
        m = c[0]
        for h in heads[1:]:
            m = jnp.maximum(m, c[h])
        return jnp.max(m)

    def cond(state):
        jj, live, _ = state
        return jnp.logical_and(jj < i * n_diag, live)

    def body(state):
        jj, _, carry = state
        carry = block(pl.multiple_of((i * n_diag - 1 - jj) * tk, tk), carry, False)
        return jj + 1, c_max(carry[1]) > EXP_UNDERFLOW, carry

    _, _, (acc, _) = lax.while_loop(cond, body, (jnp.int32(0), c_max(carry[1]) > EXP_UNDERFLOW, carry))
    for h in heads:
        o_ref[0, :, hs[h]] = acc[h].astype(o_ref.dtype)


def _sb_attention(proj3, *, n_heads, tq, tk, hp):
    B, S, _ = proj3.shape
    kern = functools.partial(_sb_kernel, tq=tq, tk=tk, hp=hp)
    ng = n_heads // hp
    wd = hp * HEAD_DIM
    return pl.pallas_call(
        kern,
        grid=(B, ng, S // tq),
        in_specs=[
            pl.BlockSpec((1, tq, wd), lambda b, h, i: (b, i, h)),
            pl.BlockSpec((1, S, wd), lambda b, h, i: (b, 0, ng + h)),
            pl.BlockSpec((1, S, wd), lambda b, h, i: (b, 0, 2 * ng + h)),
        ],
        out_specs=pl.BlockSpec((1, tq, wd), lambda b, h, i: (b, i, h)),
        out_shape=jax.ShapeDtypeStruct((B, S, n_heads * HEAD_DIM), BF16),
        compiler_params=_cparams(("arbitrary", "arbitrary", "arbitrary")),
        name="sb_attention",
    )(proj3, proj3, proj3)


def _silu(x):
    return x * _sigmoid(x)


SUB = 2 * GDN_CHUNK


def _bf(a):
    return a.astype(BF16)


def _exact_parts(a):
    p1 = a.astype(BF16)
    r1 = a - p1.astype(F32)
    p2 = r1.astype(BF16)
    p3 = (r1 - p2.astype(F32)).astype(BF16)
    return p1, p2, p3


def _gdn_kernel(xq_ref, xk_ref, xv_ref, z_ref, cw_ref, g_ref, b_ref, ow_ref,
                o_ref, xs_ref, qkv_ref, state_ref, *, tt, n_heads):
    t = pl.program_id(1)
    C = GDN_CHUNK
    HALO = 8
    W = n_heads * HEAD_DIM
    heads = range(n_heads)

    @pl.when(t == 0)
    def _():
        xs_ref[:, 0:HALO, :] = jnp.zeros((3, HALO, W), F32)
        state_ref[...] = jnp.zeros_like(state_ref)

    @pl.when(t > 0)
    def _():
        xs_ref[:, 0:HALO, :] = xs_ref[:, tt:tt + HALO, :]

    for idx, x_ref in enumerate((xq_ref, xk_ref, xv_ref)):
        for h in heads:
            cs = slice(h * HEAD_DIM, (h + 1) * HEAD_DIM)
            xs_ref[idx, HALO:, cs] = x_ref[0, :, cs].astype(F32)
            acc = None
            for kk in range(GDN_CONV):
                wrow = cw_ref[kk:kk + 1, idx * W + h * HEAD_DIM: idx * W + (h + 1) * HEAD_DIM]
                term = xs_ref[idx, pl.ds(HALO - GDN_CONV + 1 + kk, tt), cs] * wrow
                acc = term if acc is None else acc + term
            a = _silu(acc)
            if idx < 2:
                inv = lax.rsqrt(jnp.sum(a * a, axis=-1, keepdims=True) + RMS_EPS)
                a = a * (inv * (1.0 / math.sqrt(HEAD_DIM)) if idx == 0 else inv)
            qkv_ref[idx, :, cs] = a

    ri = lax.broadcasted_iota(I32, (SUB, SUB), 0)
    ci = lax.broadcasted_iota(I32, (SUB, SUB), 1)
    same = (ri >> 6) == (ci >> 6)
    lower_incl = jnp.logical_and(same, ri >= ci)
    lower_strict = jnp.logical_and(same, ri > ci)
    blk16 = (ri >> 4) == (ci >> 4)
    blk32 = (ri >> 5) == (ci >> 5)
    only32 = jnp.logical_and(blk32, jnp.logical_not(blk16))
    only64 = jnp.logical_and(same, jnp.logical_not(blk32))
    eye_f = (ri == ci).astype(F32)
    eye_b = (ri == ci).astype(BF16)
    cum_b = jnp.logical_and(same, ri <= ci).astype(BF16)
    first = lax.broadcasted_iota(I32, (SUB, 1), 0) < C

    def sub_tile(s, carry):
        r0 = pl.multiple_of(s * SUB, SUB)
        g_rows = g_ref[:, pl.ds(r0, SUB)]
        b_rows = b_ref[:, pl.ds(r0, SUB)]
        gp = _exact_parts(g_rows)
        gc_rows = _dot(gp[0], cum_b) + _dot(gp[1], cum_b) + _dot(gp[2], cum_b)
        sp = _exact_parts(jnp.concatenate([gc_rows, b_rows], axis=0))
        cols = _dot(eye_b, sp[0], _NT) + _dot(eye_b, sp[1], _NT) + _dot(eye_b, sp[2], _NT)

        q, k, v, beta, decay, egc, kdec, eg_last = [], [], [], [], [], [], [], []
        for h in heads:
            cs = slice(h * HEAD_DIM, (h + 1) * HEAD_DIM)
            q.append(qkv_ref[0, pl.ds(r0, SUB), cs])
            k.append(qkv_ref[1, pl.ds(r0, SUB), cs])
            v.append(qkv_ref[2, pl.ds(r0, SUB), cs])
            gc_col = cols[:, h:h + 1]
            beta.append(cols[:, n_heads + h:n_heads + h + 1])
            decay.append(jnp.where(lower_incl, jnp.exp(jnp.minimum(gc_col - gc_rows[h:h + 1, :], 0.0)), 0.0))
            egc.append(jnp.exp(gc_col))
            g_last = jnp.where(first, gc_col[C - 1:C, :], gc_col[SUB - 1:SUB, :])
            kdec.append(_bf(k[h] * jnp.exp(g_last - gc_col)))
            eg_last.append((jnp.exp(gc_col[C - 1:C, :]), jnp.exp(gc_col[SUB - 1:SUB, :])))

        kb = [k[h] * beta[h] for h in heads]
        k_b = [_bf(k[h]) for h in heads]
        kb_b = [_bf(kb[h]) for h in heads]
        lmat = [jnp.where(lower_strict, _dot(kb_b[h], k_b[h], _NT) * decay[h], 0.0) for h in heads]
        d16 = [jnp.where(blk16, lmat[h], 0.0) for h in heads]
        d16_b = [_bf(d16[h]) for h in heads]
        p2 = [_bf(_dot(d16_b[h], d16_b[h])) for h in heads]
        p4 = [_bf(_dot(p2[h], p2[h])) for h in heads]
        p8 = [_bf(_dot(p4[h], p4[h])) for h in heads]
        x0 = [eye_f - d16[h] for h in heads]
        x0 = [x0[h] + _dot(_bf(x0[h]), p2[h]) for h in heads]
        x0 = [x0[h] + _dot(_bf(x0[h]), p4[h]) for h in heads]
        x0 = [x0[h] + _dot(_bf(x0[h]), p8[h]) for h in heads]
        x0_b = [_bf(x0[h]) for h in heads]
        y1 = [_bf(_dot(x0_b[h], _bf(jnp.where(only32, lmat[h], 0.0)))) for h in heads]
        x1 = [x0[h] - _dot(y1[h], x0_b[h]) for h in heads]
        x1_b = [_bf(x1[h]) for h in heads]
        y2 = [_bf(_dot(x1_b[h], _bf(jnp.where(only64, lmat[h], 0.0)))) for h in heads]
        t_b = [_bf(x1[h] - _dot(y2[h], x1_b[h])) for h in heads]
        u = [_dot(t_b[h], _bf(v[h] * beta[h])) for h in heads]
        w = [_dot(t_b[h], _bf(kb[h] * egc[h])) for h in heads]
        attn = [_bf(jnp.where(lower_incl, _dot(_bf(q[h]), k_b[h], _NT) * decay[h], 0.0)) for h in heads]
        qg = [q[h] * egc[h] for h in heads]

        state = [state_ref[h] for h in heads]
        zeros = jnp.zeros((C, HEAD_DIM), F32)
        for c in range(2):
            rows = slice(c * C, (c + 1) * C)
            wq = [_bf(jnp.concatenate([w[h][rows], qg[h][rows]], axis=0)) for h in heads]
            r = [_dot(wq[h], _bf(state[h])) for h in heads]
            v_new = [u[h][rows] - r[h][0:C] for h in heads]
            v_pad = [_bf(jnp.concatenate([v_new[h], zeros] if c == 0 else [zeros, v_new[h]], axis=0))
                     for h in heads]
            o = [r[h][C:SUB] + _dot(attn[h][rows], v_pad[h]) for h in heads]
            state = [state[h] * eg_last[h][c] + _dot(kdec[h][rows], _bf(v_new[h]), _TN) for h in heads]
            for h in heads:
                cs = slice(h * HEAD_DIM, (h + 1) * HEAD_DIM)
                ms = jnp.mean(o[h] * o[h], axis=-1, keepdims=True)
                zc = z_ref[0, pl.ds(r0 + c * C, C), cs].astype(F32)
                o_ref[0, pl.ds(r0 + c * C, C), cs] = (
                    o[h] * lax.rsqrt(ms + RMS_EPS) * ow_ref[...] * _silu(zc)).astype(o_ref.dtype)
        for h in heads:
            state_ref[h] = state[h]
        return carry

    lax.fori_loop(0, tt // SUB, sub_tile, 0)


def _gdn(proj3, conv_w, g, beta, out_norm_w, *, n_heads, col0, tt):
    B, S, _ = proj3.shape
    W = n_heads * HEAD_DIM
    kern = functools.partial(_gdn_kernel, tt=tt, n_heads=n_heads)
    cb = col0 // W
    nt = S // tt
    xspec = lambda off: pl.BlockSpec((1, tt, W), lambda b, t: (b, t, cb + off))
    gspec = pl.BlockSpec((n_heads, tt), lambda b, t: (0, b * nt + t))
    return pl.pallas_call(
        kern,
        grid=(B, nt),
        in_specs=[xspec(0), xspec(1), xspec(2), xspec(3),
                  pl.BlockSpec((GDN_CONV, 3 * W), lambda b, t: (0, 0)),
                  gspec, gspec, pl.BlockSpec((1, HEAD_DIM), lambda b, t: (0, 0))],
        out_specs=pl.BlockSpec((1, tt, W), lambda b, t: (b, t, 0)),
        out_shape=jax.ShapeDtypeStruct((B, S, W), BF16),
        scratch_shapes=[pltpu.VMEM((3, tt + 8, W), F32), pltpu.VMEM((3, tt, W), F32),
                        pltpu.VMEM((n_heads, HEAD_DIM, HEAD_DIM), F32)],
        compiler_params=_cparams(("arbitrary", "arbitrary")),
        name="gdn",
    )(proj3, proj3, proj3, proj3, conv_w, g, beta, out_norm_w)


def _outproj_kernel(a_ref, b_ref, wa_ref, wb_ref, x_ref, o_ref):
    o_ref[...] = x_ref[...] + _dot(a_ref[...], wa_ref[...]) + _dot(b_ref[...], wb_ref[...])


def _outproj(o_sb, o_gdn, w_o, x, *, tm):
    T, D = x.shape
    wa, wb = o_sb.shape[1], o_gdn.shape[1]
    assert wa == wb and w_o.shape == (wa + wb, D)
    return pl.pallas_call(
        _outproj_kernel,
        grid=(T // tm,),
        in_specs=[
            pl.BlockSpec((tm, wa), lambda i: (i, 0)),
            pl.BlockSpec((tm, wb), lambda i: (i, 0)),
            pl.BlockSpec((wa, D), lambda i: (0, 0)),
            pl.BlockSpec((wb, D), lambda i: (1, 0)),
            pl.BlockSpec((tm, D), lambda i: (i, 0)),
        ],
        out_specs=pl.BlockSpec((tm, D), lambda i: (i, 0)),
        out_shape=jax.ShapeDtypeStruct((T, D), F32),
        compiler_params=_cparams(("arbitrary",)),
        name="outproj",
    )(o_sb, o_gdn, w_o, w_o, x)


def _router_kernel(x_ref, nw_ref, wr_ref, br_ref, h_ref, eid_ref, gate_ref):
    x = x_ref[...]
    tm = x.shape[0]
    ms = jnp.mean(x * x, axis=-1, keepdims=True)
    h = x * lax.rsqrt(ms + RMS_EPS) * nw_ref[...]
    words = _pack_words(h)
    for s, word in enumerate(words):
        h_ref[pl.ds(s, tm, stride=len(words)), :] = word
    logits = _dot3(wr_ref[...], h, _NT) + br_ref[...]
    best = logits[0:1]
    gidx = jnp.zeros((1, tm), I32)
    for g in range(1, N_GROUPS):
        better = logits[g:g + 1] > best
        gidx = jnp.where(better, g, gidx)
        best = jnp.where(better, logits[g:g + 1], best)
    gsum = jnp.zeros((1, tm), F32)
    for g in range(N_GROUPS):
        gsum = gsum + jnp.exp(logits[g:g + 1] - best)
    group_gate = 1.0 / gsum
    E = EXPERTS_PER_GROUP
    in_group = jnp.zeros((E, tm), F32)
    for g in range(N_GROUPS):
        in_group = jnp.where(gidx == g, logits[8 + g * E:8 + (g + 1) * E], in_group)
    sub = lax.broadcasted_iota(I32, (E, tm), 0)
    m1 = jnp.max(in_group, axis=0, keepdims=True)
    i1 = jnp.min(jnp.where(in_group == m1, sub, E), axis=0, keepdims=True)
    rest = jnp.where(sub == i1, -jnp.inf, in_group)
    m2 = jnp.max(rest, axis=0, keepdims=True)
    i2 = jnp.min(jnp.where(rest == m2, sub, E), axis=0, keepdims=True)
    e2 = jnp.exp(m2 - m1)
    inv = group_gate / (1.0 + e2)
    eid_ref[0:1, :] = gidx * E + i1
    eid_ref[1:2, :] = gidx * E + i2
    gate_ref[0:1, :] = inv
    gate_ref[1:2, :] = inv * e2


def _router(x, norm_w, wr_t, br, *, tm):
    T, D = x.shape
    R = D // PACK_LANES
    return pl.pallas_call(
        _router_kernel,
        grid=(T // tm,),
        in_specs=[
            pl.BlockSpec((tm, D), lambda i: (i, 0)),
            pl.BlockSpec((1, D), lambda i: (0, 0)),
            pl.BlockSpec((ROUTER_ROWS, D), lambda i: (0, 0)),
            pl.BlockSpec((ROUTER_ROWS, 1), lambda i: (0, 0)),
        ],
        out_specs=[
            pl.BlockSpec((tm * R, HEAD_DIM), lambda i: (i, 0)),
            pl.BlockSpec((2, tm), lambda i: (0, i)),
            pl.BlockSpec((2, tm), lambda i: (0, i)),
        ],
        out_shape=[
            jax.ShapeDtypeStruct((T * R, HEAD_DIM), U32),
            jax.ShapeDtypeStruct((2, T), I32),
            jax.ShapeDtypeStruct((2, T), F32),
        ],
        compiler_params=_cparams(("arbitrary",)),
        name="router",
    )(x, norm_w, wr_t, br)


def _plan_kernel(eid_ref, dest_ref, blk_ref, cnt_ref, *, tl, n_blk_pad):
    p = pl.program_id(0)
    i = pl.program_id(1)
    NE = N_EXPERTS
    e0 = eid_ref[0:1, :]
    e1 = eid_ref[1:2, :]
    sub = lax.broadcasted_iota(I32, (NE, tl), 0)
    hot0 = sub == e0
    hot1 = sub == e1
    onehot = jnp.logical_or(hot0, hot1).astype(BF16)
    ones = jnp.ones((tl, HEAD_DIM), BF16)

    @pl.when(jnp.logical_and(p == 0, i == 0))
    def _():
        cnt_ref[0] = jnp.zeros((NE, HEAD_DIM), F32)

    @pl.when(p == 0)
    def _():
        cnt_ref[0] += _dot(onehot, ones)

    @pl.when(jnp.logical_and(p == 1, i == 0))
    def _():
        cnt = cnt_ref[0]
        padded = jnp.floor((cnt + (MOE_BLOCK - 1)) * (1.0 / MOE_BLOCK)) * MOE_BLOCK
        er = lax.broadcasted_iota(I32, (NE, NE), 0)
        ec = lax.broadcasted_iota(I32, (NE, NE), 1)
        start = _dot3((ec < er).astype(F32), padded)
        cnt_ref[1] = start
        end_col = (start + padded)[:, 0:1]
        pos = (lax.broadcasted_iota(I32, (NE, n_blk_pad), 1) * MOE_BLOCK).astype(F32)
        n_before = jnp.sum((end_col <= pos).astype(I32), axis=0, keepdims=True)
        blk_ref[0:1, :] = jnp.minimum(n_before, NE - 1)
        total = jnp.max(end_col, axis=0, keepdims=True)
        blk_ref[1:2, :] = jnp.broadcast_to((total * (1.0 / MOE_BLOCK)).astype(I32), (1, n_blk_pad))
        cnt_ref[0] = jnp.zeros((NE, HEAD_DIM), F32)

    @pl.when(p == 1)
    def _():
        r = lax.broadcasted_iota(I32, (tl, tl), 0)
        c = lax.broadcasted_iota(I32, (tl, tl), 1)
        before = (r < c).astype(BF16)
        run = cnt_ref[0][:, 0:1] + cnt_ref[1][:, 0:1]
        slot = _dot(onehot, before) + run
        dest_ref[0:1, :] = jnp.sum(jnp.where(hot0, slot, 0.0), axis=0, keepdims=True).astype(I32)
        dest_ref[1:2, :] = jnp.sum(jnp.where(hot1, slot, 0.0), axis=0, keepdims=True).astype(I32)
        cnt_ref[0] += _dot(onehot, ones)


def _plan(eid, *, tl, n_blk_pad):
    T = eid.shape[1]
    kern = functools.partial(_plan_kernel, tl=tl, n_blk_pad=n_blk_pad)
    return pl.pallas_call(
        kern,
        grid=(2, T // tl),
        in_specs=[pl.BlockSpec((2, tl), lambda p, i: (0, i))],
        out_specs=[
            pl.BlockSpec((2, tl), lambda p, i: (0, i * p)),
            pl.BlockSpec((2, n_blk_pad), lambda p, i: (0, 0)),
        ],
        out_shape=[
            jax.ShapeDtypeStruct((2, T), I32),
            jax.ShapeDtypeStruct((2, n_blk_pad), I32),
        ],
        scratch_shapes=[pltpu.VMEM((2, N_EXPERTS, HEAD_DIM), F32)],
        compiler_params=_cparams(("arbitrary", "arbitrary")),
        name="plan",
    )(eid)


def _scatter_kernel(dest_ref, blk_ref, h_ref, xs_ref, zero_ref, sem, zsem, *, tt, n_blocks, R):
    i = pl.program_id(0)
    blk_rows = MOE_BLOCK * R

    @pl.when(i == 0)
    def _():
        zero_ref[...] = jnp.zeros_like(zero_ref)
        n_valid = blk_ref[1, 0]

        def clear(b, n):
            last = blk_ref[0, b] != blk_ref[0, jnp.minimum(b + 1, n_blocks - 1)]
            do = jnp.logical_or(b >= n_valid - 1, last)

            @pl.when(do)
            def _():
                pltpu.make_async_copy(zero_ref, xs_ref.at[pl.ds(pl.multiple_of(b * blk_rows, blk_rows), blk_rows)],
                                      zsem).start()
            return n + do.astype(I32)
        n_started = lax.fori_loop(0, n_blocks, clear, 0)

        def drain(b, carry):
            pltpu.make_async_copy(zero_ref, xs_ref.at[pl.ds(0, blk_rows)], zsem).wait()
            return carry
        lax.fori_loop(0, n_started, drain, 0)

    def issue(r, carry):
        src = h_ref.at[pl.ds(pl.multiple_of(r * R, R), R)]
        for kk in range(2):
            dst = xs_ref.at[pl.ds(pl.multiple_of(dest_ref[0, 0, 2 * r + kk] * R, R), R)]
            pltpu.make_async_copy(src, dst, sem).start(priority=kk)
        return carry
    lax.fori_loop(0, tt, issue, 0, unroll=4)

    for kk in range(2):
        pltpu.make_async_copy(h_ref, xs_ref.at[pl.ds(0, tt * R)], sem).wait()


def _scatter_rows(dest_flat, blk, h_packed, *, cap, tt, R):
    T = h_packed.shape[0] // R
    n_blocks = cap // MOE_BLOCK
    kern = functools.partial(_scatter_kernel, tt=tt, n_blocks=n_blocks, R=R)
    return pl.pallas_call(
        kern,
        grid=(T // tt,),
        in_specs=[
            pl.BlockSpec((1, 1, 2 * tt), lambda i: (i, 0, 0), memory_space=pltpu.SMEM),
            pl.BlockSpec(memory_space=pltpu.SMEM),
            pl.BlockSpec((tt * R, HEAD_DIM), lambda i: (i, 0)),
        ],
        out_specs=pl.BlockSpec(memory_space=pl.ANY),
        out_shape=jax.ShapeDtypeStruct((cap * R, HEAD_DIM), U32),
        scratch_shapes=[pltpu.VMEM((MOE_BLOCK * R, HEAD_DIM), U32), pltpu.SemaphoreType.DMA,
                        pltpu.SemaphoreType.DMA],
        compiler_params=_cparams(("arbitrary",)),
        name="scatter_rows",
    )(dest_flat, blk, h_packed)


def _expert_kernel(blk_ref, x_ref, w1_ref, w3_ref, w2_ref, y_ref, w1b_ref, w3b_ref, w2b_ref):
    b = pl.program_id(0)
    valid = b < blk_ref[1, 0]
    new_expert = jnp.logical_or(b == 0, blk_ref[0, b] != blk_ref[0, jnp.maximum(b - 1, 0)])

    @pl.when(jnp.logical_and(valid, new_expert))
    def _():
        w1b_ref[...] = w1_ref[0, 0].astype(BF16)
        w3b_ref[...] = w3_ref[0, 0].astype(BF16)
        w2b_ref[...] = w2_ref[0, 0].astype(BF16)

    @pl.when(valid)
    def _():
        R = x_ref.shape[0] // MOE_BLOCK
        lo, hi = _unpack_words([x_ref[pl.ds(s, MOE_BLOCK, stride=R), :] for s in range(R)])
        x = jnp.concatenate(lo + hi, axis=1).astype(BF16)
        hid = _silu(_dot(x, w1b_ref[...])) * _dot(x, w3b_ref[...])
        y = _dot(hid.astype(BF16), w2b_ref[...])
        for s, word in enumerate(_pack_words(y)):
            y_ref[pl.ds(s, MOE_BLOCK, stride=R), :] = word

    @pl.when(b >= blk_ref[1, 0])
    def _():
        y_ref[...] = jnp.zeros_like(y_ref)


def _experts(blk, x_slots, w1, w3, w2, *, layer, R):
    cap = x_slots.shape[0] // R
    n_blocks = cap // MOE_BLOCK
    D, F = w1.shape[2], w1.shape[3]

    def row_blk(b, blk_ref):
        return jnp.minimum(b, blk_ref[1, 0] - 1)

    def w_idx(b, blk_ref):
        return (layer, blk_ref[0, row_blk(b, blk_ref)], 0, 0)

    grid_spec = pltpu.PrefetchScalarGridSpec(
        num_scalar_prefetch=1,
        grid=(n_blocks,),
        in_specs=[
            pl.BlockSpec((MOE_BLOCK * R, HEAD_DIM), lambda b, blk_ref: (row_blk(b, blk_ref), 0)),
            pl.BlockSpec((1, 1, D, F), w_idx),
            pl.BlockSpec((1, 1, D, F), w_idx),
            pl.BlockSpec((1, 1, F, D), w_idx),
        ],
        out_specs=pl.BlockSpec((MOE_BLOCK * R, HEAD_DIM), lambda b, blk_ref: (b, 0)),
        scratch_shapes=[pltpu.VMEM((D, F), BF16), pltpu.VMEM((D, F), BF16), pltpu.VMEM((F, D), BF16)],
    )
    return pl.pallas_call(
        _expert_kernel,
        grid_spec=grid_spec,
        out_shape=jax.ShapeDtypeStruct((cap * R, HEAD_DIM), U32),
        compiler_params=_cparams(("arbitrary",)),
        name="experts",
    )(blk, x_slots, w1, w3, w2)


def _combine_kernel(dest_ref, dnext_ref, ys_ref, x_ref, gate_ref, o_ref, buf_ref, sems, *, tt, R):
    i = pl.program_id(0)
    slot = lax.rem(i, 2)

    def issue(d_ref, s):
        def one(r, carry):
            for kk in range(2):
                src = ys_ref.at[pl.ds(pl.multiple_of(d_ref[0, 0, 2 * r + kk] * R, R), R)]
                pltpu.make_async_copy(src, buf_ref.at[s, kk, pl.ds(pl.multiple_of(r * R, R), R)],
                                      sems.at[s]).start(priority=kk)
            return carry
        lax.fori_loop(0, tt, one, 0, unroll=4)

    @pl.when(i == 0)
    def _():
        issue(dest_ref, 0)

    @pl.when(i + 1 < pl.num_programs(0))
    def _():
        issue(dnext_ref, 1 - slot)

    for kk in range(2):
        pltpu.make_async_copy(ys_ref.at[pl.ds(0, tt * R)], buf_ref.at[slot, kk], sems.at[slot]).wait()
    g = gate_ref[...]
    half = R * HEAD_DIM
    for s in range(R):
        lo0, hi0 = _unpack_words([buf_ref[slot, 0, pl.ds(s, tt, stride=R), :]])
        lo1, hi1 = _unpack_words([buf_ref[slot, 1, pl.ds(s, tt, stride=R), :]])
        for off, y0, y1 in ((s * HEAD_DIM, lo0[0], lo1[0]), (half + s * HEAD_DIM, hi0[0], hi1[0])):
            cs = slice(off, off + HEAD_DIM)
            o_ref[:, cs] = x_ref[:, cs] + (y0 * g[:, 0:1] + y1 * g[:, 1:2])


def _combine(dest_flat, y_slots, x, gate_t, *, tt, R):
    T, D = x.shape
    n = T // tt
    kern = functools.partial(_combine_kernel, tt=tt, R=R)
    return pl.pallas_call(
        kern,
        grid=(n,),
        in_specs=[
            pl.BlockSpec((1, 1, 2 * tt), lambda i: (i, 0, 0), memory_space=pltpu.SMEM),
            pl.BlockSpec((1, 1, 2 * tt), lambda i: (jnp.minimum(i + 1, n - 1), 0, 0), memory_space=pltpu.SMEM),
            pl.BlockSpec(memory_space=pl.ANY),
            pl.BlockSpec((tt, D), lambda i: (i, 0)),
            pl.BlockSpec((tt, 2), lambda i: (i, 0)),
        ],
        out_specs=pl.BlockSpec((tt, D), lambda i: (i, 0)),
        out_shape=jax.ShapeDtypeStruct((T, D), F32),
        scratch_shapes=[pltpu.VMEM((2, 2, tt * R, HEAD_DIM), U32), pltpu.SemaphoreType.DMA((2,))],
        compiler_params=_cparams(("arbitrary",)),
        name="combine",
    )(dest_flat, dest_flat, y_slots, x, gate_t)


def _mixer(x2, batch, layer, norm_w, w_in, q_norm_w, k_norm_w, conv_w, a_log, dt_bias, out_norm_w, w_o,
           *, tm, tn, tq, tk, hp, tt, tm_out):
    T, D = x2.shape
    S = T // batch
    n_gdn = a_log.shape[0]
    gdn_width = n_gdn * HEAD_DIM
    n_main = w_in.shape[2] - 2 * n_gdn
    sb_width = (n_main - 4 * gdn_width) // 3
    n_sb = sb_width // HEAD_DIM
    reps = tn // HEAD_DIM
    qk_w = jnp.stack([jnp.tile(q_norm_w, reps) * (1.0 / math.sqrt(HEAD_DIM)), jnp.tile(k_norm_w, reps)])
    proj, g, beta = _inproj(x2, norm_w.reshape(1, D), w_in, qk_w,
                            a_log.reshape(-1, 1), dt_bias.reshape(-1, 1), layer=layer, sb_width=sb_width,
                            n_main=n_main, tm=tm, tn=tn)
    proj3 = proj.reshape(batch, S, n_main)
    o_sb = _sb_attention(proj3, n_heads=n_sb, tq=tq, tk=tk, hp=hp)
    o_gdn = _gdn(proj3, conv_w, g, beta, out_norm_w.reshape(1, HEAD_DIM), n_heads=n_gdn,
                 col0=3 * sb_width, tt=tt)
    return _outproj(o_sb.reshape(T, sb_width), o_gdn.reshape(T, gdn_width), w_o.astype(BF16), x2, tm=tm_out)


def _moe(x2, layer, norm_w, w_group, b_group, w_expert, b_expert, w1, w3, w2, *, tm_r, tl, tt):
    T, D = x2.shape
    R = D // PACK_LANES
    tail = ROUTER_ROWS - 8 - N_EXPERTS
    wr_t = jnp.concatenate([w_group, jnp.zeros((D, 8 - N_GROUPS), F32), w_expert,
                            jnp.zeros((D, tail), F32)], axis=1).T
    br = jnp.concatenate([b_group, jnp.zeros((8 - N_GROUPS,), F32), b_expert,
                          jnp.zeros((tail,), F32)]).reshape(-1, 1)
    h, eid, gate = _router(x2, norm_w.reshape(1, D), wr_t, br, tm=tm_r)
    n_blocks = (2 * T + MOE_BLOCK - 1) // MOE_BLOCK + N_EXPERTS
    n_blk_pad = ((n_blocks + 127) // 128) * 128
    dest, blk = _plan(eid, tl=tl, n_blk_pad=n_blk_pad)
    dest_flat = dest.T.reshape(T // tt, 1, 2 * tt)
    x_slots = _scatter_rows(dest_flat, blk, h, cap=n_blocks * MOE_BLOCK, tt=tt, R=R)
    y_slots = _experts(blk, x_slots, w1, w3, w2, layer=layer, R=R)
    return _combine(dest_flat, y_slots, x2, gate.T, tt=tt, R=R)


def _forward(x, norm1_w, w_in, sb_q_norm_w, sb_k_norm_w, gdn_conv_w, gdn_a_log, gdn_dt_bias,
             gdn_out_norm_w, w_o, norm2_w, w_group, b_group, w_expert, b_expert, w1, w3, w2, *, tiles):
    batch, S, D = x.shape
    x2 = x.reshape(batch * S, D)
    for l in range(norm1_w.shape[0]):
        x2 = _mixer(x2, batch, l, norm1_w[l], w_in, sb_q_norm_w[l], sb_k_norm_w[l], gdn_conv_w[l],
                    gdn_a_log[l], gdn_dt_bias[l], gdn_out_norm_w[l], w_o[l], **tiles["mixer"])
        x2 = _moe(x2, l, norm2_w[l], w_group[l], b_group[l], w_expert[l], b_expert[l],
                  w1, w3, w2, **tiles["moe"])
    return x2.reshape(batch, S, D)


_TILES = {
    "mixer": dict(tm=1024, tn=1024, tq=256, tk=256, hp=4, tt=512, tm_out=512),
    "moe": dict(tm_r=512, tl=512, tt=256),
}


def kernel(x, norm1_w, w_in, sb_q_norm_w, sb_k_norm_w, gdn_conv_w, gdn_a_log, gdn_dt_bias, gdn_out_norm_w, w_o, norm2_w, w_group, b_group, w_expert, b_expert, w1, w3, w2):
    return _forward(x, norm1_w, w_in, sb_q_norm_w, sb_k_norm_w, gdn_conv_w, gdn_a_log, gdn_dt_bias,
                    gdn_out_norm_w, w_o, norm2_w, w_group, b_group, w_expert, b_expert, w1, w3, w2,
                    tiles=_TILES)
```

```python
import functools
import math

import jax
import jax.numpy as jnp
from jax import lax
from jax.experimental import pallas as pl
from jax.experimental.pallas import tpu as pltpu

F32 = jnp.float32
BF16 = jnp.bfloat16
I32 = jnp.int32

HEAD_DIM = 128
GDN_CONV = 4
GDN_CHUNK = 64
N_GROUPS = 4
EXPERTS_PER_GROUP = 8
N_EXPERTS = N_GROUPS * EXPERTS_PER_GROUP
MOE_BLOCK = 512
RMS_EPS = 1e-6
EXP_UNDERFLOW = -104.0
ROUTER_ROWS = 48
VMEM_LIMIT = 56 * 1024 * 1024


def _cparams(sem, vmem=VMEM_LIMIT):
    return pltpu.CompilerParams(dimension_semantics=sem, vmem_limit_bytes=vmem)


def _dot(a, b, dims=(((1,), (0,)), ((), ()))):
    return lax.dot_general(a, b, dims, preferred_element_type=F32)


_NN = (((1,), (0,)), ((), ()))
_NT = (((1,), (1,)), ((), ()))
_TN = (((0,), (0,)), ((), ()))


def _split(a):
    hi = a.astype(BF16)
    lo = (a - hi.astype(F32)).astype(BF16)
    return hi, lo


def _dot3(a, b, dims=(((1,), (0,)), ((), ()))):
    ah, al = _split(a)
    bh, bl = _split(b)
    return _dot(ah, bh, dims) + (_dot(ah, bl, dims) + _dot(al, bh, dims))


U32 = jnp.uint32
PACK_LANES = 2 * HEAD_DIM


def _pack_words(a):
    half = a.shape[1] // 2
    bits = lax.bitcast_convert_type(a.astype(BF16).astype(F32), U32)
    return [(bits[:, half + s * HEAD_DIM:half + (s + 1) * HEAD_DIM] & jnp.uint32(0xFFFF0000))
            | (bits[:, s * HEAD_DIM:(s + 1) * HEAD_DIM] >> 16) for s in range(half // HEAD_DIM)]


def _unpack_words(words):
    lo = [lax.bitcast_convert_type(w << 16, F32) for w in words]
    hi = [lax.bitcast_convert_type(w & jnp.uint32(0xFFFF0000), F32) for w in words]
    return lo, hi


def _softplus(x):
    return jnp.maximum(x, 0.0) + jnp.log(1.0 + jnp.exp(-jnp.abs(x)))


def _sigmoid(x):
    return 1.0 / (1.0 + jnp.exp(-x))


def _inproj_kernel(x_ref, nw_ref, w_ref, wab_ref, qkw_ref, alog_ref, dt_ref,
                   proj_ref, g_ref, beta_ref, h_ref, *, n_qk_tiles, n_heads, rows):
    j = pl.program_id(1)
    tm = x_ref.shape[0]

    @pl.when(j == 0)
    def _():
        def norm_rows(r, carry):
            rs = pl.multiple_of(r * rows, rows)
            x = x_ref[pl.ds(rs, rows), :]
            ms = jnp.mean(x * x, axis=-1, keepdims=True)
            h = x * lax.rsqrt(ms + RMS_EPS) * nw_ref[...]
            h_ref[pl.ds(rs, rows), :] = h.astype(BF16)
            return carry
        lax.fori_loop(0, tm // rows, norm_rows, 0)
        ab = _dot(wab_ref[0].astype(F32).T.astype(BF16), h_ref[...], _NT)
        ga = ab[0:n_heads]
        gb = ab[n_heads:2 * n_heads]
        g_ref[...] = -jnp.exp(alog_ref[...]) * _softplus(ga + dt_ref[...])
        beta_ref[...] = _sigmoid(gb)

    acc = _dot(h_ref[...], w_ref[0])

    is_qk = j < n_qk_tiles
    wsel = jnp.where(j < n_qk_tiles // 2, qkw_ref[0:1, :], qkw_ref[1:2, :])
    for hh in range(acc.shape[1] // HEAD_DIM):
        sl = slice(hh * HEAD_DIM, (hh + 1) * HEAD_DIM)
        a = acc[:, sl]
        ms = jnp.mean(a * a, axis=-1, keepdims=True)
        normed = a * lax.rsqrt(ms + RMS_EPS) * wsel[:, sl]
        proj_ref[:, sl] = jnp.where(is_qk, normed, a).astype(BF16)


def _inproj(x, norm_w, w_main, qk_w, a_log, dt_bias, *, layer, sb_width, n_main, tm, tn):
    T, D = x.shape
    assert n_main % HEAD_DIM == 0 and w_main.shape[2] == n_main + 2 * a_log.shape[0]
    N = n_main
    n_heads = a_log.shape[0]
    n_qk_tiles = 2 * sb_width // tn
    kern = functools.partial(_inproj_kernel, n_qk_tiles=n_qk_tiles, n_heads=n_heads,
                             rows=min(tm, 256))
    return pl.pallas_call(
        kern,
        grid=(T // tm, N // tn),
        in_specs=[
            pl.BlockSpec((tm, D), lambda i, j: (i, 0)),
            pl.BlockSpec((1, D), lambda i, j: (0, 0)),
            pl.BlockSpec((1, D, tn), lambda i, j: (layer, 0, j)),
            pl.BlockSpec((1, D, HEAD_DIM), lambda i, j: (layer, 0, n_main // HEAD_DIM)),
            pl.BlockSpec((2, tn), lambda i, j: (0, 0)),
            pl.BlockSpec((n_heads, 1), lambda i, j: (0, 0)),
            pl.BlockSpec((n_heads, 1), lambda i, j: (0, 0)),
        ],
        out_specs=[
            pl.BlockSpec((tm, tn), lambda i, j: (i, j)),
            pl.BlockSpec((n_heads, tm), lambda i, j: (0, i)),
            pl.BlockSpec((n_heads, tm), lambda i, j: (0, i)),
        ],
        out_shape=[
            jax.ShapeDtypeStruct((T, N), BF16),
            jax.ShapeDtypeStruct((n_heads, T), F32),
            jax.ShapeDtypeStruct((n_heads, T), F32),
        ],
        scratch_shapes=[pltpu.VMEM((tm, D), BF16)],
        compiler_params=_cparams(("arbitrary", "arbitrary")),
        name="inproj",
    )(x, norm_w, w_main, w_main, qk_w, a_log, dt_bias)


def _sb_kernel(q_ref, k_ref, v_ref, o_ref, *, tq, tk, hp):
    i = pl.program_id(2)
    heads = range(hp)
    hs = [slice(h * HEAD_DIM, (h + 1) * HEAD_DIM) for h in heads]
    q = [q_ref[0, :, hs[h]] for h in heads]
    row = lax.broadcasted_iota(I32, (tk, tk), 0)
    col = lax.broadcasted_iota(I32, (tk, tk), 1)
    upper_incl = (row >= col).astype(BF16)
    rel = lax.broadcasted_iota(I32, (tq, tk), 1) - lax.broadcasted_iota(I32, (tq, tk), 0)

    def block(ks, carry, masked):
        acc, c = carry
        z = [_dot(q[h], k_ref[0, pl.ds(ks, tk), hs[h]], _NT) for h in heads]
        log_1mb = [-_softplus(z[h]) for h in heads]
        if masked:
            causal = rel + (ks - i * tq) < 0
            log_1mb = [jnp.where(causal, log_1mb[h], 0.0) for h in heads]
        parts = [_split(log_1mb[h]) for h in heads]
        incl = [_dot(parts[h][0], upper_incl) + _dot(parts[h][1], upper_incl) for h in heads]
        w = [jnp.exp(z[h] + incl[h] + c[h]) for h in heads]
        if masked:
            w = [jnp.where(causal, w[h], 0.0) for h in heads]
        acc = [acc[h] + _dot(_bf(w[h]), v_ref[0, pl.ds(ks, tk), hs[h]]) for h in heads]
        c = [c[h] + incl[h][:, 0:1] for h in heads]
        return acc, c

    carry = ([jnp.zeros((tq, HEAD_DIM), F32) for _ in heads], [jnp.zeros((tq, 1), F32) for _ in heads])
    n_diag = tq // tk
    for d in range(n_diag - 1, -1, -1):
        carry = block(pl.multiple_of(i * tq + d * tk, tk), carry, True)

    def c_max(c):
        m = c[0]
        for h in heads[1:]:
            m = jnp.maximum(m, c[h])
        return jnp.max(m)

    def cond(state):
        jj, live, _ = state
        return jnp.logical_and(jj < i * n_diag, live)

    def body(state):
        jj, _, carry = state
        carry = block(pl.multiple_of((i * n_diag - 1 - jj) * tk, tk), carry, False)
        return jj + 1, c_max(carry[1]) > EXP_UNDERFLOW, carry

    _, _, (acc, _) = lax.while_loop(cond, body, (jnp.int32(0), c_max(carry[1]) > EXP_UNDERFLOW, carry))
    for h in heads:
        o_ref[0, :, hs[h]] = acc[h].astype(o_ref.dtype)


def _sb_attention(proj3, *, n_heads, tq, tk, hp):
    B, S, _ = proj3.shape
    kern = functools.partial(_sb_kernel, tq=tq, tk=tk, hp=hp)
    ng = n_heads // hp
    wd = hp * HEAD_DIM
    return pl.pallas_call(
        kern,
        grid=(B, ng, S // tq),
        in_specs=[
            pl.BlockSpec((1, tq, wd), lambda b, h, i: (b, i, h)),
            pl.BlockSpec((1, S, wd), lambda b, h, i: (b, 0, ng + h)),
            pl.BlockSpec((1, S, wd), lambda b, h, i: (b, 0, 2 * ng + h)),
        ],
        out_specs=pl.BlockSpec((1, tq, wd), lambda b, h, i: (b, i, h)),
        out_shape=jax.ShapeDtypeStruct((B, S, n_heads * HEAD_DIM), BF16),
        compiler_params=_cparams(("arbitrary", "arbitrary", "arbitrary")),
        name="sb_attention",
    )(proj3, proj3, proj3)


def _silu(x):
    return x * _sigmoid(x)


SUB = 2 * GDN_CHUNK


def _bf(a):
    return a.astype(BF16)


def _exact_parts(a):
    p1 = a.astype(BF16)
    r1 = a - p1.astype(F32)
    p2 = r1.astype(BF16)
    p3 = (r1 - p2.astype(F32)).astype(BF16)
    return p1, p2, p3


def _pair_dots(a, b, nt=False):
    return [_dot(x, y, _NT if nt else _NN) for x, y in zip(a, b)]


def _gdn_kernel(xq_ref, xk_ref, xv_ref, z_ref, cw_ref, g_ref, b_ref, ow_ref,
                o_ref, xs_ref, qkv_ref, state_ref, *, tt, n_heads):
    t = pl.program_id(1)
    C = GDN_CHUNK
    HALO = 8
    W = n_heads * HEAD_DIM
    heads = range(n_heads)

    @pl.when(t == 0)
    def _():
        xs_ref[:, 0:HALO, :] = jnp.zeros((3, HALO, W), F32)
        state_ref[...] = jnp.zeros_like(state_ref)

    @pl.when(t > 0)
    def _():
        xs_ref[:, 0:HALO, :] = xs_ref[:, tt:tt + HALO, :]

    for idx, x_ref in enumerate((xq_ref, xk_ref, xv_ref)):
        for h in heads:
            cs = slice(h * HEAD_DIM, (h + 1) * HEAD_DIM)
            xs_ref[idx, HALO:, cs] = x_ref[0, :, cs].astype(F32)
            acc = None
            for kk in range(GDN_CONV):
                wrow = cw_ref[kk:kk + 1, idx * W + h * HEAD_DIM: idx * W + (h + 1) * HEAD_DIM]
                term = xs_ref[idx, pl.ds(HALO - GDN_CONV + 1 + kk, tt), cs] * wrow
                acc = term if acc is None else acc + term
            a = _silu(acc)
            if idx < 2:
                inv = lax.rsqrt(jnp.sum(a * a, axis=-1, keepdims=True) + RMS_EPS)
                a = a * (inv * (1.0 / math.sqrt(HEAD_DIM)) if idx == 0 else inv)
            qkv_ref[idx, :, cs] = a

    ri = lax.broadcasted_iota(I32, (SUB, SUB), 0)
    ci = lax.broadcasted_iota(I32, (SUB, SUB), 1)
    same = (ri >> 6) == (ci >> 6)
    lower_incl = jnp.logical_and(same, ri >= ci)
    lower_strict = jnp.logical_and(same, ri > ci)
    blk16 = (ri >> 4) == (ci >> 4)
    blk32 = (ri >> 5) == (ci >> 5)
    only32 = jnp.logical_and(blk32, jnp.logical_not(blk16))
    only64 = jnp.logical_and(same, jnp.logical_not(blk32))
    eye_f = (ri == ci).astype(F32)
    eye_b = (ri == ci).astype(BF16)
    cum_b = jnp.logical_and(same, ri <= ci).astype(BF16)
    first = lax.broadcasted_iota(I32, (SUB, 1), 0) < C

    def sub_tile(s, carry):
        r0 = pl.multiple_of(s * SUB, SUB)
        g_rows = g_ref[:, pl.ds(r0, SUB)]
        b_rows = b_ref[:, pl.ds(r0, SUB)]
        gp = _exact_parts(g_rows)
        gc_rows = _dot(gp[0], cum_b) + _dot(gp[1], cum_b) + _dot(gp[2], cum_b)
        sp = _exact_parts(jnp.concatenate([gc_rows, b_rows], axis=0))
        cols = _dot(eye_b, sp[0], _NT) + _dot(eye_b, sp[1], _NT) + _dot(eye_b, sp[2], _NT)

        q, k, v, beta, decay, egc, kdec, eg_last = [], [], [], [], [], [], [], []
        for h in heads:
            cs = slice(h * HEAD_DIM, (h + 1) * HEAD_DIM)
            q.append(qkv_ref[0, pl.ds(r0, SUB), cs])
            k.append(qkv_ref[1, pl.ds(r0, SUB), cs])
            v.append(qkv_ref[2, pl.ds(r0, SUB), cs])
            gc_col = cols[:, h:h + 1]
            beta.append(cols[:, n_heads + h:n_heads + h + 1])
            decay.append(jnp.where(lower_incl, jnp.exp(jnp.minimum(gc_col - gc_rows[h:h + 1, :], 0.0)), 0.0))
            egc.append(jnp.exp(gc_col))
            g_last = jnp.where(first, gc_col[C - 1:C, :], gc_col[SUB - 1:SUB, :])
            kdec.append(_bf(k[h] * jnp.exp(g_last - gc_col)))
            eg_last.append((jnp.exp(gc_col[C - 1:C, :]), jnp.exp(gc_col[SUB - 1:SUB, :])))

        def bfl(xs):
            return [_bf(x) for x in xs]

        kb = [k[h] * beta[h] for h in heads]
        k_b = bfl(k)
        kk = _pair_dots(bfl(kb), k_b, nt=True)
        lmat = [jnp.where(lower_strict, kk[h] * decay[h], 0.0) for h in heads]
        d16 = [jnp.where(blk16, lmat[h], 0.0) for h in heads]
        d16_b = bfl(d16)
        p2 = bfl(_pair_dots(d16_b, d16_b))
        p4 = bfl(_pair_dots(p2, p2))
        p8 = bfl(_pair_dots(p4, p4))
        x0 = [eye_f - d16[h] for h in heads]
        for p in (p2, p4, p8):
            step = _pair_dots(bfl(x0), p)
            x0 = [x0[h] + step[h] for h in heads]
        x0_b = bfl(x0)
        y1 = bfl(_pair_dots(x0_b, bfl([jnp.where(only32, lmat[h], 0.0) for h in heads])))
        step = _pair_dots(y1, x0_b)
        x1 = [x0[h] - step[h] for h in heads]
        x1_b = bfl(x1)
        y2 = bfl(_pair_dots(x1_b, bfl([jnp.where(only64, lmat[h], 0.0) for h in heads])))
        step = _pair_dots(y2, x1_b)
        t_b = bfl([x1[h] - step[h] for h in heads])
        uw = _pair_dots(t_b, bfl([jnp.concatenate([v[h] * beta[h], kb[h] * egc[h]], axis=1) for h in heads]))
        u = [uw[h][:, :HEAD_DIM] for h in heads]
        w = [uw[h][:, HEAD_DIM:] for h in heads]
        qk = _pair_dots(bfl(q), k_b, nt=True)
        attn = bfl([jnp.where(lower_incl, qk[h] * decay[h], 0.0) for h in heads])
        qg = [q[h] * egc[h] for h in heads]

        state = [state_ref[h] for h in heads]
        zeros = jnp.zeros((C, HEAD_DIM), F32)
        for c in range(2):
            rows = slice(c * C, (c + 1) * C)
            wq = bfl([jnp.concatenate([w[h][rows], qg[h][rows]], axis=0) for h in heads])
            r = _pair_dots(wq, bfl(state))
            v_new = [u[h][rows] - r[h][0:C] for h in heads]
            v_pad = bfl([jnp.concatenate([v_new[h], zeros] if c == 0 else [zeros, v_new[h]], axis=0)
                         for h in heads])
            av = _pair_dots([attn[h][rows] for h in heads], v_pad)
            o = [r[h][C:SUB] + av[h] for h in heads]
            state = [state[h] * eg_last[h][c] + _dot(kdec[h][rows], _bf(v_new[h]), _TN) for h in heads]
            for h in heads:
                cs = slice(h * HEAD_DIM, (h + 1) * HEAD_DIM)
                ms = jnp.mean(o[h] * o[h], axis=-1, keepdims=True)
                zc = z_ref[0, pl.ds(r0 + c * C, C), cs].astype(F32)
                o_ref[0, pl.ds(r0 + c * C, C), cs] = (
                    o[h] * lax.rsqrt(ms + RMS_EPS) * ow_ref[...] * _silu(zc)).astype(o_ref.dtype)
        for h in heads:
            state_ref[h] = state[h]
        return carry

    lax.fori_loop(0, tt // SUB, sub_tile, 0)


def _gdn(proj3, conv_w, g, beta, out_norm_w, *, n_heads, col0, tt):
    B, S, _ = proj3.shape
    W = n_heads * HEAD_DIM
    kern = functools.partial(_gdn_kernel, tt=tt, n_heads=n_heads)
    cb = col0 // W
    nt = S // tt
    xspec = lambda off: pl.BlockSpec((1, tt, W), lambda b, t: (b, t, cb + off))
    gspec = pl.BlockSpec((n_heads, tt), lambda b, t: (0, b * nt + t))
    return pl.pallas_call(
        kern,
        grid=(B, nt),
        in_specs=[xspec(0), xspec(1), xspec(2), xspec(3),
                  pl.BlockSpec((GDN_CONV, 3 * W), lambda b, t: (0, 0)),
                  gspec, gspec, pl.BlockSpec((1, HEAD_DIM), lambda b, t: (0, 0))],
        out_specs=pl.BlockSpec((1, tt, W), lambda b, t: (b, t, 0)),
        out_shape=jax.ShapeDtypeStruct((B, S, W), BF16),
        scratch_shapes=[pltpu.VMEM((3, tt + 8, W), F32), pltpu.VMEM((3, tt, W), F32),
                        pltpu.VMEM((n_heads, HEAD_DIM, HEAD_DIM), F32)],
        compiler_params=_cparams(("arbitrary", "arbitrary")),
        name="gdn",
    )(proj3, proj3, proj3, proj3, conv_w, g, beta, out_norm_w)


def _outproj_router_kernel(a_ref, b_ref, wa_ref, wb_ref, x_ref, nw_ref, wr_ref, br_ref,
                           o_ref, h_ref, eid_ref, gate_ref):
    x = x_ref[...] + _dot(a_ref[...], wa_ref[...]) + _dot(b_ref[...], wb_ref[...])
    o_ref[...] = x
    tm = x.shape[0]
    ms = jnp.mean(x * x, axis=-1, keepdims=True)
    h = x * lax.rsqrt(ms + RMS_EPS) * nw_ref[...]
    words = _pack_words(h)
    for s, word in enumerate(words):
        h_ref[pl.ds(s, tm, stride=len(words)), :] = word
    logits = _dot3(wr_ref[...], h, _NT) + br_ref[...]
    best = logits[0:1]
    gidx = jnp.zeros((1, tm), I32)
    for g in range(1, N_GROUPS):
        better = logits[g:g + 1] > best
        gidx = jnp.where(better, g, gidx)
        best = jnp.where(better, logits[g:g + 1], best)
    gsum = jnp.zeros((1, tm), F32)
    for g in range(N_GROUPS):
        gsum = gsum + jnp.exp(logits[g:g + 1] - best)
    group_gate = 1.0 / gsum
    E = EXPERTS_PER_GROUP
    in_group = jnp.zeros((E, tm), F32)
    for g in range(N_GROUPS):
        in_group = jnp.where(gidx == g, logits[8 + g * E:8 + (g + 1) * E], in_group)
    sub = lax.broadcasted_iota(I32, (E, tm), 0)
    m1 = jnp.max(in_group, axis=0, keepdims=True)
    i1 = jnp.min(jnp.where(in_group == m1, sub, E), axis=0, keepdims=True)
    rest = jnp.where(sub == i1, -jnp.inf, in_group)
    m2 = jnp.max(rest, axis=0, keepdims=True)
    i2 = jnp.min(jnp.where(rest == m2, sub, E), axis=0, keepdims=True)
    e2 = jnp.exp(m2 - m1)
    inv = group_gate / (1.0 + e2)
    eid_ref[0:1, :] = gidx * E + i1
    eid_ref[1:2, :] = gidx * E + i2
    gate_ref[0:1, :] = inv
    gate_ref[1:2, :] = inv * e2


def _outproj_router(o_sb, o_gdn, w_o, x, norm_w, wr_t, br, *, tm):
    T, D = x.shape
    R = D // PACK_LANES
    wa, wb = o_sb.shape[1], o_gdn.shape[1]
    assert wa == wb and w_o.shape == (wa + wb, D)
    return pl.pallas_call(
        _outproj_router_kernel,
        grid=(T // tm,),
        in_specs=[
            pl.BlockSpec((tm, wa), lambda i: (i, 0)),
            pl.BlockSpec((tm, wb), lambda i: (i, 0)),
            pl.BlockSpec((wa, D), lambda i: (0, 0)),
            pl.BlockSpec((wb, D), lambda i: (1, 0)),
            pl.BlockSpec((tm, D), lambda i: (i, 0)),
            pl.BlockSpec((1, D), lambda i: (0, 0)),
            pl.BlockSpec((ROUTER_ROWS, D), lambda i: (0, 0)),
            pl.BlockSpec((ROUTER_ROWS, 1), lambda i: (0, 0)),
        ],
        out_specs=[
            pl.BlockSpec((tm, D), lambda i: (i, 0)),
            pl.BlockSpec((tm * R, HEAD_DIM), lambda i: (i, 0)),
            pl.BlockSpec((2, tm), lambda i: (0, i)),
            pl.BlockSpec((2, tm), lambda i: (0, i)),
        ],
        out_shape=[
            jax.ShapeDtypeStruct((T, D), F32),
            jax.ShapeDtypeStruct((T * R, HEAD_DIM), U32),
            jax.ShapeDtypeStruct((2, T), I32),
            jax.ShapeDtypeStruct((2, T), F32),
        ],
        compiler_params=_cparams(("arbitrary",)),
        name="outproj_router",
    )(o_sb, o_gdn, w_o, w_o, x, norm_w, wr_t, br)


def _plan_kernel(eid_ref, dest_ref, blk_ref, cnt_ref, *, tl, n_blk_pad):
    p = pl.program_id(0)
    i = pl.program_id(1)
    NE = N_EXPERTS
    e0 = eid_ref[0:1, :]
    e1 = eid_ref[1:2, :]
    sub = lax.broadcasted_iota(I32, (NE, tl), 0)
    hot0 = sub == e0
    hot1 = sub == e1
    onehot = jnp.logical_or(hot0, hot1).astype(BF16)
    ones = jnp.ones((tl, HEAD_DIM), BF16)

    @pl.when(jnp.logical_and(p == 0, i == 0))
    def _():
        cnt_ref[0] = jnp.zeros((NE, HEAD_DIM), F32)

    @pl.when(p == 0)
    def _():
        cnt_ref[0] += _dot(onehot, ones)

    @pl.when(jnp.logical_and(p == 1, i == 0))
    def _():
        cnt = cnt_ref[0]
        padded = jnp.floor((cnt + (MOE_BLOCK - 1)) * (1.0 / MOE_BLOCK)) * MOE_BLOCK
        er = lax.broadcasted_iota(I32, (NE, NE), 0)
        ec = lax.broadcasted_iota(I32, (NE, NE), 1)
        start = _dot3((ec < er).astype(F32), padded)
        cnt_ref[1] = start
        end_col = (start + padded)[:, 0:1]
        pos = (lax.broadcasted_iota(I32, (NE, n_blk_pad), 1) * MOE_BLOCK).astype(F32)
        n_before = jnp.sum((end_col <= pos).astype(I32), axis=0, keepdims=True)
        blk_ref[0:1, :] = jnp.minimum(n_before, NE - 1)
        total = jnp.max(end_col, axis=0, keepdims=True)
        blk_ref[1:2, :] = jnp.broadcast_to((total * (1.0 / MOE_BLOCK)).astype(I32), (1, n_blk_pad))
        cnt_ref[0] = jnp.zeros((NE, HEAD_DIM), F32)

    @pl.when(p == 1)
    def _():
        r = lax.broadcasted_iota(I32, (tl, tl), 0)
        c = lax.broadcasted_iota(I32, (tl, tl), 1)
        before = (r < c).astype(BF16)
        run = cnt_ref[0][:, 0:1] + cnt_ref[1][:, 0:1]
        slot = _dot(onehot, before) + run
        dest_ref[0:1, :] = jnp.sum(jnp.where(hot0, slot, 0.0), axis=0, keepdims=True).astype(I32)
        dest_ref[1:2, :] = jnp.sum(jnp.where(hot1, slot, 0.0), axis=0, keepdims=True).astype(I32)
        cnt_ref[0] += _dot(onehot, ones)


def _plan(eid, *, tl, n_blk_pad):
    T = eid.shape[1]
    kern = functools.partial(_plan_kernel, tl=tl, n_blk_pad=n_blk_pad)
    return pl.pallas_call(
        kern,
        grid=(2, T // tl),
        in_specs=[pl.BlockSpec((2, tl), lambda p, i: (0, i))],
        out_specs=[
            pl.BlockSpec((2, tl), lambda p, i: (0, i * p)),
            pl.BlockSpec((2, n_blk_pad), lambda p, i: (0, 0)),
        ],
        out_shape=[
            jax.ShapeDtypeStruct((2, T), I32),
            jax.ShapeDtypeStruct((2, n_blk_pad), I32),
        ],
        scratch_shapes=[pltpu.VMEM((2, N_EXPERTS, HEAD_DIM), F32)],
        compiler_params=_cparams(("arbitrary", "arbitrary")),
        name="plan",
    )(eid)


def _scatter_kernel(dest_ref, blk_ref, h_ref, xs_ref, zero_ref, sem, zsem, *, tt, n_blocks, R):
    i = pl.program_id(0)
    blk_rows = MOE_BLOCK * R

    @pl.when(i == 0)
    def _():
        zero_ref[...] = jnp.zeros_like(zero_ref)
        n_valid = blk_ref[1, 0]

        def clear(b, n):
            last = blk_ref[0, b] != blk_ref[0, jnp.minimum(b + 1, n_blocks - 1)]
            do = jnp.logical_or(b >= n_valid - 1, last)

            @pl.when(do)
            def _():
                pltpu.make_async_copy(zero_ref, xs_ref.at[pl.ds(pl.multiple_of(b * blk_rows, blk_rows), blk_rows)],
                                      zsem).start()
            return n + do.astype(I32)
        n_started = lax.fori_loop(0, n_blocks, clear, 0)

        def drain(b, carry):
            pltpu.make_async_copy(zero_ref, xs_ref.at[pl.ds(0, blk_rows)], zsem).wait()
            return carry
        lax.fori_loop(0, n_started, drain, 0)

    def issue(r, carry):
        src = h_ref.at[pl.ds(pl.multiple_of(r * R, R), R)]
        for kk in range(2):
            dst = xs_ref.at[pl.ds(pl.multiple_of(dest_ref[0, 0, 2 * r + kk] * R, R), R)]
            pltpu.make_async_copy(src, dst, sem).start(priority=kk)
        return carry
    lax.fori_loop(0, tt, issue, 0, unroll=4)

    for kk in range(2):
        pltpu.make_async_copy(h_ref, xs_ref.at[pl.ds(0, tt * R)], sem).wait()


def _scatter_rows(dest_flat, blk, h_packed, *, cap, tt, R):
    T = h_packed.shape[0] // R
    n_blocks = cap // MOE_BLOCK
    kern = functools.partial(_scatter_kernel, tt=tt, n_blocks=n_blocks, R=R)
    return pl.pallas_call(
        kern,
        grid=(T // tt,),
        in_specs=[
            pl.BlockSpec((1, 1, 2 * tt), lambda i: (i, 0, 0), memory_space=pltpu.SMEM),
            pl.BlockSpec(memory_space=pltpu.SMEM),
            pl.BlockSpec((tt * R, HEAD_DIM), lambda i: (i, 0)),
        ],
        out_specs=pl.BlockSpec(memory_space=pl.ANY),
        out_shape=jax.ShapeDtypeStruct((cap * R, HEAD_DIM), U32),
        scratch_shapes=[pltpu.VMEM((MOE_BLOCK * R, HEAD_DIM), U32), pltpu.SemaphoreType.DMA,
                        pltpu.SemaphoreType.DMA],
        compiler_params=_cparams(("arbitrary",)),
        name="scatter_rows",
    )(dest_flat, blk, h_packed)


def _expert_kernel(blk_ref, x_ref, w1_ref, w3_ref, w2_ref, y_ref, w1b_ref, w3b_ref, w2b_ref):
    b = pl.program_id(0)
    valid = b < blk_ref[1, 0]
    new_expert = jnp.logical_or(b == 0, blk_ref[0, b] != blk_ref[0, jnp.maximum(b - 1, 0)])

    @pl.when(jnp.logical_and(valid, new_expert))
    def _():
        w1b_ref[...] = w1_ref[0, 0].astype(BF16)
        w3b_ref[...] = w3_ref[0, 0].astype(BF16)
        w2b_ref[...] = w2_ref[0, 0].astype(BF16)

    @pl.when(valid)
    def _():
        R = x_ref.shape[0] // MOE_BLOCK
        lo, hi = _unpack_words([x_ref[pl.ds(s, MOE_BLOCK, stride=R), :] for s in range(R)])
        x = jnp.concatenate(lo + hi, axis=1).astype(BF16)
        hid = _silu(_dot(x, w1b_ref[...])) * _dot(x, w3b_ref[...])
        y = _dot(hid.astype(BF16), w2b_ref[...])
        for s, word in enumerate(_pack_words(y)):
            y_ref[pl.ds(s, MOE_BLOCK, stride=R), :] = word

    @pl.when(b >= blk_ref[1, 0])
    def _():
        y_ref[...] = jnp.zeros_like(y_ref)


def _experts(blk, x_slots, w1, w3, w2, *, layer, R):
    cap = x_slots.shape[0] // R
    n_blocks = cap // MOE_BLOCK
    D, F = w1.shape[2], w1.shape[3]

    def row_blk(b, blk_ref):
        return jnp.minimum(b, blk_ref[1, 0] - 1)

    def w_idx(b, blk_ref):
        return (layer, blk_ref[0, row_blk(b, blk_ref)], 0, 0)

    grid_spec = pltpu.PrefetchScalarGridSpec(
        num_scalar_prefetch=1,
        grid=(n_blocks,),
        in_specs=[
            pl.BlockSpec((MOE_BLOCK * R, HEAD_DIM), lambda b, blk_ref: (row_blk(b, blk_ref), 0)),
            pl.BlockSpec((1, 1, D, F), w_idx),
            pl.BlockSpec((1, 1, D, F), w_idx),
            pl.BlockSpec((1, 1, F, D), w_idx),
        ],
        out_specs=pl.BlockSpec((MOE_BLOCK * R, HEAD_DIM), lambda b, blk_ref: (b, 0)),
        scratch_shapes=[pltpu.VMEM((D, F), BF16), pltpu.VMEM((D, F), BF16), pltpu.VMEM((F, D), BF16)],
    )
    return pl.pallas_call(
        _expert_kernel,
        grid_spec=grid_spec,
        out_shape=jax.ShapeDtypeStruct((cap * R, HEAD_DIM), U32),
        compiler_params=_cparams(("arbitrary",)),
        name="experts",
    )(blk, x_slots, w1, w3, w2)


def _combine_kernel(dest_ref, dnext_ref, ys_ref, x_ref, gate_ref, o_ref, buf_ref, sems, *, tt, R):
    i = pl.program_id(0)
    slot = lax.rem(i, 2)

    def issue(d_ref, s):
        def one(r, carry):
            for kk in range(2):
                src = ys_ref.at[pl.ds(pl.multiple_of(d_ref[0, 0, 2 * r + kk] * R, R), R)]
                pltpu.make_async_copy(src, buf_ref.at[s, kk, pl.ds(pl.multiple_of(r * R, R), R)],
                                      sems.at[s]).start(priority=kk)
            return carry
        lax.fori_loop(0, tt, one, 0, unroll=4)

    @pl.when(i == 0)
    def _():
        issue(dest_ref, 0)

    @pl.when(i + 1 < pl.num_programs(0))
    def _():
        issue(dnext_ref, 1 - slot)

    for kk in range(2):
        pltpu.make_async_copy(ys_ref.at[pl.ds(0, tt * R)], buf_ref.at[slot, kk], sems.at[slot]).wait()
    g = gate_ref[...]
    half = R * HEAD_DIM
    for s in range(R):
        lo0, hi0 = _unpack_words([buf_ref[slot, 0, pl.ds(s, tt, stride=R), :]])
        lo1, hi1 = _unpack_words([buf_ref[slot, 1, pl.ds(s, tt, stride=R), :]])
        for off, y0, y1 in ((s * HEAD_DIM, lo0[0], lo1[0]), (half + s * HEAD_DIM, hi0[0], hi1[0])):
            cs = slice(off, off + HEAD_DIM)
            o_ref[:, cs] = x_ref[:, cs] + (y0 * g[:, 0:1] + y1 * g[:, 1:2])


def _combine(dest_flat, y_slots, x, gate_t, *, tt, R):
    T, D = x.shape
    n = T // tt
    kern = functools.partial(_combine_kernel, tt=tt, R=R)
    return pl.pallas_call(
        kern,
        grid=(n,),
        in_specs=[
            pl.BlockSpec((1, 1, 2 * tt), lambda i: (i, 0, 0), memory_space=pltpu.SMEM),
            pl.BlockSpec((1, 1, 2 * tt), lambda i: (jnp.minimum(i + 1, n - 1), 0, 0), memory_space=pltpu.SMEM),
            pl.BlockSpec(memory_space=pl.ANY),
            pl.BlockSpec((tt, D), lambda i: (i, 0)),
            pl.BlockSpec((tt, 2), lambda i: (i, 0)),
        ],
        out_specs=pl.BlockSpec((tt, D), lambda i: (i, 0)),
        out_shape=jax.ShapeDtypeStruct((T, D), F32),
        scratch_shapes=[pltpu.VMEM((2, 2, tt * R, HEAD_DIM), U32), pltpu.SemaphoreType.DMA((2,))],
        compiler_params=_cparams(("arbitrary",)),
        name="combine",
    )(dest_flat, dest_flat, y_slots, x, gate_t)


def _mixer(x2, batch, layer, norm_w, w_in, q_norm_w, k_norm_w, conv_w, a_log, dt_bias, out_norm_w, w_o,
           norm2_w, w_group, b_group, w_expert, b_expert, *, tm, tn, tq, tk, hp, tt, tm_out):
    T, D = x2.shape
    S = T // batch
    n_gdn = a_log.shape[0]
    gdn_width = n_gdn * HEAD_DIM
    n_main = w_in.shape[2] - 2 * n_gdn
    sb_width = (n_main - 4 * gdn_width) // 3
    n_sb = sb_width // HEAD_DIM
    reps = tn // HEAD_DIM
    qk_w = jnp.stack([jnp.tile(q_norm_w, reps) * (1.0 / math.sqrt(HEAD_DIM)), jnp.tile(k_norm_w, reps)])
    proj, g, beta = _inproj(x2, norm_w.reshape(1, D), w_in, qk_w,
                            a_log.reshape(-1, 1), dt_bias.reshape(-1, 1), layer=layer, sb_width=sb_width,
                            n_main=n_main, tm=tm, tn=tn)
    proj3 = proj.reshape(batch, S, n_main)
    o_sb = _sb_attention(proj3, n_heads=n_sb, tq=tq, tk=tk, hp=hp)
    o_gdn = _gdn(proj3, conv_w, g, beta, out_norm_w.reshape(1, HEAD_DIM), n_heads=n_gdn,
                 col0=3 * sb_width, tt=tt)
    tail = ROUTER_ROWS - 8 - N_EXPERTS
    wr_t = jnp.concatenate([w_group, jnp.zeros((D, 8 - N_GROUPS), F32), w_expert,
                            jnp.zeros((D, tail), F32)], axis=1).T
    br = jnp.concatenate([b_group, jnp.zeros((8 - N_GROUPS,), F32), b_expert,
                          jnp.zeros((tail,), F32)]).reshape(-1, 1)
    return _outproj_router(o_sb.reshape(T, sb_width), o_gdn.reshape(T, gdn_width), w_o.astype(BF16), x2,
                           norm2_w.reshape(1, D), wr_t, br, tm=tm_out)


def _moe(x2, h, eid, gate, layer, w1, w3, w2, *, tl, tt):
    T, D = x2.shape
    R = D // PACK_LANES
    n_blocks = (2 * T + MOE_BLOCK - 1) // MOE_BLOCK + N_EXPERTS
    n_blk_pad = ((n_blocks + 127) // 128) * 128
    dest, blk = _plan(eid, tl=tl, n_blk_pad=n_blk_pad)
    dest_flat = dest.T.reshape(T // tt, 1, 2 * tt)
    x_slots = _scatter_rows(dest_flat, blk, h, cap=n_blocks * MOE_BLOCK, tt=tt, R=R)
    y_slots = _experts(blk, x_slots, w1, w3, w2, layer=layer, R=R)
    return _combine(dest_flat, y_slots, x2, gate.T, tt=tt, R=R)


def _forward(x, norm1_w, w_in, sb_q_norm_w, sb_k_norm_w, gdn_conv_w, gdn_a_log, gdn_dt_bias,
             gdn_out_norm_w, w_o, norm2_w, w_group, b_group, w_expert, b_expert, w1, w3, w2, *, tiles):
    batch, S, D = x.shape
    x2 = x.reshape(batch * S, D)
    w_in = w_in.astype(BF16)
    for l in range(norm1_w.shape[0]):
        x2, h, eid, gate = _mixer(x2, batch, l, norm1_w[l], w_in, sb_q_norm_w[l], sb_k_norm_w[l],
                                  gdn_conv_w[l], gdn_a_log[l], gdn_dt_bias[l], gdn_out_norm_w[l], w_o[l],
                                  norm2_w[l], w_group[l], b_group[l], w_expert[l], b_expert[l],
                                  **tiles["mixer"])
        x2 = _moe(x2, h, eid, gate, l, w1, w3, w2, **tiles["moe"])
    return x2.reshape(batch, S, D)


_TILES = {
    "mixer": dict(tm=1024, tn=1024, tq=256, tk=256, hp=4, tt=512, tm_out=512),
    "moe": dict(tl=512, tt=256),
}


def kernel(x, norm1_w, w_in, sb_q_norm_w, sb_k_norm_w, gdn_conv_w, gdn_a_log, gdn_dt_bias, gdn_out_norm_w, w_o, norm2_w, w_group, b_group, w_expert, b_expert, w1, w3, w2):
    return _forward(x, norm1_w, w_in, sb_q_norm_w, sb_k_norm_w, gdn_conv_w, gdn_a_log, gdn_dt_bias,
                    gdn_out_norm_w, w_o, norm2_w, w_group, b_group, w_expert, b_expert, w1, w3, w2,
                    tiles=_TILES)
```

```python
import functools
import math

import jax
import jax.numpy as jnp
from jax import lax
from jax.experimental import pallas as pl
from jax.experimental.pallas import tpu as pltpu

F32 = jnp.float32
BF16 = jnp.bfloat16
I32 = jnp.int32

HEAD_DIM = 128
GDN_CONV = 4
GDN_CHUNK = 64
N_GROUPS = 4
EXPERTS_PER_GROUP = 8
N_EXPERTS = N_GROUPS * EXPERTS_PER_GROUP
MOE_BLOCK = 512
RMS_EPS = 1e-6
EXP_UNDERFLOW = -104.0
ROUTER_ROWS = 48
VMEM_LIMIT = 56 * 1024 * 1024


def _cparams(sem, vmem=VMEM_LIMIT):
    return pltpu.CompilerParams(dimension_semantics=sem, vmem_limit_bytes=vmem)


def _dot(a, b, dims=(((1,), (0,)), ((), ()))):
    return lax.dot_general(a, b, dims, preferred_element_type=F32)


_NN = (((1,), (0,)), ((), ()))
_NT = (((1,), (1,)), ((), ()))
_TN = (((0,), (0,)), ((), ()))


def _split(a):
    hi = a.astype(BF16)
    lo = (a - hi.astype(F32)).astype(BF16)
    return hi, lo


def _dot3(a, b, dims=(((1,), (0,)), ((), ()))):
    ah, al = _split(a)
    bh, bl = _split(b)
    return _dot(ah, bh, dims) + (_dot(ah, bl, dims) + _dot(al, bh, dims))


U32 = jnp.uint32
PACK_LANES = 2 * HEAD_DIM


def _pack_words(a):
    half = a.shape[1] // 2
    bits = lax.bitcast_convert_type(a.astype(BF16).astype(F32), U32)
    return [(bits[:, half + s * HEAD_DIM:half + (s + 1) * HEAD_DIM] & jnp.uint32(0xFFFF0000))
            | (bits[:, s * HEAD_DIM:(s + 1) * HEAD_DIM] >> 16) for s in range(half // HEAD_DIM)]


def _unpack_words(words):
    lo = [lax.bitcast_convert_type(w << 16, F32) for w in words]
    hi = [lax.bitcast_convert_type(w & jnp.uint32(0xFFFF0000), F32) for w in words]
    return lo, hi


def _softplus(x):
    return jnp.maximum(x, 0.0) + jnp.log(1.0 + jnp.exp(-jnp.abs(x)))


def _sigmoid(x):
    return 1.0 / (1.0 + jnp.exp(-x))


def _inproj_kernel(x_ref, nw_ref, w_ref, wab_ref, qkw_ref, alog_ref, dt_ref,
                   proj_ref, g_ref, beta_ref, h_ref, *, n_qk_tiles, n_heads, rows):
    j = pl.program_id(1)
    tm = x_ref.shape[0]

    @pl.when(j == 0)
    def _():
        def norm_rows(r, carry):
            rs = pl.multiple_of(r * rows, rows)
            x = x_ref[pl.ds(rs, rows), :]
            ms = jnp.mean(x * x, axis=-1, keepdims=True)
            h = x * lax.rsqrt(ms + RMS_EPS) * nw_ref[...]
            h_ref[pl.ds(rs, rows), :] = h.astype(BF16)
            return carry
        lax.fori_loop(0, tm // rows, norm_rows, 0)
        ab = _dot(wab_ref[0].astype(F32).T.astype(BF16), h_ref[...], _NT)
        ga = ab[0:n_heads]
        gb = ab[n_heads:2 * n_heads]
        g_ref[...] = -jnp.exp(alog_ref[...]) * _softplus(ga + dt_ref[...])
        beta_ref[...] = _sigmoid(gb)

    acc = _dot(h_ref[...], w_ref[0])

    is_qk = j < n_qk_tiles
    wsel = jnp.where(j < n_qk_tiles // 2, qkw_ref[0:1, :], qkw_ref[1:2, :])
    for hh in range(acc.shape[1] // HEAD_DIM):
        sl = slice(hh * HEAD_DIM, (hh + 1) * HEAD_DIM)
        a = acc[:, sl]
        ms = jnp.mean(a * a, axis=-1, keepdims=True)
        normed = a * lax.rsqrt(ms + RMS_EPS) * wsel[:, sl]
        proj_ref[:, sl] = jnp.where(is_qk, normed, a).astype(BF16)


def _inproj(x, norm_w, w_main, qk_w, a_log, dt_bias, *, layer, sb_width, n_main, tm, tn):
    T, D = x.shape
    assert n_main % HEAD_DIM == 0 and w_main.shape[2] == n_main + 2 * a_log.shape[0]
    N = n_main
    n_heads = a_log.shape[0]
    n_qk_tiles = 2 * sb_width // tn
    kern = functools.partial(_inproj_kernel, n_qk_tiles=n_qk_tiles, n_heads=n_heads,
                             rows=min(tm, 256))
    return pl.pallas_call(
        kern,
        grid=(T // tm, N // tn),
        in_specs=[
            pl.BlockSpec((tm, D), lambda i, j: (i, 0)),
            pl.BlockSpec((1, D), lambda i, j: (0, 0)),
            pl.BlockSpec((1, D, tn), lambda i, j: (layer, 0, j)),
            pl.BlockSpec((1, D, HEAD_DIM), lambda i, j: (layer, 0, n_main // HEAD_DIM)),
            pl.BlockSpec((2, tn), lambda i, j: (0, 0)),
            pl.BlockSpec((n_heads, 1), lambda i, j: (0, 0)),
            pl.BlockSpec((n_heads, 1), lambda i, j: (0, 0)),
        ],
        out_specs=[
            pl.BlockSpec((tm, tn), lambda i, j: (i, j)),
            pl.BlockSpec((n_heads, tm), lambda i, j: (0, i)),
            pl.BlockSpec((n_heads, tm), lambda i, j: (0, i)),
        ],
        out_shape=[
            jax.ShapeDtypeStruct((T, N), BF16),
            jax.ShapeDtypeStruct((n_heads, T), F32),
            jax.ShapeDtypeStruct((n_heads, T), F32),
        ],
        scratch_shapes=[pltpu.VMEM((tm, D), BF16)],
        compiler_params=_cparams(("arbitrary", "arbitrary")),
        name="inproj",
    )(x, norm_w, w_main, w_main, qk_w, a_log, dt_bias)


def _sb_kernel(q_ref, k_ref, v_ref, o_ref, *, tq, tk, hp):
    i = pl.program_id(2)
    heads = range(hp)
    hs = [slice(h * HEAD_DIM, (h + 1) * HEAD_DIM) for h in heads]
    q = [q_ref[0, :, hs[h]] for h in heads]
    row = lax.broadcasted_iota(I32, (tk, tk), 0)
    col = lax.broadcasted_iota(I32, (tk, tk), 1)
    upper_incl = (row >= col).astype(BF16)
    rel = lax.broadcasted_iota(I32, (tq, tk), 1) - lax.broadcasted_iota(I32, (tq, tk), 0)

    def block(ks, carry, masked):
        acc, c = carry
        z = [_dot(q[h], k_ref[0, pl.ds(ks, tk), hs[h]], _NT) for h in heads]
        log_1mb = [-_softplus(z[h]) for h in heads]
        if masked:
            causal = rel + (ks - i * tq) < 0
            log_1mb = [jnp.where(causal, log_1mb[h], 0.0) for h in heads]
        parts = [_split(log_1mb[h]) for h in heads]
        incl = [_dot(parts[h][0], upper_incl) + _dot(parts[h][1], upper_incl) for h in heads]
        w = [jnp.exp(z[h] + incl[h] + c[h]) for h in heads]
        if masked:
            w = [jnp.where(causal, w[h], 0.0) for h in heads]
        acc = [acc[h] + _dot(_bf(w[h]), v_ref[0, pl.ds(ks, tk), hs[h]]) for h in heads]
        c = [c[h] + incl[h][:, 0:1] for h in heads]
        return acc, c

    carry = ([jnp.zeros((tq, HEAD_DIM), F32) for _ in heads], [jnp.zeros((tq, 1), F32) for _ in heads])
    n_diag = tq // tk
    for d in range(n_diag - 1, -1, -1):
        carry = block(pl.multiple_of(i * tq + d * tk, tk), carry, True)

    def c_max(c):
        m = c[0]
        for h in heads[1:]:
            m = jnp.maximum(m, c[h])
        return jnp.max(m)

    def cond(state):
        jj, live, _ = state
        return jnp.logical_and(jj < i * n_diag, live)

    def body(state):
        jj, _, carry = state
        carry = block(pl.multiple_of((i * n_diag - 1 - jj) * tk, tk), carry, False)
        return jj + 1, c_max(carry[1]) > EXP_UNDERFLOW, carry

    _, _, (acc, _) = lax.while_loop(cond, body, (jnp.int32(0), c_max(carry[1]) > EXP_UNDERFLOW, carry))
    for h in heads:
        o_ref[0, :, hs[h]] = acc[h].astype(o_ref.dtype)


def _sb_attention(proj3, *, n_heads, tq, tk, hp):
    B, S, _ = proj3.shape
    kern = functools.partial(_sb_kernel, tq=tq, tk=tk, hp=hp)
    ng = n_heads // hp
    wd = hp * HEAD_DIM
    return pl.pallas_call(
        kern,
        grid=(B, ng, S // tq),
        in_specs=[
            pl.BlockSpec((1, tq, wd), lambda b, h, i: (b, i, h)),
            pl.BlockSpec((1, S, wd), lambda b, h, i: (b, 0, ng + h)),
            pl.BlockSpec((1, S, wd), lambda b, h, i: (b, 0, 2 * ng + h)),
        ],
        out_specs=pl.BlockSpec((1, tq, wd), lambda b, h, i: (b, i, h)),
        out_shape=jax.ShapeDtypeStruct((B, S, n_heads * HEAD_DIM), BF16),
        compiler_params=_cparams(("arbitrary", "arbitrary", "arbitrary")),
        name="sb_attention",
    )(proj3, proj3, proj3)


def _silu(x):
    return x * _sigmoid(x)


SUB = 2 * GDN_CHUNK


def _bf(a):
    return a.astype(BF16)


def _exact_parts(a):
    p1 = a.astype(BF16)
    r1 = a - p1.astype(F32)
    p2 = r1.astype(BF16)
    p3 = (r1 - p2.astype(F32)).astype(BF16)
    return p1, p2, p3


def _pair_dots(a, b, nt=False):
    return [_dot(x, y, _NT if nt else _NN) for x, y in zip(a, b)]


def _gdn_kernel(xq_ref, xk_ref, xv_ref, z_ref, cw_ref, g_ref, b_ref, ow_ref,
                o_ref, xs_ref, qkv_ref, state_ref, *, tt, n_heads):
    t = pl.program_id(1)
    C = GDN_CHUNK
    HALO = 8
    W = n_heads * HEAD_DIM
    heads = range(n_heads)

    @pl.when(t == 0)
    def _():
        xs_ref[:, 0:HALO, :] = jnp.zeros((3, HALO, W), F32)
        state_ref[...] = jnp.zeros_like(state_ref)

    @pl.when(t > 0)
    def _():
        xs_ref[:, 0:HALO, :] = xs_ref[:, tt:tt + HALO, :]

    for idx, x_ref in enumerate((xq_ref, xk_ref, xv_ref)):
        for h in heads:
            cs = slice(h * HEAD_DIM, (h + 1) * HEAD_DIM)
            xs_ref[idx, HALO:, cs] = x_ref[0, :, cs].astype(F32)
            acc = None
            for kk in range(GDN_CONV):
                wrow = cw_ref[kk:kk + 1, idx * W + h * HEAD_DIM: idx * W + (h + 1) * HEAD_DIM]
                term = xs_ref[idx, pl.ds(HALO - GDN_CONV + 1 + kk, tt), cs] * wrow
                acc = term if acc is None else acc + term
            a = _silu(acc)
            if idx < 2:
                inv = lax.rsqrt(jnp.sum(a * a, axis=-1, keepdims=True) + RMS_EPS)
                a = a * (inv * (1.0 / math.sqrt(HEAD_DIM)) if idx == 0 else inv)
            qkv_ref[idx, :, cs] = a

    ri = lax.broadcasted_iota(I32, (SUB, SUB), 0)
    ci = lax.broadcasted_iota(I32, (SUB, SUB), 1)
    same = (ri >> 6) == (ci >> 6)
    lower_incl = jnp.logical_and(same, ri >= ci)
    lower_strict = jnp.logical_and(same, ri > ci)
    blk16 = (ri >> 4) == (ci >> 4)
    blk32 = (ri >> 5) == (ci >> 5)
    only32 = jnp.logical_and(blk32, jnp.logical_not(blk16))
    only64 = jnp.logical_and(same, jnp.logical_not(blk32))
    eye_f = (ri == ci).astype(F32)
    eye_b = (ri == ci).astype(BF16)
    cum_b = jnp.logical_and(same, ri <= ci).astype(BF16)
    first = lax.broadcasted_iota(I32, (SUB, 1), 0) < C

    def sub_tile(s, carry):
        r0 = pl.multiple_of(s * SUB, SUB)
        g_rows = g_ref[:, pl.ds(r0, SUB)]
        b_rows = b_ref[:, pl.ds(r0, SUB)]
        gp = _exact_parts(g_rows)
        gc_rows = _dot(gp[0], cum_b) + _dot(gp[1], cum_b) + _dot(gp[2], cum_b)
        sp = _exact_parts(jnp.concatenate([gc_rows, b_rows], axis=0))
        cols = _dot(eye_b, sp[0], _NT) + _dot(eye_b, sp[1], _NT) + _dot(eye_b, sp[2], _NT)

        q, k, v, beta, decay, egc, kdec, eg_last = [], [], [], [], [], [], [], []
        for h in heads:
            cs = slice(h * HEAD_DIM, (h + 1) * HEAD_DIM)
            q.append(qkv_ref[0, pl.ds(r0, SUB), cs])
            k.append(qkv_ref[1, pl.ds(r0, SUB), cs])
            v.append(qkv_ref[2, pl.ds(r0, SUB), cs])
            gc_col = cols[:, h:h + 1]
            beta.append(cols[:, n_heads + h:n_heads + h + 1])
            decay.append(jnp.where(lower_incl, jnp.exp(jnp.minimum(gc_col - gc_rows[h:h + 1, :], 0.0)), 0.0))
            egc.append(jnp.exp(gc_col))
            g_last = jnp.where(first, gc_col[C - 1:C, :], gc_col[SUB - 1:SUB, :])
            kdec.append(_bf(k[h] * jnp.exp(g_last - gc_col)))
            eg_last.append((jnp.exp(gc_col[C - 1:C, :]), jnp.exp(gc_col[SUB - 1:SUB, :])))

        def bfl(xs):
            return [_bf(x) for x in xs]

        kb = [k[h] * beta[h] for h in heads]
        k_b = bfl(k)
        kk = _pair_dots(bfl(kb), k_b, nt=True)
        lmat = [jnp.where(lower_strict, kk[h] * decay[h], 0.0) for h in heads]
        d16 = [jnp.where(blk16, lmat[h], 0.0) for h in heads]
        d16_b = bfl(d16)
        p2 = bfl(_pair_dots(d16_b, d16_b))
        p4 = bfl(_pair_dots(p2, p2))
        p8 = bfl(_pair_dots(p4, p4))
        x0 = [eye_f - d16[h] for h in heads]
        for p in (p2, p4, p8):
            step = _pair_dots(bfl(x0), p)
            x0 = [x0[h] + step[h] for h in heads]
        x0_b = bfl(x0)
        y1 = bfl(_pair_dots(x0_b, bfl([jnp.where(only32, lmat[h], 0.0) for h in heads])))
        step = _pair_dots(y1, x0_b)
        x1 = [x0[h] - step[h] for h in heads]
        x1_b = bfl(x1)
        y2 = bfl(_pair_dots(x1_b, bfl([jnp.where(only64, lmat[h], 0.0) for h in heads])))
        step = _pair_dots(y2, x1_b)
        t_b = bfl([x1[h] - step[h] for h in heads])
        uw = _pair_dots(t_b, bfl([jnp.concatenate([v[h] * beta[h], kb[h] * egc[h]], axis=1) for h in heads]))
        u = [uw[h][:, :HEAD_DIM] for h in heads]
        w = [uw[h][:, HEAD_DIM:] for h in heads]
        qk = _pair_dots(bfl(q), k_b, nt=True)
        attn = bfl([jnp.where(lower_incl, qk[h] * decay[h], 0.0) for h in heads])
        qg = [q[h] * egc[h] for h in heads]

        state = [state_ref[h] for h in heads]
        zeros = jnp.zeros((C, HEAD_DIM), F32)
        for c in range(2):
            rows = slice(c * C, (c + 1) * C)
            wq = bfl([jnp.concatenate([w[h][rows], qg[h][rows]], axis=0) for h in heads])
            r = _pair_dots(wq, bfl(state))
            v_new = [u[h][rows] - r[h][0:C] for h in heads]
            v_pad = bfl([jnp.concatenate([v_new[h], zeros] if c == 0 else [zeros, v_new[h]], axis=0)
                         for h in heads])
            av = _pair_dots([attn[h][rows] for h in heads], v_pad)
            o = [r[h][C:SUB] + av[h] for h in heads]
            state = [state[h] * eg_last[h][c] + _dot(kdec[h][rows], _bf(v_new[h]), _TN) for h in heads]
            for h in heads:
                cs = slice(h * HEAD_DIM, (h + 1) * HEAD_DIM)
                ms = jnp.mean(o[h] * o[h], axis=-1, keepdims=True)
                zc = z_ref[0, pl.ds(r0 + c * C, C), cs].astype(F32)
                o_ref[0, pl.ds(r0 + c * C, C), cs] = (
                    o[h] * lax.rsqrt(ms + RMS_EPS) * ow_ref[...] * _silu(zc)).astype(o_ref.dtype)
        for h in heads:
            state_ref[h] = state[h]
        return carry

    lax.fori_loop(0, tt // SUB, sub_tile, 0)


def _gdn(proj3, conv_w, g, beta, out_norm_w, *, n_heads, col0, tt):
    B, S, _ = proj3.shape
    W = n_heads * HEAD_DIM
    kern = functools.partial(_gdn_kernel, tt=tt, n_heads=n_heads)
    cb = col0 // W
    nt = S // tt
    xspec = lambda off: pl.BlockSpec((1, tt, W), lambda b, t: (b, t, cb + off))
    gspec = pl.BlockSpec((n_heads, tt), lambda b, t: (0, b * nt + t))
    return pl.pallas_call(
        kern,
        grid=(B, nt),
        in_specs=[xspec(0), xspec(1), xspec(2), xspec(3),
                  pl.BlockSpec((GDN_CONV, 3 * W), lambda b, t: (0, 0)),
                  gspec, gspec, pl.BlockSpec((1, HEAD_DIM), lambda b, t: (0, 0))],
        out_specs=pl.BlockSpec((1, tt, W), lambda b, t: (b, t, 0)),
        out_shape=jax.ShapeDtypeStruct((B, S, W), BF16),
        scratch_shapes=[pltpu.VMEM((3, tt + 8, W), F32), pltpu.VMEM((3, tt, W), F32),
                        pltpu.VMEM((n_heads, HEAD_DIM, HEAD_DIM), F32)],
        compiler_params=_cparams(("arbitrary", "arbitrary")),
        name="gdn",
    )(proj3, proj3, proj3, proj3, conv_w, g, beta, out_norm_w)


def _outproj_router_kernel(a_ref, b_ref, wa_ref, wb_ref, x_ref, nw_ref, wr_ref, br_ref,
                           o_ref, h_ref, eid_ref, gate_ref):
    x = x_ref[...] + _dot(a_ref[...], wa_ref[...]) + _dot(b_ref[...], wb_ref[...])
    o_ref[...] = x
    tm = x.shape[0]
    ms = jnp.mean(x * x, axis=-1, keepdims=True)
    h = x * lax.rsqrt(ms + RMS_EPS) * nw_ref[...]
    words = _pack_words(h)
    for s, word in enumerate(words):
        h_ref[pl.ds(s, tm, stride=len(words)), :] = word
    logits = _dot3(wr_ref[...], h, _NT) + br_ref[...]
    best = logits[0:1]
    gidx = jnp.zeros((1, tm), I32)
    for g in range(1, N_GROUPS):
        better = logits[g:g + 1] > best
        gidx = jnp.where(better, g, gidx)
        best = jnp.where(better, logits[g:g + 1], best)
    gsum = jnp.zeros((1, tm), F32)
    for g in range(N_GROUPS):
        gsum = gsum + jnp.exp(logits[g:g + 1] - best)
    group_gate = 1.0 / gsum
    E = EXPERTS_PER_GROUP
    in_group = jnp.zeros((E, tm), F32)
    for g in range(N_GROUPS):
        in_group = jnp.where(gidx == g, logits[8 + g * E:8 + (g + 1) * E], in_group)
    sub = lax.broadcasted_iota(I32, (E, tm), 0)
    m1 = jnp.max(in_group, axis=0, keepdims=True)
    i1 = jnp.min(jnp.where(in_group == m1, sub, E), axis=0, keepdims=True)
    rest = jnp.where(sub == i1, -jnp.inf, in_group)
    m2 = jnp.max(rest, axis=0, keepdims=True)
    i2 = jnp.min(jnp.where(rest == m2, sub, E), axis=0, keepdims=True)
    e2 = jnp.exp(m2 - m1)
    inv = group_gate / (1.0 + e2)
    eid_ref[0:1, :] = gidx * E + i1
    eid_ref[1:2, :] = gidx * E + i2
    gate_ref[0:1, :] = inv
    gate_ref[1:2, :] = inv * e2


def _outproj_router(o_sb, o_gdn, w_o, x, norm_w, wr_t, br, *, tm):
    T, D = x.shape
    R = D // PACK_LANES
    wa, wb = o_sb.shape[1], o_gdn.shape[1]
    assert wa == wb and w_o.shape == (wa + wb, D)
    return pl.pallas_call(
        _outproj_router_kernel,
        grid=(T // tm,),
        in_specs=[
            pl.BlockSpec((tm, wa), lambda i: (i, 0)),
            pl.BlockSpec((tm, wb), lambda i: (i, 0)),
            pl.BlockSpec((wa, D), lambda i: (0, 0)),
            pl.BlockSpec((wb, D), lambda i: (1, 0)),
            pl.BlockSpec((tm, D), lambda i: (i, 0)),
            pl.BlockSpec((1, D), lambda i: (0, 0)),
            pl.BlockSpec((ROUTER_ROWS, D), lambda i: (0, 0)),
            pl.BlockSpec((ROUTER_ROWS, 1), lambda i: (0, 0)),
        ],
        out_specs=[
            pl.BlockSpec((tm, D), lambda i: (i, 0)),
            pl.BlockSpec((tm * R, HEAD_DIM), lambda i: (i, 0)),
            pl.BlockSpec((2, tm), lambda i: (0, i)),
            pl.BlockSpec((2, tm), lambda i: (0, i)),
        ],
        out_shape=[
            jax.ShapeDtypeStruct((T, D), F32),
            jax.ShapeDtypeStruct((T * R, HEAD_DIM), U32),
            jax.ShapeDtypeStruct((2, T), I32),
            jax.ShapeDtypeStruct((2, T), F32),
        ],
        compiler_params=_cparams(("arbitrary",)),
        name="outproj_router",
    )(o_sb, o_gdn, w_o, w_o, x, norm_w, wr_t, br)


def _plan_kernel(eid_ref, dest_ref, blk_ref, cnt_ref, *, tl, n_blk_pad):
    p = pl.program_id(0)
    i = pl.program_id(1)
    NE = N_EXPERTS
    e0 = eid_ref[0:1, :]
    e1 = eid_ref[1:2, :]
    sub = lax.broadcasted_iota(I32, (NE, tl), 0)
    hot0 = sub == e0
    hot1 = sub == e1
    onehot = jnp.logical_or(hot0, hot1).astype(BF16)
    ones = jnp.ones((tl, HEAD_DIM), BF16)

    @pl.when(jnp.logical_and(p == 0, i == 0))
    def _():
        cnt_ref[0] = jnp.zeros((NE, HEAD_DIM), F32)

    @pl.when(p == 0)
    def _():
        cnt_ref[0] += _dot(onehot, ones)

    @pl.when(jnp.logical_and(p == 1, i == 0))
    def _():
        cnt = cnt_ref[0]
        padded = jnp.floor((cnt + (MOE_BLOCK - 1)) * (1.0 / MOE_BLOCK)) * MOE_BLOCK
        er = lax.broadcasted_iota(I32, (NE, NE), 0)
        ec = lax.broadcasted_iota(I32, (NE, NE), 1)
        start = _dot3((ec < er).astype(F32), padded)
        cnt_ref[1] = start
        end_col = (start + padded)[:, 0:1]
        pos = (lax.broadcasted_iota(I32, (NE, n_blk_pad), 1) * MOE_BLOCK).astype(F32)
        n_before = jnp.sum((end_col <= pos).astype(I32), axis=0, keepdims=True)
        blk_ref[0:1, :] = jnp.minimum(n_before, NE - 1)
        total = jnp.max(end_col, axis=0, keepdims=True)
        blk_ref[1:2, :] = jnp.broadcast_to((total * (1.0 / MOE_BLOCK)).astype(I32), (1, n_blk_pad))
        cnt_ref[0] = jnp.zeros((NE, HEAD_DIM), F32)

    @pl.when(p == 1)
    def _():
        r = lax.broadcasted_iota(I32, (tl, tl), 0)
        c = lax.broadcasted_iota(I32, (tl, tl), 1)
        before = (r < c).astype(BF16)
        run = cnt_ref[0][:, 0:1] + cnt_ref[1][:, 0:1]
        slot = _dot(onehot, before) + run
        dest_ref[0:1, :] = jnp.sum(jnp.where(hot0, slot, 0.0), axis=0, keepdims=True).astype(I32)
        dest_ref[1:2, :] = jnp.sum(jnp.where(hot1, slot, 0.0), axis=0, keepdims=True).astype(I32)
        cnt_ref[0] += _dot(onehot, ones)


def _plan(eid, *, tl, n_blk_pad):
    T = eid.shape[1]
    kern = functools.partial(_plan_kernel, tl=tl, n_blk_pad=n_blk_pad)
    return pl.pallas_call(
        kern,
        grid=(2, T // tl),
        in_specs=[pl.BlockSpec((2, tl), lambda p, i: (0, i))],
        out_specs=[
            pl.BlockSpec((2, tl), lambda p, i: (0, i * p)),
            pl.BlockSpec((2, n_blk_pad), lambda p, i: (0, 0)),
        ],
        out_shape=[
            jax.ShapeDtypeStruct((2, T), I32),
            jax.ShapeDtypeStruct((2, n_blk_pad), I32),
        ],
        scratch_shapes=[pltpu.VMEM((2, N_EXPERTS, HEAD_DIM), F32)],
        compiler_params=_cparams(("arbitrary", "arbitrary")),
        name="plan",
    )(eid)


def _scatter_kernel(dest_ref, blk_ref, h_ref, xs_ref, zero_ref, sem, zsem, *, tt, n_blocks, R):
    i = pl.program_id(0)
    blk_rows = MOE_BLOCK * R

    @pl.when(i == 0)
    def _():
        zero_ref[...] = jnp.zeros_like(zero_ref)
        n_valid = blk_ref[1, 0]

        def clear(b, n):
            last = blk_ref[0, b] != blk_ref[0, jnp.minimum(b + 1, n_blocks - 1)]
            do = jnp.logical_or(b >= n_valid - 1, last)

            @pl.when(do)
            def _():
                pltpu.make_async_copy(zero_ref, xs_ref.at[pl.ds(pl.multiple_of(b * blk_rows, blk_rows), blk_rows)],
                                      zsem).start()
            return n + do.astype(I32)
        n_started = lax.fori_loop(0, n_blocks, clear, 0)

        def drain(b, carry):
            pltpu.make_async_copy(zero_ref, xs_ref.at[pl.ds(0, blk_rows)], zsem).wait()
            return carry
        lax.fori_loop(0, n_started, drain, 0)

    def issue(r, carry):
        src = h_ref.at[pl.ds(pl.multiple_of(r * R, R), R)]
        for kk in range(2):
            dst = xs_ref.at[pl.ds(pl.multiple_of(dest_ref[0, 0, 2 * r + kk] * R, R), R)]
            pltpu.make_async_copy(src, dst, sem).start(priority=kk)
        return carry
    lax.fori_loop(0, tt, issue, 0, unroll=4)

    for kk in range(2):
        pltpu.make_async_copy(h_ref, xs_ref.at[pl.ds(0, tt * R)], sem).wait()


def _scatter_rows(dest_flat, blk, h_packed, *, cap, tt, R):
    T = h_packed.shape[0] // R
    n_blocks = cap // MOE_BLOCK
    kern = functools.partial(_scatter_kernel, tt=tt, n_blocks=n_blocks, R=R)
    return pl.pallas_call(
        kern,
        grid=(T // tt,),
        in_specs=[
            pl.BlockSpec((1, 1, 2 * tt), lambda i: (i, 0, 0), memory_space=pltpu.SMEM),
            pl.BlockSpec(memory_space=pltpu.SMEM),
            pl.BlockSpec((tt * R, HEAD_DIM), lambda i: (i, 0)),
        ],
        out_specs=pl.BlockSpec(memory_space=pl.ANY),
        out_shape=jax.ShapeDtypeStruct((cap * R, HEAD_DIM), U32),
        scratch_shapes=[pltpu.VMEM((MOE_BLOCK * R, HEAD_DIM), U32), pltpu.SemaphoreType.DMA,
                        pltpu.SemaphoreType.DMA],
        compiler_params=_cparams(("arbitrary",)),
        name="scatter_rows",
    )(dest_flat, blk, h_packed)


def _expert_kernel(blk_ref, x_ref, w1_ref, w3_ref, w2_ref, y_ref, w1b_ref, w3b_ref, w2b_ref):
    b = pl.program_id(0)
    valid = b < blk_ref[1, 0]
    new_expert = jnp.logical_or(b == 0, blk_ref[0, b] != blk_ref[0, jnp.maximum(b - 1, 0)])

    @pl.when(jnp.logical_and(valid, new_expert))
    def _():
        w1b_ref[...] = w1_ref[0, 0].astype(BF16)
        w3b_ref[...] = w3_ref[0, 0].astype(BF16)
        w2b_ref[...] = w2_ref[0, 0].astype(BF16)

    @pl.when(valid)
    def _():
        R = x_ref.shape[0] // MOE_BLOCK
        lo, hi = _unpack_words([x_ref[pl.ds(s, MOE_BLOCK, stride=R), :] for s in range(R)])
        x = jnp.concatenate(lo + hi, axis=1).astype(BF16)
        hid = _silu(_dot(x, w1b_ref[...])) * _dot(x, w3b_ref[...])
        y = _dot(hid.astype(BF16), w2b_ref[...])
        for s, word in enumerate(_pack_words(y)):
            y_ref[pl.ds(s, MOE_BLOCK, stride=R), :] = word

    @pl.when(b >= blk_ref[1, 0])
    def _():
        y_ref[...] = jnp.zeros_like(y_ref)


def _experts(blk, x_slots, w1, w3, w2, *, layer, R):
    cap = x_slots.shape[0] // R
    n_blocks = cap // MOE_BLOCK
    D, F = w1.shape[2], w1.shape[3]

    def row_blk(b, blk_ref):
        return jnp.minimum(b, blk_ref[1, 0] - 1)

    def w_idx(b, blk_ref):
        return (layer, blk_ref[0, row_blk(b, blk_ref)], 0, 0)

    grid_spec = pltpu.PrefetchScalarGridSpec(
        num_scalar_prefetch=1,
        grid=(n_blocks,),
        in_specs=[
            pl.BlockSpec((MOE_BLOCK * R, HEAD_DIM), lambda b, blk_ref: (row_blk(b, blk_ref), 0)),
            pl.BlockSpec((1, 1, D, F), w_idx),
            pl.BlockSpec((1, 1, D, F), w_idx),
            pl.BlockSpec((1, 1, F, D), w_idx),
        ],
        out_specs=pl.BlockSpec((MOE_BLOCK * R, HEAD_DIM), lambda b, blk_ref: (b, 0)),
        scratch_shapes=[pltpu.VMEM((D, F), BF16), pltpu.VMEM((D, F), BF16), pltpu.VMEM((F, D), BF16)],
    )
    return pl.pallas_call(
        _expert_kernel,
        grid_spec=grid_spec,
        out_shape=jax.ShapeDtypeStruct((cap * R, HEAD_DIM), U32),
        compiler_params=_cparams(("arbitrary",)),
        name="experts",
    )(blk, x_slots, w1, w3, w2)


def _combine_kernel(dest_ref, dnext_ref, ys_ref, x_ref, gate_ref, o_ref, buf_ref, sems, *, tt, R):
    i = pl.program_id(0)
    slot = lax.rem(i, 2)

    def issue(d_ref, s):
        def one(r, carry):
            for kk in range(2):
                src = ys_ref.at[pl.ds(pl.multiple_of(d_ref[0, 0, 2 * r + kk] * R, R), R)]
                pltpu.make_async_copy(src, buf_ref.at[s, kk, pl.ds(pl.multiple_of(r * R, R), R)],
                                      sems.at[s]).start(priority=kk)
            return carry
        lax.fori_loop(0, tt, one, 0, unroll=4)

    @pl.when(i == 0)
    def _():
        issue(dest_ref, 0)

    @pl.when(i + 1 < pl.num_programs(0))
    def _():
        issue(dnext_ref, 1 - slot)

    for kk in range(2):
        pltpu.make_async_copy(ys_ref.at[pl.ds(0, tt * R)], buf_ref.at[slot, kk], sems.at[slot]).wait()
    g = gate_ref[...]
    half = R * HEAD_DIM
    for s in range(R):
        lo0, hi0 = _unpack_words([buf_ref[slot, 0, pl.ds(s, tt, stride=R), :]])
        lo1, hi1 = _unpack_words([buf_ref[slot, 1, pl.ds(s, tt, stride=R), :]])
        for off, y0, y1 in ((s * HEAD_DIM, lo0[0], lo1[0]), (half + s * HEAD_DIM, hi0[0], hi1[0])):
            cs = slice(off, off + HEAD_DIM)
            o_ref[:, cs] = x_ref[:, cs] + (y0 * g[:, 0:1] + y1 * g[:, 1:2])


def _combine(dest_flat, y_slots, x, gate_t, *, tt, R):
    T, D = x.shape
    n = T // tt
    kern = functools.partial(_combine_kernel, tt=tt, R=R)
    return pl.pallas_call(
        kern,
        grid=(n,),
        in_specs=[
            pl.BlockSpec((1, 1, 2 * tt), lambda i: (i, 0, 0), memory_space=pltpu.SMEM),
            pl.BlockSpec((1, 1, 2 * tt), lambda i: (jnp.minimum(i + 1, n - 1), 0, 0), memory_space=pltpu.SMEM),
            pl.BlockSpec(memory_space=pl.ANY),
            pl.BlockSpec((tt, D), lambda i: (i, 0)),
            pl.BlockSpec((tt, 2), lambda i: (i, 0)),
        ],
        out_specs=pl.BlockSpec((tt, D), lambda i: (i, 0)),
        out_shape=jax.ShapeDtypeStruct((T, D), F32),
        scratch_shapes=[pltpu.VMEM((2, 2, tt * R, HEAD_DIM), U32), pltpu.SemaphoreType.DMA((2,))],
        compiler_params=_cparams(("arbitrary",)),
        name="combine",
    )(dest_flat, dest_flat, y_slots, x, gate_t)


def _mixer(x2, batch, layer, norm_w, w_in, q_norm_w, k_norm_w, conv_w, a_log, dt_bias, out_norm_w, w_o,
           norm2_w, w_group, b_group, w_expert, b_expert, *, tm, tn, tq, tk, hp, tt, tm_out):
    T, D = x2.shape
    S = T // batch
    n_gdn = a_log.shape[0]
    gdn_width = n_gdn * HEAD_DIM
    n_main = w_in.shape[2] - 2 * n_gdn
    sb_width = (n_main - 4 * gdn_width) // 3
    n_sb = sb_width // HEAD_DIM
    reps = tn // HEAD_DIM
    qk_w = jnp.stack([jnp.tile(q_norm_w, reps) * (1.0 / math.sqrt(HEAD_DIM)), jnp.tile(k_norm_w, reps)])
    proj, g, beta = _inproj(x2, norm_w.reshape(1, D), w_in, qk_w,
                            a_log.reshape(-1, 1), dt_bias.reshape(-1, 1), layer=layer, sb_width=sb_width,
                            n_main=n_main, tm=tm, tn=tn)
    proj3 = proj.reshape(batch, S, n_main)
    o_sb = _sb_attention(proj3, n_heads=n_sb, tq=tq, tk=tk, hp=hp)
    o_gdn = _gdn(proj3, conv_w, g, beta, out_norm_w.reshape(1, HEAD_DIM), n_heads=n_gdn,
                 col0=3 * sb_width, tt=tt)
    tail = ROUTER_ROWS - 8 - N_EXPERTS
    wr_t = jnp.concatenate([w_group, jnp.zeros((D, 8 - N_GROUPS), F32), w_expert,
                            jnp.zeros((D, tail), F32)], axis=1).T
    br = jnp.concatenate([b_group, jnp.zeros((8 - N_GROUPS,), F32), b_expert,
                          jnp.zeros((tail,), F32)]).reshape(-1, 1)
    return _outproj_router(o_sb.reshape(T, sb_width), o_gdn.reshape(T, gdn_width), w_o.astype(BF16), x2,
                           norm2_w.reshape(1, D), wr_t, br, tm=tm_out)


def _moe(x2, h, eid, gate, layer, w1, w3, w2, *, tl, tt):
    T, D = x2.shape
    R = D // PACK_LANES
    n_blocks = (2 * T + MOE_BLOCK - 1) // MOE_BLOCK + N_EXPERTS
    n_blk_pad = ((n_blocks + 127) // 128) * 128
    dest, blk = _plan(eid, tl=tl, n_blk_pad=n_blk_pad)
    dest_flat = dest.T.reshape(T // tt, 1, 2 * tt)
    x_slots = _scatter_rows(dest_flat, blk, h, cap=n_blocks * MOE_BLOCK, tt=tt, R=R)
    y_slots = _experts(blk, x_slots, w1, w3, w2, layer=layer, R=R)
    return _combine(dest_flat, y_slots, x2, gate.T, tt=tt, R=R)


def _forward(x, norm1_w, w_in, sb_q_norm_w, sb_k_norm_w, gdn_conv_w, gdn_a_log, gdn_dt_bias,
             gdn_out_norm_w, w_o, norm2_w, w_group, b_group, w_expert, b_expert, w1, w3, w2, *, tiles):
    batch, S, D = x.shape
    x2 = x.reshape(batch * S, D)
    w_in = w_in.astype(BF16)
    for l in range(norm1_w.shape[0]):
        x2, h, eid, gate = _mixer(x2, batch, l, norm1_w[l], w_in, sb_q_norm_w[l], sb_k_norm_w[l],
                                  gdn_conv_w[l], gdn_a_log[l], gdn_dt_bias[l], gdn_out_norm_w[l], w_o[l],
                                  norm2_w[l], w_group[l], b_group[l], w_expert[l], b_expert[l],
                                  **tiles["mixer"])
        x2 = _moe(x2, h, eid, gate, l, w1, w3, w2, **tiles["moe"])
    return x2.reshape(batch, S, D)


_TILES = {
    "mixer": dict(tm=1024, tn=1024, tq=256, tk=256, hp=8, tt=512, tm_out=512),
    "moe": dict(tl=512, tt=512),
}


def kernel(x, norm1_w, w_in, sb_q_norm_w, sb_k_norm_w, gdn_conv_w, gdn_a_log, gdn_dt_bias, gdn_out_norm_w, w_o, norm2_w, w_group, b_group, w_expert, b_expert, w1, w3, w2):
    return _forward(x, norm1_w, w_in, sb_q_norm_w, sb_k_norm_w, gdn_conv_w, gdn_a_log, gdn_dt_bias,
                    gdn_out_norm_w, w_o, norm2_w, w_group, b_group, w_expert, b_expert, w1, w3, w2,
                    tiles=_TILES)
```

```python
import functools
import math

import jax
import jax.numpy as jnp
from jax import lax
from jax.experimental import pallas as pl
from jax.experimental.pallas import tpu as pltpu

F32 = jnp.float32
BF16 = jnp.bfloat16
I32 = jnp.int32

HEAD_DIM = 128
GDN_CONV = 4
GDN_CHUNK = 64
N_GROUPS = 4
EXPERTS_PER_GROUP = 8
N_EXPERTS = N_GROUPS * EXPERTS_PER_GROUP
MOE_BLOCK = 512
RMS_EPS = 1e-6
EXP_UNDERFLOW = -104.0
ROUTER_ROWS = 48
VMEM_LIMIT = 56 * 1024 * 1024


def _cparams(sem, vmem=VMEM_LIMIT):
    return pltpu.CompilerParams(dimension_semantics=sem, vmem_limit_bytes=vmem)


def _dot(a, b, dims=(((1,), (0,)), ((), ()))):
    return lax.dot_general(a, b, dims, preferred_element_type=F32)


_NN = (((1,), (0,)), ((), ()))
_NT = (((1,), (1,)), ((), ()))
_TN = (((0,), (0,)), ((), ()))


def _split(a):
    hi = a.astype(BF16)
    lo = (a - hi.astype(F32)).astype(BF16)
    return hi, lo


def _dot3(a, b, dims=(((1,), (0,)), ((), ()))):
    ah, al = _split(a)
    bh, bl = _split(b)
    return _dot(ah, bh, dims) + (_dot(ah, bl, dims) + _dot(al, bh, dims))


U32 = jnp.uint32
PACK_LANES = 2 * HEAD_DIM


def _pack_words(a):
    half = a.shape[1] // 2
    bits = lax.bitcast_convert_type(a.astype(BF16).astype(F32), U32)
    return [(bits[:, half + s * HEAD_DIM:half + (s + 1) * HEAD_DIM] & jnp.uint32(0xFFFF0000))
            | (bits[:, s * HEAD_DIM:(s + 1) * HEAD_DIM] >> 16) for s in range(half // HEAD_DIM)]


def _unpack_words(words):
    lo = [lax.bitcast_convert_type(w << 16, F32) for w in words]
    hi = [lax.bitcast_convert_type(w & jnp.uint32(0xFFFF0000), F32) for w in words]
    return lo, hi


def _softplus(x):
    return jnp.maximum(x, 0.0) + jnp.log(1.0 + jnp.exp(-jnp.abs(x)))


def _sigmoid(x):
    return 1.0 / (1.0 + jnp.exp(-x))


def _inproj_kernel(x_ref, nw_ref, w_ref, wab_ref, qkw_ref, alog_ref, dt_ref,
                   proj_ref, g_ref, beta_ref, h_ref, *, n_qk_tiles, n_heads, rows):
    j = pl.program_id(1)
    tm = x_ref.shape[0]

    @pl.when(j == 0)
    def _():
        def norm_rows(r, carry):
            rs = pl.multiple_of(r * rows, rows)
            x = x_ref[pl.ds(rs, rows), :]
            ms = jnp.mean(x * x, axis=-1, keepdims=True)
            h = x * lax.rsqrt(ms + RMS_EPS) * nw_ref[...]
            h_ref[pl.ds(rs, rows), :] = h.astype(BF16)
            return carry
        lax.fori_loop(0, tm // rows, norm_rows, 0)
        ab = _dot(wab_ref[0].astype(F32).T.astype(BF16), h_ref[...], _NT)
        ga = ab[0:n_heads]
        gb = ab[n_heads:2 * n_heads]
        g_ref[...] = -jnp.exp(alog_ref[...]) * _softplus(ga + dt_ref[...])
        beta_ref[...] = _sigmoid(gb)

    acc = _dot(h_ref[...], w_ref[0])

    is_qk = j < n_qk_tiles
    wsel = jnp.where(j < n_qk_tiles // 2, qkw_ref[0:1, :], qkw_ref[1:2, :])
    for hh in range(acc.shape[1] // HEAD_DIM):
        sl = slice(hh * HEAD_DIM, (hh + 1) * HEAD_DIM)
        a = acc[:, sl]
        ms = jnp.mean(a * a, axis=-1, keepdims=True)
        normed = a * lax.rsqrt(ms + RMS_EPS) * wsel[:, sl]
        proj_ref[:, sl] = jnp.where(is_qk, normed, a).astype(BF16)


def _inproj(x, norm_w, w_main, qk_w, a_log, dt_bias, *, layer, sb_width, n_main, tm, tn):
    T, D = x.shape
    assert n_main % HEAD_DIM == 0 and w_main.shape[2] == n_main + 2 * a_log.shape[0]
    N = n_main
    n_heads = a_log.shape[0]
    n_qk_tiles = 2 * sb_width // tn
    kern = functools.partial(_inproj_kernel, n_qk_tiles=n_qk_tiles, n_heads=n_heads,
                             rows=min(tm, 256))
    return pl.pallas_call(
        kern,
        grid=(T // tm, N // tn),
        in_specs=[
            pl.BlockSpec((tm, D), lambda i, j: (i, 0)),
            pl.BlockSpec((1, D), lambda i, j: (0, 0)),
            pl.BlockSpec((1, D, tn), lambda i, j: (layer, 0, j)),
            pl.BlockSpec((1, D, HEAD_DIM), lambda i, j: (layer, 0, n_main // HEAD_DIM)),
            pl.BlockSpec((2, tn), lambda i, j: (0, 0)),
            pl.BlockSpec((n_heads, 1), lambda i, j: (0, 0)),
            pl.BlockSpec((n_heads, 1), lambda i, j: (0, 0)),
        ],
        out_specs=[
            pl.BlockSpec((tm, tn), lambda i, j: (i, j)),
            pl.BlockSpec((n_heads, tm), lambda i, j: (0, i)),
            pl.BlockSpec((n_heads, tm), lambda i, j: (0, i)),
        ],
        out_shape=[
            jax.ShapeDtypeStruct((T, N), BF16),
            jax.ShapeDtypeStruct((n_heads, T), F32),
            jax.ShapeDtypeStruct((n_heads, T), F32),
        ],
        scratch_shapes=[pltpu.VMEM((tm, D), BF16)],
        compiler_params=_cparams(("arbitrary", "arbitrary")),
        name="inproj",
    )(x, norm_w, w_main, w_main, qk_w, a_log, dt_bias)


def _sb_kernel(q_ref, k_ref, v_ref, o_ref, *, tq, tk, hp):
    i = pl.program_id(2)
    heads = range(hp)
    hs = [slice(h * HEAD_DIM, (h + 1) * HEAD_DIM) for h in heads]
    q = [q_ref[0, :, hs[h]] for h in heads]
    row = lax.broadcasted_iota(I32, (tk, tk), 0)
    col = lax.broadcasted_iota(I32, (tk, tk), 1)
    upper_incl = (row >= col).astype(BF16)
    rel = lax.broadcasted_iota(I32, (tq, tk), 1) - lax.broadcasted_iota(I32, (tq, tk), 0)

    def block(ks, carry, masked):
        acc, c = carry
        z = [_dot(q[h], k_ref[0, pl.ds(ks, tk), hs[h]], _NT) for h in heads]
        log_1mb = [-_softplus(z[h]) for h in heads]
        if masked:
            causal = rel + (ks - i * tq) < 0
            log_1mb = [jnp.where(causal, log_1mb[h], 0.0) for h in heads]
        parts = [_split(log_1mb[h]) for h in heads]
        incl = [_dot(parts[h][0], upper_incl) + _dot(parts[h][1], upper_incl) for h in heads]
        w = [jnp.exp(z[h] + incl[h] + c[h]) for h in heads]
        if masked:
            w = [jnp.where(causal, w[h], 0.0) for h in heads]
        acc = [acc[h] + _dot(_bf(w[h]), v_ref[0, pl.ds(ks, tk), hs[h]]) for h in heads]
        c = [c[h] + incl[h][:, 0:1] for h in heads]
        return acc, c

    carry = ([jnp.zeros((tq, HEAD_DIM), F32) for _ in heads], [jnp.zeros((tq, 1), F32) for _ in heads])
    n_diag = tq // tk
    for d in range(n_diag - 1, -1, -1):
        carry = block(pl.multiple_of(i * tq + d * tk, tk), carry, True)

    def c_max(c):
        m = c[0]
        for h in heads[1:]:
            m = jnp.maximum(m, c[h])
        return jnp.max(m)

    def cond(state):
        jj, live, _ = state
        return jnp.logical_and(jj < i * n_diag, live)

    def body(state):
        jj, _, carry = state
        carry = block(pl.multiple_of((i * n_diag - 1 - jj) * tk, tk), carry, False)
        return jj + 1, c_max(carry[1]) > EXP_UNDERFLOW, carry

    _, _, (acc, _) = lax.while_loop(cond, body, (jnp.int32(0), c_max(carry[1]) > EXP_UNDERFLOW, carry))
    for h in heads:
        o_ref[0, :, hs[h]] = acc[h].astype(o_ref.dtype)


def _sb_attention(proj3, *, n_heads, tq, tk, hp):
    B, S, _ = proj3.shape
    kern = functools.partial(_sb_kernel, tq=tq, tk=tk, hp=hp)
    ng = n_heads // hp
    wd = hp * HEAD_DIM
    return pl.pallas_call(
        kern,
        grid=(B, ng, S // tq),
        in_specs=[
            pl.BlockSpec((1, tq, wd), lambda b, h, i: (b, i, h)),
            pl.BlockSpec((1, S, wd), lambda b, h, i: (b, 0, ng + h)),
            pl.BlockSpec((1, S, wd), lambda b, h, i: (b, 0, 2 * ng + h)),
        ],
        out_specs=pl.BlockSpec((1, tq, wd), lambda b, h, i: (b, i, h)),
        out_shape=jax.ShapeDtypeStruct((B, S, n_heads * HEAD_DIM), BF16),
        compiler_params=_cparams(("arbitrary", "arbitrary", "arbitrary")),
        name="sb_attention",
    )(proj3, proj3, proj3)


def _silu(x):
    return x * _sigmoid(x)


SUB = 2 * GDN_CHUNK


def _bf(a):
    return a.astype(BF16)


def _exact_parts(a):
    p1 = a.astype(BF16)
    r1 = a - p1.astype(F32)
    p2 = r1.astype(BF16)
    p3 = (r1 - p2.astype(F32)).astype(BF16)
    return p1, p2, p3


def _pair_dots(a, b, nt=False):
    return [_dot(x, y, _NT if nt else _NN) for x, y in zip(a, b)]


def _gdn_kernel(xq_ref, xk_ref, xv_ref, z_ref, cw_ref, g_ref, b_ref, ow_ref,
                o_ref, xs_ref, qkv_ref, state_ref, *, tt, n_heads):
    t = pl.program_id(1)
    C = GDN_CHUNK
    HALO = 8
    W = n_heads * HEAD_DIM
    heads = range(n_heads)

    @pl.when(t == 0)
    def _():
        xs_ref[:, 0:HALO, :] = jnp.zeros((3, HALO, W), F32)
        state_ref[...] = jnp.zeros_like(state_ref)

    @pl.when(t > 0)
    def _():
        xs_ref[:, 0:HALO, :] = xs_ref[:, tt:tt + HALO, :]

    for idx, x_ref in enumerate((xq_ref, xk_ref, xv_ref)):
        for h in heads:
            cs = slice(h * HEAD_DIM, (h + 1) * HEAD_DIM)
            xs_ref[idx, HALO:, cs] = x_ref[0, :, cs].astype(F32)
            acc = None
            for kk in range(GDN_CONV):
                wrow = cw_ref[kk:kk + 1, idx * W + h * HEAD_DIM: idx * W + (h + 1) * HEAD_DIM]
                term = xs_ref[idx, pl.ds(HALO - GDN_CONV + 1 + kk, tt), cs] * wrow
                acc = term if acc is None else acc + term
            a = _silu(acc)
            if idx < 2:
                inv = lax.rsqrt(jnp.sum(a * a, axis=-1, keepdims=True) + RMS_EPS)
                a = a * (inv * (1.0 / math.sqrt(HEAD_DIM)) if idx == 0 else inv)
            qkv_ref[idx, :, cs] = a

    ri = lax.broadcasted_iota(I32, (SUB, SUB), 0)
    ci = lax.broadcasted_iota(I32, (SUB, SUB), 1)
    same = (ri >> 6) == (ci >> 6)
    lower_incl = jnp.logical_and(same, ri >= ci)
    lower_strict = jnp.logical_and(same, ri > ci)
    blk16 = (ri >> 4) == (ci >> 4)
    blk32 = (ri >> 5) == (ci >> 5)
    only32 = jnp.logical_and(blk32, jnp.logical_not(blk16))
    only64 = jnp.logical_and(same, jnp.logical_not(blk32))
    eye_f = (ri == ci).astype(F32)
    eye_b = (ri == ci).astype(BF16)
    cum_b = jnp.logical_and(same, ri <= ci).astype(BF16)
    first = lax.broadcasted_iota(I32, (SUB, 1), 0) < C

    def sub_tile(s, carry):
        r0 = pl.multiple_of(s * SUB, SUB)
        g_rows = g_ref[:, pl.ds(r0, SUB)]
        b_rows = b_ref[:, pl.ds(r0, SUB)]
        gp = _exact_parts(g_rows)
        gc_rows = _dot(gp[0], cum_b) + _dot(gp[1], cum_b) + _dot(gp[2], cum_b)
        sp = _exact_parts(jnp.concatenate([gc_rows, b_rows], axis=0))
        cols = _dot(eye_b, sp[0], _NT) + _dot(eye_b, sp[1], _NT) + _dot(eye_b, sp[2], _NT)

        q, k, v, beta, decay, egc, kdec, eg_last = [], [], [], [], [], [], [], []
        for h in heads:
            cs = slice(h * HEAD_DIM, (h + 1) * HEAD_DIM)
            q.append(qkv_ref[0, pl.ds(r0, SUB), cs])
            k.append(qkv_ref[1, pl.ds(r0, SUB), cs])
            v.append(qkv_ref[2, pl.ds(r0, SUB), cs])
            gc_col = cols[:, h:h + 1]
            beta.append(cols[:, n_heads + h:n_heads + h + 1])
            decay.append(jnp.where(lower_incl, jnp.exp(jnp.minimum(gc_col - gc_rows[h:h + 1, :], 0.0)), 0.0))
            egc.append(jnp.exp(gc_col))
            g_last = jnp.where(first, gc_col[C - 1:C, :], gc_col[SUB - 1:SUB, :])
            kdec.append(_bf(k[h] * jnp.exp(g_last - gc_col)))
            eg_last.append((jnp.exp(gc_col[C - 1:C, :]), jnp.exp(gc_col[SUB - 1:SUB, :])))

        def bfl(xs):
            return [_bf(x) for x in xs]

        kb = [k[h] * beta[h] for h in heads]
        k_b = bfl(k)
        kk = _pair_dots(bfl(kb), k_b, nt=True)
        lmat = [jnp.where(lower_strict, kk[h] * decay[h], 0.0) for h in heads]
        d16 = [jnp.where(blk16, lmat[h], 0.0) for h in heads]
        d16_b = bfl(d16)
        p2 = bfl(_pair_dots(d16_b, d16_b))
        p4 = bfl(_pair_dots(p2, p2))
        p8 = bfl(_pair_dots(p4, p4))
        x0 = [eye_f - d16[h] for h in heads]
        for p in (p2, p4, p8):
            step = _pair_dots(bfl(x0), p)
            x0 = [x0[h] + step[h] for h in heads]
        x0_b = bfl(x0)
        y1 = bfl(_pair_dots(x0_b, bfl([jnp.where(only32, lmat[h], 0.0) for h in heads])))
        step = _pair_dots(y1, x0_b)
        x1 = [x0[h] - step[h] for h in heads]
        x1_b = bfl(x1)
        y2 = bfl(_pair_dots(x1_b, bfl([jnp.where(only64, lmat[h], 0.0) for h in heads])))
        step = _pair_dots(y2, x1_b)
        t_b = bfl([x1[h] - step[h] for h in heads])
        uw = _pair_dots(t_b, bfl([jnp.concatenate([v[h] * beta[h], kb[h] * egc[h]], axis=1) for h in heads]))
        u = [uw[h][:, :HEAD_DIM] for h in heads]
        w = [uw[h][:, HEAD_DIM:] for h in heads]
        qk = _pair_dots(bfl(q), k_b, nt=True)
        attn = bfl([jnp.where(lower_incl, qk[h] * decay[h], 0.0) for h in heads])
        qg = [q[h] * egc[h] for h in heads]

        state = [state_ref[h] for h in heads]
        zeros = jnp.zeros((C, HEAD_DIM), F32)
        for c in range(2):
            rows = slice(c * C, (c + 1) * C)
            wq = bfl([jnp.concatenate([w[h][rows], qg[h][rows]], axis=0) for h in heads])
            r = _pair_dots(wq, bfl(state))
            v_new = [u[h][rows] - r[h][0:C] for h in heads]
            v_pad = bfl([jnp.concatenate([v_new[h], zeros] if c == 0 else [zeros, v_new[h]], axis=0)
                         for h in heads])
            av = _pair_dots([attn[h][rows] for h in heads], v_pad)
            o = [r[h][C:SUB] + av[h] for h in heads]
            state = [state[h] * eg_last[h][c] + _dot(kdec[h][rows], _bf(v_new[h]), _TN) for h in heads]
            for h in heads:
                cs = slice(h * HEAD_DIM, (h + 1) * HEAD_DIM)
                ms = jnp.mean(o[h] * o[h], axis=-1, keepdims=True)
                zc = z_ref[0, pl.ds(r0 + c * C, C), cs].astype(F32)
                o_ref[0, pl.ds(r0 + c * C, C), cs] = (
                    o[h] * lax.rsqrt(ms + RMS_EPS) * ow_ref[...] * _silu(zc)).astype(o_ref.dtype)
        for h in heads:
            state_ref[h] = state[h]
        return carry

    lax.fori_loop(0, tt // SUB, sub_tile, 0)


def _gdn(proj3, conv_w, g, beta, out_norm_w, *, n_heads, col0, tt):
    B, S, _ = proj3.shape
    W = n_heads * HEAD_DIM
    kern = functools.partial(_gdn_kernel, tt=tt, n_heads=n_heads)
    cb = col0 // W
    nt = S // tt
    xspec = lambda off: pl.BlockSpec((1, tt, W), lambda b, t: (b, t, cb + off))
    gspec = pl.BlockSpec((n_heads, tt), lambda b, t: (0, b * nt + t))
    return pl.pallas_call(
        kern,
        grid=(B, nt),
        in_specs=[xspec(0), xspec(1), xspec(2), xspec(3),
                  pl.BlockSpec((GDN_CONV, 3 * W), lambda b, t: (0, 0)),
                  gspec, gspec, pl.BlockSpec((1, HEAD_DIM), lambda b, t: (0, 0))],
        out_specs=pl.BlockSpec((1, tt, W), lambda b, t: (b, t, 0)),
        out_shape=jax.ShapeDtypeStruct((B, S, W), BF16),
        scratch_shapes=[pltpu.VMEM((3, tt + 8, W), F32), pltpu.VMEM((3, tt, W), F32),
                        pltpu.VMEM((n_heads, HEAD_DIM, HEAD_DIM), F32)],
        compiler_params=_cparams(("arbitrary", "arbitrary")),
        name="gdn",
    )(proj3, proj3, proj3, proj3, conv_w, g, beta, out_norm_w)


def _outproj_router_kernel(a_ref, b_ref, wa_ref, wb_ref, x_ref, nw_ref, wr_ref, br_ref,
                           o_ref, h_ref, eid_ref, gate_ref):
    x = x_ref[...] + _dot(a_ref[...], wa_ref[...]) + _dot(b_ref[...], wb_ref[...])
    o_ref[...] = x
    tm = x.shape[0]
    ms = jnp.mean(x * x, axis=-1, keepdims=True)
    h = x * lax.rsqrt(ms + RMS_EPS) * nw_ref[...]
    words = _pack_words(h)
    for s, word in enumerate(words):
        h_ref[pl.ds(s, tm, stride=len(words)), :] = word
    logits = _dot3(wr_ref[...], h, _NT) + br_ref[...]
    best = logits[0:1]
    gidx = jnp.zeros((1, tm), I32)
    for g in range(1, N_GROUPS):
        better = logits[g:g + 1] > best
        gidx = jnp.where(better, g, gidx)
        best = jnp.where(better, logits[g:g + 1], best)
    gsum = jnp.zeros((1, tm), F32)
    for g in range(N_GROUPS):
        gsum = gsum + jnp.exp(logits[g:g + 1] - best)
    group_gate = 1.0 / gsum
    E = EXPERTS_PER_GROUP
    in_group = jnp.zeros((E, tm), F32)
    for g in range(N_GROUPS):
        in_group = jnp.where(gidx == g, logits[8 + g * E:8 + (g + 1) * E], in_group)
    sub = lax.broadcasted_iota(I32, (E, tm), 0)
    m1 = jnp.max(in_group, axis=0, keepdims=True)
    i1 = jnp.min(jnp.where(in_group == m1, sub, E), axis=0, keepdims=True)
    rest = jnp.where(sub == i1, -jnp.inf, in_group)
    m2 = jnp.max(rest, axis=0, keepdims=True)
    i2 = jnp.min(jnp.where(rest == m2, sub, E), axis=0, keepdims=True)
    e2 = jnp.exp(m2 - m1)
    inv = group_gate / (1.0 + e2)
    eid_ref[0:1, :] = gidx * E + i1
    eid_ref[1:2, :] = gidx * E + i2
    gate_ref[0:1, :] = inv
    gate_ref[1:2, :] = inv * e2


def _outproj_router(o_sb, o_gdn, w_o, x, norm_w, wr_t, br, *, tm):
    T, D = x.shape
    R = D // PACK_LANES
    wa, wb = o_sb.shape[1], o_gdn.shape[1]
    assert wa == wb and w_o.shape == (wa + wb, D)
    return pl.pallas_call(
        _outproj_router_kernel,
        grid=(T // tm,),
        in_specs=[
            pl.BlockSpec((tm, wa), lambda i: (i, 0)),
            pl.BlockSpec((tm, wb), lambda i: (i, 0)),
            pl.BlockSpec((wa, D), lambda i: (0, 0)),
            pl.BlockSpec((wb, D), lambda i: (1, 0)),
            pl.BlockSpec((tm, D), lambda i: (i, 0)),
            pl.BlockSpec((1, D), lambda i: (0, 0)),
            pl.BlockSpec((ROUTER_ROWS, D), lambda i: (0, 0)),
            pl.BlockSpec((ROUTER_ROWS, 1), lambda i: (0, 0)),
        ],
        out_specs=[
            pl.BlockSpec((tm, D), lambda i: (i, 0)),
            pl.BlockSpec((tm * R, HEAD_DIM), lambda i: (i, 0)),
            pl.BlockSpec((2, tm), lambda i: (0, i)),
            pl.BlockSpec((2, tm), lambda i: (0, i)),
        ],
        out_shape=[
            jax.ShapeDtypeStruct((T, D), F32),
            jax.ShapeDtypeStruct((T * R, HEAD_DIM), U32),
            jax.ShapeDtypeStruct((2, T), I32),
            jax.ShapeDtypeStruct((2, T), F32),
        ],
        compiler_params=_cparams(("arbitrary",)),
        name="outproj_router",
    )(o_sb, o_gdn, w_o, w_o, x, norm_w, wr_t, br)


def _plan_kernel(eid_ref, dest_ref, blk_ref, cnt_ref, *, tl, n_blk_pad):
    p = pl.program_id(0)
    i = pl.program_id(1)
    NE = N_EXPERTS
    e0 = eid_ref[0:1, :]
    e1 = eid_ref[1:2, :]
    sub = lax.broadcasted_iota(I32, (NE, tl), 0)
    hot0 = sub == e0
    hot1 = sub == e1
    onehot = jnp.logical_or(hot0, hot1).astype(BF16)
    ones = jnp.ones((tl, HEAD_DIM), BF16)

    @pl.when(jnp.logical_and(p == 0, i == 0))
    def _():
        cnt_ref[0] = jnp.zeros((NE, HEAD_DIM), F32)

    @pl.when(p == 0)
    def _():
        cnt_ref[0] += _dot(onehot, ones)

    @pl.when(jnp.logical_and(p == 1, i == 0))
    def _():
        cnt = cnt_ref[0]
        padded = jnp.floor((cnt + (MOE_BLOCK - 1)) * (1.0 / MOE_BLOCK)) * MOE_BLOCK
        er = lax.broadcasted_iota(I32, (NE, NE), 0)
        ec = lax.broadcasted_iota(I32, (NE, NE), 1)
        start = _dot3((ec < er).astype(F32), padded)
        cnt_ref[1] = start
        end_col = (start + padded)[:, 0:1]
        pos = (lax.broadcasted_iota(I32, (NE, n_blk_pad), 1) * MOE_BLOCK).astype(F32)
        n_before = jnp.sum((end_col <= pos).astype(I32), axis=0, keepdims=True)
        blk_ref[0:1, :] = jnp.minimum(n_before, NE - 1)
        total = jnp.max(end_col, axis=0, keepdims=True)
        blk_ref[1:2, :] = jnp.broadcast_to((total * (1.0 / MOE_BLOCK)).astype(I32), (1, n_blk_pad))
        cnt_ref[0] = jnp.zeros((NE, HEAD_DIM), F32)

    @pl.when(p == 1)
    def _():
        r = lax.broadcasted_iota(I32, (tl, tl), 0)
        c = lax.broadcasted_iota(I32, (tl, tl), 1)
        before = (r < c).astype(BF16)
        run = cnt_ref[0][:, 0:1] + cnt_ref[1][:, 0:1]
        slot = _dot(onehot, before) + run
        dest_ref[0:1, :] = jnp.sum(jnp.where(hot0, slot, 0.0), axis=0, keepdims=True).astype(I32)
        dest_ref[1:2, :] = jnp.sum(jnp.where(hot1, slot, 0.0), axis=0, keepdims=True).astype(I32)
        cnt_ref[0] += _dot(onehot, ones)


def _plan(eid, *, tl, n_blk_pad):
    T = eid.shape[1]
    kern = functools.partial(_plan_kernel, tl=tl, n_blk_pad=n_blk_pad)
    return pl.pallas_call(
        kern,
        grid=(2, T // tl),
        in_specs=[pl.BlockSpec((2, tl), lambda p, i: (0, i))],
        out_specs=[
            pl.BlockSpec((2, tl), lambda p, i: (0, i * p)),
            pl.BlockSpec((2, n_blk_pad), lambda p, i: (0, 0)),
        ],
        out_shape=[
            jax.ShapeDtypeStruct((2, T), I32),
            jax.ShapeDtypeStruct((2, n_blk_pad), I32),
        ],
        scratch_shapes=[pltpu.VMEM((2, N_EXPERTS, HEAD_DIM), F32)],
        compiler_params=_cparams(("arbitrary", "arbitrary")),
        name="plan",
    )(eid)


def _scatter_kernel(dest_ref, blk_ref, h_ref, xs_ref, zero_ref, sem, zsem, *, tt, n_blocks, R):
    i = pl.program_id(0)
    blk_rows = MOE_BLOCK * R

    @pl.when(i == 0)
    def _():
        zero_ref[...] = jnp.zeros_like(zero_ref)
        n_valid = blk_ref[1, 0]

        def clear(b, n):
            last = blk_ref[0, b] != blk_ref[0, jnp.minimum(b + 1, n_blocks - 1)]
            do = jnp.logical_or(b >= n_valid - 1, last)

            @pl.when(do)
            def _():
                pltpu.make_async_copy(zero_ref, xs_ref.at[pl.ds(pl.multiple_of(b * blk_rows, blk_rows), blk_rows)],
                                      zsem).start()
            return n + do.astype(I32)
        n_started = lax.fori_loop(0, n_blocks, clear, 0)

        def drain(b, carry):
            pltpu.make_async_copy(zero_ref, xs_ref.at[pl.ds(0, blk_rows)], zsem).wait()
            return carry
        lax.fori_loop(0, n_started, drain, 0)

    def issue(r, carry):
        src = h_ref.at[pl.ds(pl.multiple_of(r * R, R), R)]
        for kk in range(2):
            dst = xs_ref.at[pl.ds(pl.multiple_of(dest_ref[0, 0, 2 * r + kk] * R, R), R)]
            pltpu.make_async_copy(src, dst, sem).start(priority=kk)
        return carry
    lax.fori_loop(0, tt, issue, 0, unroll=4)

    for kk in range(2):
        pltpu.make_async_copy(h_ref, xs_ref.at[pl.ds(0, tt * R)], sem).wait()


def _scatter_rows(dest_flat, blk, h_packed, *, cap, tt, R):
    T = h_packed.shape[0] // R
    n_blocks = cap // MOE_BLOCK
    kern = functools.partial(_scatter_kernel, tt=tt, n_blocks=n_blocks, R=R)
    return pl.pallas_call(
        kern,
        grid=(T // tt,),
        in_specs=[
            pl.BlockSpec((1, 1, 2 * tt), lambda i: (i, 0, 0), memory_space=pltpu.SMEM),
            pl.BlockSpec(memory_space=pltpu.SMEM),
            pl.BlockSpec((tt * R, HEAD_DIM), lambda i: (i, 0)),
        ],
        out_specs=pl.BlockSpec(memory_space=pl.ANY),
        out_shape=jax.ShapeDtypeStruct((cap * R, HEAD_DIM), U32),
        scratch_shapes=[pltpu.VMEM((MOE_BLOCK * R, HEAD_DIM), U32), pltpu.SemaphoreType.DMA,
                        pltpu.SemaphoreType.DMA],
        compiler_params=_cparams(("arbitrary",)),
        name="scatter_rows",
    )(dest_flat, blk, h_packed)


def _expert_kernel(blk_ref, x_ref, w1_hbm, w3_hbm, w2_hbm, y_ref, wf1_ref, wf3_ref, wf2_ref,
                   w1b_ref, w3b_ref, w2b_ref, ord_ref, sems, *, layer):
    b = pl.program_id(0)
    n_valid = blk_ref[1, 0]
    valid = b < n_valid
    e = blk_ref[0, b]
    new_expert = jnp.logical_or(b == 0, e != blk_ref[0, jnp.maximum(b - 1, 0)])

    def copies(expert, slot):
        return [pltpu.make_async_copy(w_hbm.at[layer, expert], wf_ref.at[slot], sems.at[slot])
                for w_hbm, wf_ref in ((w1_hbm, wf1_ref), (w3_hbm, wf3_ref), (w2_hbm, wf2_ref))]

    @pl.when(b == 0)
    def _():
        ord_ref[0] = 0
        for c in copies(e, 0):
            c.start()

    @pl.when(jnp.logical_and(valid, new_expert))
    def _():
        slot = lax.rem(ord_ref[0], 2)
        for c in copies(e, slot):
            c.wait()
        w1b_ref[...] = wf1_ref[slot].astype(BF16)
        w3b_ref[...] = wf3_ref[slot].astype(BF16)
        w2b_ref[...] = wf2_ref[slot].astype(BF16)
        nb = lax.while_loop(lambda j: jnp.logical_and(j < n_valid, blk_ref[0, j] == e), lambda j: j + 1, b + 1)

        @pl.when(nb < n_valid)
        def _():
            for c in copies(blk_ref[0, nb], 1 - slot):
                c.start()
        ord_ref[0] = ord_ref[0] + 1

    @pl.when(valid)
    def _():
        R = x_ref.shape[0] // MOE_BLOCK
        lo, hi = _unpack_words([x_ref[pl.ds(s, MOE_BLOCK, stride=R), :] for s in range(R)])
        x = jnp.concatenate(lo + hi, axis=1).astype(BF16)
        hid = _silu(_dot(x, w1b_ref[...])) * _dot(x, w3b_ref[...])
        y = _dot(hid.astype(BF16), w2b_ref[...])
        for s, word in enumerate(_pack_words(y)):
            y_ref[pl.ds(s, MOE_BLOCK, stride=R), :] = word

    @pl.when(b >= blk_ref[1, 0])
    def _():
        y_ref[...] = jnp.zeros_like(y_ref)


def _experts(blk, x_slots, w1, w3, w2, *, layer, R):
    cap = x_slots.shape[0] // R
    n_blocks = cap // MOE_BLOCK
    D, F = w1.shape[2], w1.shape[3]

    assert blk.shape[1] > n_blocks

    def row_blk(b, blk_ref):
        return jnp.minimum(b, blk_ref[1, 0] - 1)

    grid_spec = pltpu.PrefetchScalarGridSpec(
        num_scalar_prefetch=1,
        grid=(n_blocks,),
        in_specs=[
            pl.BlockSpec((MOE_BLOCK * R, HEAD_DIM), lambda b, blk_ref: (row_blk(b, blk_ref), 0)),
            pl.BlockSpec(memory_space=pl.ANY),
            pl.BlockSpec(memory_space=pl.ANY),
            pl.BlockSpec(memory_space=pl.ANY),
        ],
        out_specs=pl.BlockSpec((MOE_BLOCK * R, HEAD_DIM), lambda b, blk_ref: (b, 0)),
        scratch_shapes=[pltpu.VMEM((2, D, F), F32), pltpu.VMEM((2, D, F), F32), pltpu.VMEM((2, F, D), F32),
                        pltpu.VMEM((D, F), BF16), pltpu.VMEM((D, F), BF16), pltpu.VMEM((F, D), BF16),
                        pltpu.SMEM((1,), I32), pltpu.SemaphoreType.DMA((2,))],
    )
    return pl.pallas_call(
        functools.partial(_expert_kernel, layer=layer),
        grid_spec=grid_spec,
        out_shape=jax.ShapeDtypeStruct((cap * R, HEAD_DIM), U32),
        compiler_params=_cparams(("arbitrary",)),
        name="experts",
    )(blk, x_slots, w1, w3, w2)


def _combine_kernel(dest_ref, dnext_ref, ys_ref, x_ref, gate_ref, o_ref, buf_ref, sems, *, tt, R):
    i = pl.program_id(0)
    slot = lax.rem(i, 2)

    def issue(d_ref, s):
        def one(r, carry):
            for kk in range(2):
                src = ys_ref.at[pl.ds(pl.multiple_of(d_ref[0, 0, 2 * r + kk] * R, R), R)]
                pltpu.make_async_copy(src, buf_ref.at[s, kk, pl.ds(pl.multiple_of(r * R, R), R)],
                                      sems.at[s]).start(priority=kk)
            return carry
        lax.fori_loop(0, tt, one, 0, unroll=4)

    @pl.when(i == 0)
    def _():
        issue(dest_ref, 0)

    @pl.when(i + 1 < pl.num_programs(0))
    def _():
        issue(dnext_ref, 1 - slot)

    for kk in range(2):
        pltpu.make_async_copy(ys_ref.at[pl.ds(0, tt * R)], buf_ref.at[slot, kk], sems.at[slot]).wait()
    g = gate_ref[...]
    half = R * HEAD_DIM
    for s in range(R):
        lo0, hi0 = _unpack_words([buf_ref[slot, 0, pl.ds(s, tt, stride=R), :]])
        lo1, hi1 = _unpack_words([buf_ref[slot, 1, pl.ds(s, tt, stride=R), :]])
        for off, y0, y1 in ((s * HEAD_DIM, lo0[0], lo1[0]), (half + s * HEAD_DIM, hi0[0], hi1[0])):
            cs = slice(off, off + HEAD_DIM)
            o_ref[:, cs] = x_ref[:, cs] + (y0 * g[:, 0:1] + y1 * g[:, 1:2])


def _combine(dest_flat, y_slots, x, gate_t, *, tt, R):
    T, D = x.shape
    n = T // tt
    kern = functools.partial(_combine_kernel, tt=tt, R=R)
    return pl.pallas_call(
        kern,
        grid=(n,),
        in_specs=[
            pl.BlockSpec((1, 1, 2 * tt), lambda i: (i, 0, 0), memory_space=pltpu.SMEM),
            pl.BlockSpec((1, 1, 2 * tt), lambda i: (jnp.minimum(i + 1, n - 1), 0, 0), memory_space=pltpu.SMEM),
            pl.BlockSpec(memory_space=pl.ANY),
            pl.BlockSpec((tt, D), lambda i: (i, 0)),
            pl.BlockSpec((tt, 2), lambda i: (i, 0)),
        ],
        out_specs=pl.BlockSpec((tt, D), lambda i: (i, 0)),
        out_shape=jax.ShapeDtypeStruct((T, D), F32),
        scratch_shapes=[pltpu.VMEM((2, 2, tt * R, HEAD_DIM), U32), pltpu.SemaphoreType.DMA((2,))],
        compiler_params=_cparams(("arbitrary",)),
        name="combine",
    )(dest_flat, dest_flat, y_slots, x, gate_t)


def _mixer(x2, batch, layer, norm_w, w_in, q_norm_w, k_norm_w, conv_w, a_log, dt_bias, out_norm_w, w_o,
           norm2_w, w_group, b_group, w_expert, b_expert, *, tm, tn, tq, tk, hp, tt, tm_out):
    T, D = x2.shape
    S = T // batch
    n_gdn = a_log.shape[0]
    gdn_width = n_gdn * HEAD_DIM
    n_main = w_in.shape[2] - 2 * n_gdn
    sb_width = (n_main - 4 * gdn_width) // 3
    n_sb = sb_width // HEAD_DIM
    reps = tn // HEAD_DIM
    qk_w = jnp.stack([jnp.tile(q_norm_w, reps) * (1.0 / math.sqrt(HEAD_DIM)), jnp.tile(k_norm_w, reps)])
    proj, g, beta = _inproj(x2, norm_w.reshape(1, D), w_in, qk_w,
                            a_log.reshape(-1, 1), dt_bias.reshape(-1, 1), layer=layer, sb_width=sb_width,
                            n_main=n_main, tm=tm, tn=tn)
    proj3 = proj.reshape(batch, S, n_main)
    o_sb = _sb_attention(proj3, n_heads=n_sb, tq=tq, tk=tk, hp=hp)
    o_gdn = _gdn(proj3, conv_w, g, beta, out_norm_w.reshape(1, HEAD_DIM), n_heads=n_gdn,
                 col0=3 * sb_width, tt=tt)
    tail = ROUTER_ROWS - 8 - N_EXPERTS
    wr_t = jnp.concatenate([w_group, jnp.zeros((D, 8 - N_GROUPS), F32), w_expert,
                            jnp.zeros((D, tail), F32)], axis=1).T
    br = jnp.concatenate([b_group, jnp.zeros((8 - N_GROUPS,), F32), b_expert,
                          jnp.zeros((tail,), F32)]).reshape(-1, 1)
    return _outproj_router(o_sb.reshape(T, sb_width), o_gdn.reshape(T, gdn_width), w_o.astype(BF16), x2,
                           norm2_w.reshape(1, D), wr_t, br, tm=tm_out)


def _moe(x2, h, eid, gate, layer, w1, w3, w2, *, tl, tt, tc):
    T, D = x2.shape
    R = D // PACK_LANES
    n_blocks = (2 * T + MOE_BLOCK - 1) // MOE_BLOCK + N_EXPERTS
    n_blk_pad = (n_blocks // 128 + 1) * 128
    dest, blk = _plan(eid, tl=tl, n_blk_pad=n_blk_pad)
    dest_t = dest.T
    x_slots = _scatter_rows(dest_t.reshape(T // tt, 1, 2 * tt), blk, h, cap=n_blocks * MOE_BLOCK, tt=tt, R=R)
    y_slots = _experts(blk, x_slots, w1, w3, w2, layer=layer, R=R)
    return _combine(dest_t.reshape(T // tc, 1, 2 * tc), y_slots, x2, gate.T, tt=tc, R=R)


def _forward(x, norm1_w, w_in, sb_q_norm_w, sb_k_norm_w, gdn_conv_w, gdn_a_log, gdn_dt_bias,
             gdn_out_norm_w, w_o, norm2_w, w_group, b_group, w_expert, b_expert, w1, w3, w2, *, tiles):
    batch, S, D = x.shape
    x2 = x.reshape(batch * S, D)
    w_in = w_in.astype(BF16)
    for l in range(norm1_w.shape[0]):
        x2, h, eid, gate = _mixer(x2, batch, l, norm1_w[l], w_in, sb_q_norm_w[l], sb_k_norm_w[l],
                                  gdn_conv_w[l], gdn_a_log[l], gdn_dt_bias[l], gdn_out_norm_w[l], w_o[l],
                                  norm2_w[l], w_group[l], b_group[l], w_expert[l], b_expert[l],
                                  **tiles["mixer"])
        x2 = _moe(x2, h, eid, gate, l, w1, w3, w2, **tiles["moe"])
    return x2.reshape(batch, S, D)


_TILES = {
    "mixer": dict(tm=1024, tn=1024, tq=256, tk=256, hp=8, tt=512, tm_out=512),
    "moe": dict(tl=512, tt=512, tc=256),
}


def kernel(x, norm1_w, w_in, sb_q_norm_w, sb_k_norm_w, gdn_conv_w, gdn_a_log, gdn_dt_bias, gdn_out_norm_w, w_o, norm2_w, w_group, b_group, w_expert, b_expert, w1, w3, w2):
    return _forward(x, norm1_w, w_in, sb_q_norm_w, sb_k_norm_w, gdn_conv_w, gdn_a_log, gdn_dt_bias,
                    gdn_out_norm_w, w_o, norm2_w, w_group, b_group, w_expert, b_expert, w1, w3, w2,
                    tiles=_TILES)
```

```python
import functools
import math

import jax
import jax.numpy as jnp
from jax import lax
from jax.experimental import pallas as pl
from jax.experimental.pallas import tpu as pltpu

F32 = jnp.float32
BF16 = jnp.bfloat16
I32 = jnp.int32

HEAD_DIM = 128
GDN_CONV = 4
GDN_CHUNK = 64
N_GROUPS = 4
EXPERTS_PER_GROUP = 8
N_EXPERTS = N_GROUPS * EXPERTS_PER_GROUP
MOE_BLOCK = 512
RMS_EPS = 1e-6
EXP_UNDERFLOW = -104.0
ROUTER_ROWS = 48
VMEM_LIMIT = 56 * 1024 * 1024


def _cparams(sem, vmem=VMEM_LIMIT):
    return pltpu.CompilerParams(dimension_semantics=sem, vmem_limit_bytes=vmem)


def _dot(a, b, dims=(((1,), (0,)), ((), ()))):
    return lax.dot_general(a, b, dims, preferred_element_type=F32)


_NN = (((1,), (0,)), ((), ()))
_NT = (((1,), (1,)), ((), ()))
_TN = (((0,), (0,)), ((), ()))


def _split(a):
    hi = a.astype(BF16)
    lo = (a - hi.astype(F32)).astype(BF16)
    return hi, lo


def _dot3(a, b, dims=(((1,), (0,)), ((), ()))):
    ah, al = _split(a)
    bh, bl = _split(b)
    return _dot(ah, bh, dims) + (_dot(ah, bl, dims) + _dot(al, bh, dims))


U32 = jnp.uint32
PACK_LANES = 2 * HEAD_DIM


def _pack_words(a):
    half = a.shape[1] // 2
    bits = lax.bitcast_convert_type(a.astype(BF16).astype(F32), U32)
    return [(bits[:, half + s * HEAD_DIM:half + (s + 1) * HEAD_DIM] & jnp.uint32(0xFFFF0000))
            | (bits[:, s * HEAD_DIM:(s + 1) * HEAD_DIM] >> 16) for s in range(half // HEAD_DIM)]


def _unpack_words(words):
    lo = [lax.bitcast_convert_type(w << 16, F32) for w in words]
    hi = [lax.bitcast_convert_type(w & jnp.uint32(0xFFFF0000), F32) for w in words]
    return lo, hi


def _softplus(x):
    return jnp.maximum(x, 0.0) + jnp.log(1.0 + jnp.exp(-jnp.abs(x)))


def _sigmoid(x):
    return 1.0 / (1.0 + jnp.exp(-x))


def _inproj_kernel(x_ref, nw_ref, w_ref, wab_ref, qkw_ref, alog_ref, dt_ref,
                   proj_ref, g_ref, beta_ref, h_ref, *, n_qk_tiles, n_heads, rows):
    j = pl.program_id(1)
    tm = x_ref.shape[0]

    @pl.when(j == 0)
    def _():
        def norm_rows(r, carry):
            rs = pl.multiple_of(r * rows, rows)
            x = x_ref[pl.ds(rs, rows), :]
            ms = jnp.mean(x * x, axis=-1, keepdims=True)
            h = x * lax.rsqrt(ms + RMS_EPS) * nw_ref[...]
            h_ref[pl.ds(rs, rows), :] = h.astype(BF16)
            return carry
        lax.fori_loop(0, tm // rows, norm_rows, 0)
        ab = _dot(wab_ref[0].astype(F32).T.astype(BF16), h_ref[...], _NT)
        ga = ab[0:n_heads]
        gb = ab[n_heads:2 * n_heads]
        g_ref[...] = -jnp.exp(alog_ref[...]) * _softplus(ga + dt_ref[...])
        beta_ref[...] = _sigmoid(gb)

    acc = _dot(h_ref[...], w_ref[0])

    is_qk = j < n_qk_tiles
    wsel = jnp.where(j < n_qk_tiles // 2, qkw_ref[0:1, :], qkw_ref[1:2, :])
    for hh in range(acc.shape[1] // HEAD_DIM):
        sl = slice(hh * HEAD_DIM, (hh + 1) * HEAD_DIM)
        a = acc[:, sl]
        ms = jnp.mean(a * a, axis=-1, keepdims=True)
        normed = a * lax.rsqrt(ms + RMS_EPS) * wsel[:, sl]
        proj_ref[:, sl] = jnp.where(is_qk, normed, a).astype(BF16)


def _inproj(x, norm_w, w_main, qk_w, a_log, dt_bias, *, layer, sb_width, n_main, tm, tn):
    T, D = x.shape
    assert n_main % HEAD_DIM == 0 and w_main.shape[2] == n_main + 2 * a_log.shape[0]
    N = n_main
    n_heads = a_log.shape[0]
    n_qk_tiles = 2 * sb_width // tn
    kern = functools.partial(_inproj_kernel, n_qk_tiles=n_qk_tiles, n_heads=n_heads,
                             rows=min(tm, 256))
    return pl.pallas_call(
        kern,
        grid=(T // tm, N // tn),
        in_specs=[
            pl.BlockSpec((tm, D), lambda i, j: (i, 0)),
            pl.BlockSpec((1, D), lambda i, j: (0, 0)),
            pl.BlockSpec((1, D, tn), lambda i, j: (layer, 0, j)),
            pl.BlockSpec((1, D, HEAD_DIM), lambda i, j: (layer, 0, n_main // HEAD_DIM)),
            pl.BlockSpec((2, tn), lambda i, j: (0, 0)),
            pl.BlockSpec((n_heads, 1), lambda i, j: (0, 0)),
            pl.BlockSpec((n_heads, 1), lambda i, j: (0, 0)),
        ],
        out_specs=[
            pl.BlockSpec((tm, tn), lambda i, j: (i, j)),
            pl.BlockSpec((n_heads, tm), lambda i, j: (0, i)),
            pl.BlockSpec((n_heads, tm), lambda i, j: (0, i)),
        ],
        out_shape=[
            jax.ShapeDtypeStruct((T, N), BF16),
            jax.ShapeDtypeStruct((n_heads, T), F32),
            jax.ShapeDtypeStruct((n_heads, T), F32),
        ],
        scratch_shapes=[pltpu.VMEM((tm, D), BF16)],
        compiler_params=_cparams(("arbitrary", "arbitrary")),
        name="inproj",
    )(x, norm_w, w_main, w_main, qk_w, a_log, dt_bias)


def _sb_kernel(q_ref, k_ref, v_ref, o_ref, *, tq, tk, hp):
    i = pl.program_id(2)
    heads = range(hp)
    hs = [slice(h * HEAD_DIM, (h + 1) * HEAD_DIM) for h in heads]
    q = [q_ref[0, :, hs[h]] for h in heads]
    row = lax.broadcasted_iota(I32, (tk, tk), 0)
    col = lax.broadcasted_iota(I32, (tk, tk), 1)
    upper_incl = (row >= col).astype(BF16)
    rel = lax.broadcasted_iota(I32, (tq, tk), 1) - lax.broadcasted_iota(I32, (tq, tk), 0)

    def block(ks, carry, masked):
        acc, c = carry
        z = [_dot(q[h], k_ref[0, pl.ds(ks, tk), hs[h]], _NT) for h in heads]
        log_1mb = [-_softplus(z[h]) for h in heads]
        if masked:
            causal = rel + (ks - i * tq) < 0
            log_1mb = [jnp.where(causal, log_1mb[h], 0.0) for h in heads]
        parts = [_split(log_1mb[h]) for h in heads]
        incl = [_dot(parts[h][0], upper_incl) + _dot(parts[h][1], upper_incl) for h in heads]
        w = [jnp.exp(z[h] + incl[h] + c[h]) for h in heads]
        if masked:
            w = [jnp.where(causal, w[h], 0.0) for h in heads]
        acc = [acc[h] + _dot(_bf(w[h]), v_ref[0, pl.ds(ks, tk), hs[h]]) for h in heads]
        c = [c[h] + incl[h][:, 0:1] for h in heads]
        return acc, c

    carry = ([jnp.zeros((tq, HEAD_DIM), F32) for _ in heads], [jnp.zeros((tq, 1), F32) for _ in heads])
    n_diag = tq // tk
    for d in range(n_diag - 1, -1, -1):
        carry = block(pl.multiple_of(i * tq + d * tk, tk), carry, True)

    def c_max(c):
        m = c[0]
        for h in heads[1:]:
            m = jnp.maximum(m, c[h])
        return jnp.max(m)

    def cond(state):
        jj, live, _ = state
        return jnp.logical_and(jj < i * n_diag, live)

    def body(state):
        jj, _, carry = state
        carry = block(pl.multiple_of((i * n_diag - 1 - jj) * tk, tk), carry, False)
        return jj + 1, c_max(carry[1]) > EXP_UNDERFLOW, carry

    _, _, (acc, _) = lax.while_loop(cond, body, (jnp.int32(0), c_max(carry[1]) > EXP_UNDERFLOW, carry))
    for h in heads:
        o_ref[0, :, hs[h]] = acc[h].astype(o_ref.dtype)


def _sb_attention(proj3, *, n_heads, tq, tk, hp):
    B, S, _ = proj3.shape
    kern = functools.partial(_sb_kernel, tq=tq, tk=tk, hp=hp)
    ng = n_heads // hp
    wd = hp * HEAD_DIM
    return pl.pallas_call(
        kern,
        grid=(B, ng, S // tq),
        in_specs=[
            pl.BlockSpec((1, tq, wd), lambda b, h, i: (b, i, h)),
            pl.BlockSpec((1, S, wd), lambda b, h, i: (b, 0, ng + h)),
            pl.BlockSpec((1, S, wd), lambda b, h, i: (b, 0, 2 * ng + h)),
        ],
        out_specs=pl.BlockSpec((1, tq, wd), lambda b, h, i: (b, i, h)),
        out_shape=jax.ShapeDtypeStruct((B, S, n_heads * HEAD_DIM), BF16),
        compiler_params=_cparams(("arbitrary", "arbitrary", "arbitrary")),
        name="sb_attention",
    )(proj3, proj3, proj3)


def _silu(x):
    return x * _sigmoid(x)


SUB = 2 * GDN_CHUNK


def _bf(a):
    return a.astype(BF16)


def _exact_parts(a):
    p1 = a.astype(BF16)
    r1 = a - p1.astype(F32)
    p2 = r1.astype(BF16)
    p3 = (r1 - p2.astype(F32)).astype(BF16)
    return p1, p2, p3


def _pair_dots(a, b, nt=False):
    return [_dot(x, y, _NT if nt else _NN) for x, y in zip(a, b)]


def _gdn_kernel(xq_ref, xk_ref, xv_ref, z_ref, cw_ref, g_ref, b_ref, ow_ref,
                o_ref, xs_ref, qkv_ref, state_ref, *, tt, n_heads):
    t = pl.program_id(1)
    C = GDN_CHUNK
    HALO = 8
    W = n_heads * HEAD_DIM
    heads = range(n_heads)

    @pl.when(t == 0)
    def _():
        xs_ref[:, 0:HALO, :] = jnp.zeros((3, HALO, W), F32)
        state_ref[...] = jnp.zeros_like(state_ref)

    @pl.when(t > 0)
    def _():
        xs_ref[:, 0:HALO, :] = xs_ref[:, tt:tt + HALO, :]

    for idx, x_ref in enumerate((xq_ref, xk_ref, xv_ref)):
        for h in heads:
            cs = slice(h * HEAD_DIM, (h + 1) * HEAD_DIM)
            xs_ref[idx, HALO:, cs] = x_ref[0, :, cs].astype(F32)
            acc = None
            for kk in range(GDN_CONV):
                wrow = cw_ref[kk:kk + 1, idx * W + h * HEAD_DIM: idx * W + (h + 1) * HEAD_DIM]
                term = xs_ref[idx, pl.ds(HALO - GDN_CONV + 1 + kk, tt), cs] * wrow
                acc = term if acc is None else acc + term
            a = _silu(acc)
            if idx < 2:
                inv = lax.rsqrt(jnp.sum(a * a, axis=-1, keepdims=True) + RMS_EPS)
                a = a * (inv * (1.0 / math.sqrt(HEAD_DIM)) if idx == 0 else inv)
            qkv_ref[idx, :, cs] = a

    ri = lax.broadcasted_iota(I32, (SUB, SUB), 0)
    ci = lax.broadcasted_iota(I32, (SUB, SUB), 1)
    same = (ri >> 6) == (ci >> 6)
    lower_incl = jnp.logical_and(same, ri >= ci)
    lower_strict = jnp.logical_and(same, ri > ci)
    blk16 = (ri >> 4) == (ci >> 4)
    blk32 = (ri >> 5) == (ci >> 5)
    only32 = jnp.logical_and(blk32, jnp.logical_not(blk16))
    only64 = jnp.logical_and(same, jnp.logical_not(blk32))
    eye_f = (ri == ci).astype(F32)
    eye_b = (ri == ci).astype(BF16)
    cum_b = jnp.logical_and(same, ri <= ci).astype(BF16)
    first = lax.broadcasted_iota(I32, (SUB, 1), 0) < C

    def sub_tile(s, carry):
        r0 = pl.multiple_of(s * SUB, SUB)
        g_rows = g_ref[:, pl.ds(r0, SUB)]
        b_rows = b_ref[:, pl.ds(r0, SUB)]
        gp = _exact_parts(g_rows)
        gc_rows = _dot(gp[0], cum_b) + _dot(gp[1], cum_b) + _dot(gp[2], cum_b)
        sp = _exact_parts(jnp.concatenate([gc_rows, b_rows], axis=0))
        cols = _dot(eye_b, sp[0], _NT) + _dot(eye_b, sp[1], _NT) + _dot(eye_b, sp[2], _NT)

        q, k, v, beta, decay, egc, kdec, eg_last = [], [], [], [], [], [], [], []
        for h in heads:
            cs = slice(h * HEAD_DIM, (h + 1) * HEAD_DIM)
            q.append(qkv_ref[0, pl.ds(r0, SUB), cs])
            k.append(qkv_ref[1, pl.ds(r0, SUB), cs])
            v.append(qkv_ref[2, pl.ds(r0, SUB), cs])
            gc_col = cols[:, h:h + 1]
            beta.append(cols[:, n_heads + h:n_heads + h + 1])
            decay.append(jnp.where(lower_incl, jnp.exp(jnp.minimum(gc_col - gc_rows[h:h + 1, :], 0.0)), 0.0))
            egc.append(jnp.exp(gc_col))
            g_last = jnp.where(first, gc_col[C - 1:C, :], gc_col[SUB - 1:SUB, :])
            kdec.append(_bf(k[h] * jnp.exp(g_last - gc_col)))
            eg_last.append((jnp.exp(gc_col[C - 1:C, :]), jnp.exp(gc_col[SUB - 1:SUB, :])))

        def bfl(xs):
            return [_bf(x) for x in xs]

        kb = [k[h] * beta[h] for h in heads]
        k_b = bfl(k)
        kk = _pair_dots(bfl(kb), k_b, nt=True)
        lmat = [jnp.where(lower_strict, kk[h] * decay[h], 0.0) for h in heads]
        d16 = [jnp.where(blk16, lmat[h], 0.0) for h in heads]
        d16_b = bfl(d16)
        p2 = bfl(_pair_dots(d16_b, d16_b))
        p4 = bfl(_pair_dots(p2, p2))
        p8 = bfl(_pair_dots(p4, p4))
        x0 = [eye_f - d16[h] for h in heads]
        for p in (p2, p4, p8):
            step = _pair_dots(bfl(x0), p)
            x0 = [x0[h] + step[h] for h in heads]
        x0_b = bfl(x0)
        y1 = bfl(_pair_dots(x0_b, bfl([jnp.where(only32, lmat[h], 0.0) for h in heads])))
        step = _pair_dots(y1, x0_b)
        x1 = [x0[h] - step[h] for h in heads]
        x1_b = bfl(x1)
        y2 = bfl(_pair_dots(x1_b, bfl([jnp.where(only64, lmat[h], 0.0) for h in heads])))
        step = _pair_dots(y2, x1_b)
        t_b = bfl([x1[h] - step[h] for h in heads])
        uw = _pair_dots(t_b, bfl([jnp.concatenate([v[h] * beta[h], kb[h] * egc[h]], axis=1) for h in heads]))
        u = [uw[h][:, :HEAD_DIM] for h in heads]
        w = [uw[h][:, HEAD_DIM:] for h in heads]
        qk = _pair_dots(bfl(q), k_b, nt=True)
        attn = bfl([jnp.where(lower_incl, qk[h] * decay[h], 0.0) for h in heads])
        qg = [q[h] * egc[h] for h in heads]

        state = [state_ref[h] for h in heads]
        zeros = jnp.zeros((C, HEAD_DIM), F32)
        for c in range(2):
            rows = slice(c * C, (c + 1) * C)
            wq = bfl([jnp.concatenate([w[h][rows], qg[h][rows]], axis=0) for h in heads])
            r = _pair_dots(wq, bfl(state))
            v_new = [u[h][rows] - r[h][0:C] for h in heads]
            v_pad = bfl([jnp.concatenate([v_new[h], zeros] if c == 0 else [zeros, v_new[h]], axis=0)
                         for h in heads])
            av = _pair_dots([attn[h][rows] for h in heads], v_pad)
            o = [r[h][C:SUB] + av[h] for h in heads]
            state = [state[h] * eg_last[h][c] + _dot(kdec[h][rows], _bf(v_new[h]), _TN) for h in heads]
            for h in heads:
                cs = slice(h * HEAD_DIM, (h + 1) * HEAD_DIM)
                ms = jnp.mean(o[h] * o[h], axis=-1, keepdims=True)
                zc = z_ref[0, pl.ds(r0 + c * C, C), cs].astype(F32)
                o_ref[0, pl.ds(r0 + c * C, C), cs] = (
                    o[h] * lax.rsqrt(ms + RMS_EPS) * ow_ref[...] * _silu(zc)).astype(o_ref.dtype)
        for h in heads:
            state_ref[h] = state[h]
        return carry

    lax.fori_loop(0, tt // SUB, sub_tile, 0)


def _gdn(proj3, conv_w, g, beta, out_norm_w, *, n_heads, col0, tt):
    B, S, _ = proj3.shape
    W = n_heads * HEAD_DIM
    kern = functools.partial(_gdn_kernel, tt=tt, n_heads=n_heads)
    cb = col0 // W
    nt = S // tt
    xspec = lambda off: pl.BlockSpec((1, tt, W), lambda b, t: (b, t, cb + off))
    gspec = pl.BlockSpec((n_heads, tt), lambda b, t: (0, b * nt + t))
    return pl.pallas_call(
        kern,
        grid=(B, nt),
        in_specs=[xspec(0), xspec(1), xspec(2), xspec(3),
                  pl.BlockSpec((GDN_CONV, 3 * W), lambda b, t: (0, 0)),
                  gspec, gspec, pl.BlockSpec((1, HEAD_DIM), lambda b, t: (0, 0))],
        out_specs=pl.BlockSpec((1, tt, W), lambda b, t: (b, t, 0)),
        out_shape=jax.ShapeDtypeStruct((B, S, W), BF16),
        scratch_shapes=[pltpu.VMEM((3, tt + 8, W), F32), pltpu.VMEM((3, tt, W), F32),
                        pltpu.VMEM((n_heads, HEAD_DIM, HEAD_DIM), F32)],
        compiler_params=_cparams(("arbitrary", "arbitrary")),
        name="gdn",
    )(proj3, proj3, proj3, proj3, conv_w, g, beta, out_norm_w)


def _outproj_router_kernel(a_ref, b_ref, wa_ref, wb_ref, x_ref, nw_ref, wr_ref, br_ref,
                           o_ref, h_ref, eid_ref, gate_ref):
    x = x_ref[...] + _dot(a_ref[...], wa_ref[...]) + _dot(b_ref[...], wb_ref[...])
    o_ref[...] = x
    tm = x.shape[0]
    ms = jnp.mean(x * x, axis=-1, keepdims=True)
    h = x * lax.rsqrt(ms + RMS_EPS) * nw_ref[...]
    words = _pack_words(h)
    for s, word in enumerate(words):
        h_ref[pl.ds(s, tm, stride=len(words)), :] = word
    logits = _dot3(wr_ref[...], h, _NT) + br_ref[...]
    best = logits[0:1]
    gidx = jnp.zeros((1, tm), I32)
    for g in range(1, N_GROUPS):
        better = logits[g:g + 1] > best
        gidx = jnp.where(better, g, gidx)
        best = jnp.where(better, logits[g:g + 1], best)
    gsum = jnp.zeros((1, tm), F32)
    for g in range(N_GROUPS):
        gsum = gsum + jnp.exp(logits[g:g + 1] - best)
    group_gate = 1.0 / gsum
    E = EXPERTS_PER_GROUP
    in_group = jnp.zeros((E, tm), F32)
    for g in range(N_GROUPS):
        in_group = jnp.where(gidx == g, logits[8 + g * E:8 + (g + 1) * E], in_group)
    sub = lax.broadcasted_iota(I32, (E, tm), 0)
    m1 = jnp.max(in_group, axis=0, keepdims=True)
    i1 = jnp.min(jnp.where(in_group == m1, sub, E), axis=0, keepdims=True)
    rest = jnp.where(sub == i1, -jnp.inf, in_group)
    m2 = jnp.max(rest, axis=0, keepdims=True)
    i2 = jnp.min(jnp.where(rest == m2, sub, E), axis=0, keepdims=True)
    e2 = jnp.exp(m2 - m1)
    inv = group_gate / (1.0 + e2)
    eid_ref[0:1, :] = gidx * E + i1
    eid_ref[1:2, :] = gidx * E + i2
    gate_ref[0:1, :] = inv
    gate_ref[1:2, :] = inv * e2


def _outproj_router(o_sb, o_gdn, w_o, x, norm_w, wr_t, br, *, tm):
    T, D = x.shape
    R = D // PACK_LANES
    wa, wb = o_sb.shape[1], o_gdn.shape[1]
    assert wa == wb and w_o.shape == (wa + wb, D)
    return pl.pallas_call(
        _outproj_router_kernel,
        grid=(T // tm,),
        in_specs=[
            pl.BlockSpec((tm, wa), lambda i: (i, 0)),
            pl.BlockSpec((tm, wb), lambda i: (i, 0)),
            pl.BlockSpec((wa, D), lambda i: (0, 0)),
            pl.BlockSpec((wb, D), lambda i: (1, 0)),
            pl.BlockSpec((tm, D), lambda i: (i, 0)),
            pl.BlockSpec((1, D), lambda i: (0, 0)),
            pl.BlockSpec((ROUTER_ROWS, D), lambda i: (0, 0)),
            pl.BlockSpec((ROUTER_ROWS, 1), lambda i: (0, 0)),
        ],
        out_specs=[
            pl.BlockSpec((tm, D), lambda i: (i, 0)),
            pl.BlockSpec((tm * R, HEAD_DIM), lambda i: (i, 0)),
            pl.BlockSpec((2, tm), lambda i: (0, i)),
            pl.BlockSpec((2, tm), lambda i: (0, i)),
        ],
        out_shape=[
            jax.ShapeDtypeStruct((T, D), F32),
            jax.ShapeDtypeStruct((T * R, HEAD_DIM), U32),
            jax.ShapeDtypeStruct((2, T), I32),
            jax.ShapeDtypeStruct((2, T), F32),
        ],
        compiler_params=_cparams(("arbitrary",)),
        name="outproj_router",
    )(o_sb, o_gdn, w_o, w_o, x, norm_w, wr_t, br)


def _plan_kernel(eid_ref, dest_ref, blk_ref, cnt_ref, *, tl, n_blk_pad):
    p = pl.program_id(0)
    i = pl.program_id(1)
    NE = N_EXPERTS
    e0 = eid_ref[0:1, :]
    e1 = eid_ref[1:2, :]
    sub = lax.broadcasted_iota(I32, (NE, tl), 0)
    hot0 = sub == e0
    hot1 = sub == e1
    onehot = jnp.logical_or(hot0, hot1).astype(BF16)
    ones = jnp.ones((tl, HEAD_DIM), BF16)

    @pl.when(jnp.logical_and(p == 0, i == 0))
    def _():
        cnt_ref[0] = jnp.zeros((NE, HEAD_DIM), F32)

    @pl.when(p == 0)
    def _():
        cnt_ref[0] += _dot(onehot, ones)

    @pl.when(jnp.logical_and(p == 1, i == 0))
    def _():
        cnt = cnt_ref[0]
        padded = jnp.floor((cnt + (MOE_BLOCK - 1)) * (1.0 / MOE_BLOCK)) * MOE_BLOCK
        er = lax.broadcasted_iota(I32, (NE, NE), 0)
        ec = lax.broadcasted_iota(I32, (NE, NE), 1)
        start = _dot3((ec < er).astype(F32), padded)
        cnt_ref[1] = start
        end_col = (start + padded)[:, 0:1]
        pos = (lax.broadcasted_iota(I32, (NE, n_blk_pad), 1) * MOE_BLOCK).astype(F32)
        n_before = jnp.sum((end_col <= pos).astype(I32), axis=0, keepdims=True)
        blk_ref[0:1, :] = jnp.minimum(n_before, NE - 1)
        total = jnp.max(end_col, axis=0, keepdims=True)
        blk_ref[1:2, :] = jnp.broadcast_to((total * (1.0 / MOE_BLOCK)).astype(I32), (1, n_blk_pad))
        cnt_ref[0] = jnp.zeros((NE, HEAD_DIM), F32)

    @pl.when(p == 1)
    def _():
        r = lax.broadcasted_iota(I32, (tl, tl), 0)
        c = lax.broadcasted_iota(I32, (tl, tl), 1)
        before = (r < c).astype(BF16)
        run = cnt_ref[0][:, 0:1] + cnt_ref[1][:, 0:1]
        slot = _dot(onehot, before) + run
        dest_ref[0:1, :] = jnp.sum(jnp.where(hot0, slot, 0.0), axis=0, keepdims=True).astype(I32)
        dest_ref[1:2, :] = jnp.sum(jnp.where(hot1, slot, 0.0), axis=0, keepdims=True).astype(I32)
        cnt_ref[0] += _dot(onehot, ones)


def _plan(eid, *, tl, n_blk_pad):
    T = eid.shape[1]
    kern = functools.partial(_plan_kernel, tl=tl, n_blk_pad=n_blk_pad)
    return pl.pallas_call(
        kern,
        grid=(2, T // tl),
        in_specs=[pl.BlockSpec((2, tl), lambda p, i: (0, i))],
        out_specs=[
            pl.BlockSpec((2, tl), lambda p, i: (0, i * p)),
            pl.BlockSpec((2, n_blk_pad), lambda p, i: (0, 0)),
        ],
        out_shape=[
            jax.ShapeDtypeStruct((2, T), I32),
            jax.ShapeDtypeStruct((2, n_blk_pad), I32),
        ],
        scratch_shapes=[pltpu.VMEM((2, N_EXPERTS, HEAD_DIM), F32)],
        compiler_params=_cparams(("arbitrary", "arbitrary")),
        name="plan",
    )(eid)


def _scatter_kernel(dest_ref, blk_ref, h_ref, xs_ref, zero_ref, sem, zsem, *, tt, n_blocks, R):
    i = pl.program_id(0)
    blk_rows = MOE_BLOCK * R

    @pl.when(i == 0)
    def _():
        zero_ref[...] = jnp.zeros_like(zero_ref)
        n_valid = blk_ref[1, 0]

        def clear(b, n):
            last = blk_ref[0, b] != blk_ref[0, jnp.minimum(b + 1, n_blocks - 1)]
            do = jnp.logical_or(b >= n_valid - 1, last)

            @pl.when(do)
            def _():
                pltpu.make_async_copy(zero_ref, xs_ref.at[pl.ds(pl.multiple_of(b * blk_rows, blk_rows), blk_rows)],
                                      zsem).start()
            return n + do.astype(I32)
        n_started = lax.fori_loop(0, n_blocks, clear, 0)

        def drain(b, carry):
            pltpu.make_async_copy(zero_ref, xs_ref.at[pl.ds(0, blk_rows)], zsem).wait()
            return carry
        lax.fori_loop(0, n_started, drain, 0)

    def issue(r, carry):
        src = h_ref.at[pl.ds(pl.multiple_of(r * R, R), R)]
        for kk in range(2):
            dst = xs_ref.at[pl.ds(pl.multiple_of(dest_ref[kk, r] * R, R), R)]
            pltpu.make_async_copy(src, dst, sem).start(priority=kk)
        return carry
    lax.fori_loop(0, tt, issue, 0, unroll=4)

    for kk in range(2):
        pltpu.make_async_copy(h_ref, xs_ref.at[pl.ds(0, tt * R)], sem).wait()


def _scatter_rows(dest, blk, h_packed, *, cap, tt, R):
    T = h_packed.shape[0] // R
    n_blocks = cap // MOE_BLOCK
    kern = functools.partial(_scatter_kernel, tt=tt, n_blocks=n_blocks, R=R)
    return pl.pallas_call(
        kern,
        grid=(T // tt,),
        in_specs=[
            pl.BlockSpec((2, tt), lambda i: (0, i), memory_space=pltpu.SMEM),
            pl.BlockSpec(memory_space=pltpu.SMEM),
            pl.BlockSpec((tt * R, HEAD_DIM), lambda i: (i, 0)),
        ],
        out_specs=pl.BlockSpec(memory_space=pl.ANY),
        out_shape=jax.ShapeDtypeStruct((cap * R, HEAD_DIM), U32),
        scratch_shapes=[pltpu.VMEM((MOE_BLOCK * R, HEAD_DIM), U32), pltpu.SemaphoreType.DMA,
                        pltpu.SemaphoreType.DMA],
        compiler_params=_cparams(("arbitrary",)),
        name="scatter_rows",
    )(dest, blk, h_packed)


def _expert_kernel(blk_ref, x_ref, w1_hbm, w3_hbm, w2_hbm, y_ref, wf1_ref, wf3_ref, wf2_ref,
                   w1b_ref, w3b_ref, w2b_ref, ord_ref, sems, *, layer):
    b = pl.program_id(0)
    n_valid = blk_ref[1, 0]
    valid = b < n_valid
    e = blk_ref[0, b]
    new_expert = jnp.logical_or(b == 0, e != blk_ref[0, jnp.maximum(b - 1, 0)])

    def copies(expert, slot):
        return [pltpu.make_async_copy(w_hbm.at[layer, expert], wf_ref.at[slot], sems.at[slot])
                for w_hbm, wf_ref in ((w1_hbm, wf1_ref), (w3_hbm, wf3_ref), (w2_hbm, wf2_ref))]

    @pl.when(b == 0)
    def _():
        ord_ref[0] = 0
        for c in copies(e, 0):
            c.start()

    @pl.when(jnp.logical_and(valid, new_expert))
    def _():
        slot = lax.rem(ord_ref[0], 2)
        for c in copies(e, slot):
            c.wait()
        w1b_ref[...] = wf1_ref[slot].astype(BF16)
        w3b_ref[...] = wf3_ref[slot].astype(BF16)
        w2b_ref[...] = wf2_ref[slot].astype(BF16)
        nb = lax.while_loop(lambda j: jnp.logical_and(j < n_valid, blk_ref[0, j] == e), lambda j: j + 1, b + 1)

        @pl.when(nb < n_valid)
        def _():
            for c in copies(blk_ref[0, nb], 1 - slot):
                c.start()
        ord_ref[0] = ord_ref[0] + 1

    @pl.when(valid)
    def _():
        R = x_ref.shape[0] // MOE_BLOCK
        lo, hi = _unpack_words([x_ref[pl.ds(s, MOE_BLOCK, stride=R), :] for s in range(R)])
        x = jnp.concatenate(lo + hi, axis=1).astype(BF16)
        hid = _silu(_dot(x, w1b_ref[...])) * _dot(x, w3b_ref[...])
        y = _dot(hid.astype(BF16), w2b_ref[...])
        for s, word in enumerate(_pack_words(y)):
            y_ref[pl.ds(s, MOE_BLOCK, stride=R), :] = word

    @pl.when(b >= blk_ref[1, 0])
    def _():
        y_ref[...] = jnp.zeros_like(y_ref)


def _experts(blk, x_slots, w1, w3, w2, *, layer, R):
    cap = x_slots.shape[0] // R
    n_blocks = cap // MOE_BLOCK
    D, F = w1.shape[2], w1.shape[3]

    assert blk.shape[1] > n_blocks

    def row_blk(b, blk_ref):
        return jnp.minimum(b, blk_ref[1, 0] - 1)

    grid_spec = pltpu.PrefetchScalarGridSpec(
        num_scalar_prefetch=1,
        grid=(n_blocks,),
        in_specs=[
            pl.BlockSpec((MOE_BLOCK * R, HEAD_DIM), lambda b, blk_ref: (row_blk(b, blk_ref), 0)),
            pl.BlockSpec(memory_space=pl.ANY),
            pl.BlockSpec(memory_space=pl.ANY),
            pl.BlockSpec(memory_space=pl.ANY),
        ],
        out_specs=pl.BlockSpec((MOE_BLOCK * R, HEAD_DIM), lambda b, blk_ref: (b, 0)),
        scratch_shapes=[pltpu.VMEM((2, D, F), F32), pltpu.VMEM((2, D, F), F32), pltpu.VMEM((2, F, D), F32),
                        pltpu.VMEM((D, F), BF16), pltpu.VMEM((D, F), BF16), pltpu.VMEM((F, D), BF16),
                        pltpu.SMEM((1,), I32), pltpu.SemaphoreType.DMA((2,))],
    )
    return pl.pallas_call(
        functools.partial(_expert_kernel, layer=layer),
        grid_spec=grid_spec,
        out_shape=jax.ShapeDtypeStruct((cap * R, HEAD_DIM), U32),
        compiler_params=_cparams(("arbitrary",)),
        name="experts",
    )(blk, x_slots, w1, w3, w2)


def _combine_kernel(dest_ref, dnext_ref, ys_ref, x_ref, gate_ref, o_ref, buf_ref, sems, *, tt, R):
    i = pl.program_id(0)
    slot = lax.rem(i, 2)

    def issue(d_ref, s):
        def one(r, carry):
            for kk in range(2):
                src = ys_ref.at[pl.ds(pl.multiple_of(d_ref[kk, r] * R, R), R)]
                pltpu.make_async_copy(src, buf_ref.at[s, kk, pl.ds(pl.multiple_of(r * R, R), R)],
                                      sems.at[s]).start(priority=kk)
            return carry
        lax.fori_loop(0, tt, one, 0, unroll=4)

    @pl.when(i == 0)
    def _():
        issue(dest_ref, 0)

    @pl.when(i + 1 < pl.num_programs(0))
    def _():
        issue(dnext_ref, 1 - slot)

    for kk in range(2):
        pltpu.make_async_copy(ys_ref.at[pl.ds(0, tt * R)], buf_ref.at[slot, kk], sems.at[slot]).wait()
    g = gate_ref[...]
    half = R * HEAD_DIM
    for s in range(R):
        lo0, hi0 = _unpack_words([buf_ref[slot, 0, pl.ds(s, tt, stride=R), :]])
        lo1, hi1 = _unpack_words([buf_ref[slot, 1, pl.ds(s, tt, stride=R), :]])
        for off, y0, y1 in ((s * HEAD_DIM, lo0[0], lo1[0]), (half + s * HEAD_DIM, hi0[0], hi1[0])):
            cs = slice(off, off + HEAD_DIM)
            o_ref[:, cs] = x_ref[:, cs] + (y0 * g[:, 0:1] + y1 * g[:, 1:2])


def _combine(dest, y_slots, x, gate_t, *, tt, R):
    T, D = x.shape
    n = T // tt
    kern = functools.partial(_combine_kernel, tt=tt, R=R)
    return pl.pallas_call(
        kern,
        grid=(n,),
        in_specs=[
            pl.BlockSpec((2, tt), lambda i: (0, i), memory_space=pltpu.SMEM),
            pl.BlockSpec((2, tt), lambda i: (0, jnp.minimum(i + 1, n - 1)), memory_space=pltpu.SMEM),
            pl.BlockSpec(memory_space=pl.ANY),
            pl.BlockSpec((tt, D), lambda i: (i, 0)),
            pl.BlockSpec((tt, 2), lambda i: (i, 0)),
        ],
        out_specs=pl.BlockSpec((tt, D), lambda i: (i, 0)),
        out_shape=jax.ShapeDtypeStruct((T, D), F32),
        scratch_shapes=[pltpu.VMEM((2, 2, tt * R, HEAD_DIM), U32), pltpu.SemaphoreType.DMA((2,))],
        compiler_params=_cparams(("arbitrary",)),
        name="combine",
    )(dest, dest, y_slots, x, gate_t)


def _mixer(x2, batch, layer, norm_w, w_in, q_norm_w, k_norm_w, conv_w, a_log, dt_bias, out_norm_w, w_o,
           norm2_w, w_group, b_group, w_expert, b_expert, *, tm, tn, tq, tk, hp, tt, tm_out):
    T, D = x2.shape
    S = T // batch
    n_gdn = a_log.shape[0]
    gdn_width = n_gdn * HEAD_DIM
    n_main = w_in.shape[2] - 2 * n_gdn
    sb_width = (n_main - 4 * gdn_width) // 3
    n_sb = sb_width // HEAD_DIM
    reps = tn // HEAD_DIM
    qk_w = jnp.stack([jnp.tile(q_norm_w, reps) * (1.0 / math.sqrt(HEAD_DIM)), jnp.tile(k_norm_w, reps)])
    proj, g, beta = _inproj(x2, norm_w.reshape(1, D), w_in, qk_w,
                            a_log.reshape(-1, 1), dt_bias.reshape(-1, 1), layer=layer, sb_width=sb_width,
                            n_main=n_main, tm=tm, tn=tn)
    proj3 = proj.reshape(batch, S, n_main)
    o_sb = _sb_attention(proj3, n_heads=n_sb, tq=tq, tk=tk, hp=hp)
    o_gdn = _gdn(proj3, conv_w, g, beta, out_norm_w.reshape(1, HEAD_DIM), n_heads=n_gdn,
                 col0=3 * sb_width, tt=tt)
    tail = ROUTER_ROWS - 8 - N_EXPERTS
    wr_t = jnp.concatenate([w_group, jnp.zeros((D, 8 - N_GROUPS), F32), w_expert,
                            jnp.zeros((D, tail), F32)], axis=1).T
    br = jnp.concatenate([b_group, jnp.zeros((8 - N_GROUPS,), F32), b_expert,
                          jnp.zeros((tail,), F32)]).reshape(-1, 1)
    return _outproj_router(o_sb.reshape(T, sb_width), o_gdn.reshape(T, gdn_width), w_o.astype(BF16), x2,
                           norm2_w.reshape(1, D), wr_t, br, tm=tm_out)


def _moe(x2, h, eid, gate, layer, w1, w3, w2, *, tl, tt, tc):
    T, D = x2.shape
    R = D // PACK_LANES
    n_blocks = (2 * T + MOE_BLOCK - 1) // MOE_BLOCK + N_EXPERTS
    n_blk_pad = (n_blocks // 128 + 1) * 128
    dest, blk = _plan(eid, tl=tl, n_blk_pad=n_blk_pad)
    x_slots = _scatter_rows(dest, blk, h, cap=n_blocks * MOE_BLOCK, tt=tt, R=R)
    y_slots = _experts(blk, x_slots, w1, w3, w2, layer=layer, R=R)
    return _combine(dest, y_slots, x2, gate.T, tt=tc, R=R)


def _forward(x, norm1_w, w_in, sb_q_norm_w, sb_k_norm_w, gdn_conv_w, gdn_a_log, gdn_dt_bias,
             gdn_out_norm_w, w_o, norm2_w, w_group, b_group, w_expert, b_expert, w1, w3, w2, *, tiles):
    batch, S, D = x.shape
    x2 = x.reshape(batch * S, D)
    w_in = w_in.astype(BF16)
    for l in range(norm1_w.shape[0]):
        x2, h, eid, gate = _mixer(x2, batch, l, norm1_w[l], w_in, sb_q_norm_w[l], sb_k_norm_w[l],
                                  gdn_conv_w[l], gdn_a_log[l], gdn_dt_bias[l], gdn_out_norm_w[l], w_o[l],
                                  norm2_w[l], w_group[l], b_group[l], w_expert[l], b_expert[l],
                                  **tiles["mixer"])
        x2 = _moe(x2, h, eid, gate, l, w1, w3, w2, **tiles["moe"])
    return x2.reshape(batch, S, D)


_TILES = {
    "mixer": dict(tm=1024, tn=1024, tq=256, tk=256, hp=8, tt=512, tm_out=512),
    "moe": dict(tl=512, tt=512, tc=256),
}


def kernel(x, norm1_w, w_in, sb_q_norm_w, sb_k_norm_w, gdn_conv_w, gdn_a_log, gdn_dt_bias, gdn_out_norm_w, w_o, norm2_w, w_group, b_group, w_expert, b_expert, w1, w3, w2):
    return _forward(x, norm1_w, w_in, sb_q_norm_w, sb_k_norm_w, gdn_conv_w, gdn_a_log, gdn_dt_bias,
                    gdn_out_norm_w, w_o, norm2_w, w_group, b_group, w_expert, b_expert, w1, w3, w2,
                    tiles=_TILES)
```
